```python
import math
import jax
import jax.numpy as jnp
from jax import lax
import numpy as np

D_MODEL = 1024
BATCH = 8
SEQ = 2048
DEPTH = 2
DEC_BATCH = 32
DEC_SEQ = 1
PAST_LEN = 16384
PAGE_SIZE = 128

HEAD_DIM = 64
H_FOX = D_MODEL // (2 * HEAD_DIM)
H_DIFF = D_MODEL // (4 * HEAD_DIM)
N_MAPS = H_FOX + 2 * H_DIFF
D_FOX = H_FOX * HEAD_DIM
D_DIFF = 2 * H_DIFF * HEAD_DIM
EVEN_SPLITS = (D_FOX, D_FOX, D_FOX, D_DIFF, D_DIFF, D_DIFF, H_FOX)
H_NSA = D_MODEL // HEAD_DIM
KV_NSA = H_NSA // 4
GQA = H_NSA // KV_NSA
D_KV_NSA = KV_NSA * HEAD_DIM
ODD_SPLITS = (H_NSA * HEAD_DIM, 2 * D_KV_NSA, 2 * D_KV_NSA, 2 * D_KV_NSA, 3 * H_NSA)
CMP_BLOCK = 32
CMP_STRIDE = 16
CMP_R = CMP_BLOCK // CMP_STRIDE
SEL_BLOCK = 64
N_SEL = 16
WINDOW = 512
D_FF = ((8 * D_MODEL + 3 * 256 - 1) // (3 * 256)) * 256
N_BUCKETS = 32
MAX_DISTANCE = 1024
Q_BLOCK = 128
NSA_Q_BLOCK = 32
PAGES_PER_CHUNK = 8
FORGET_BIAS = 3.0
N_EVEN = (DEPTH + 1) // 2
N_ODD = DEPTH // 2
SCALE = HEAD_DIM ** -0.5
EPS = 1e-6
NEG = -1e30
FORCE = 1e9

kernel_name = "fox_diff_nsa_hybrid_step"


def rmsnorm(x, g):
    x32 = x.astype(jnp.float32)
    y = x32 * lax.rsqrt(jnp.mean(x32 * x32, axis=-1, keepdims=True) + EPS) * g.astype(jnp.float32)
    return y.astype(x.dtype)


def rel_bucket(dist):
    n = jnp.maximum(dist, 0)
    exact = N_BUCKETS // 2
    nf = jnp.maximum(n, 1).astype(jnp.float32)
    large = exact + (jnp.log(nf / exact) / math.log(MAX_DISTANCE / exact) * (N_BUCKETS - exact)).astype(jnp.int32)
    return jnp.where(n < exact, n, jnp.minimum(large, N_BUCKETS - 1))


def masked_softmax(logits, valid):
    p = jax.nn.softmax(jnp.where(valid, logits, NEG), axis=-1)
    return p * valid.astype(p.dtype)


def swiglu(h, w_in, w_out):
    gate, up = jnp.split(h @ w_in, 2, axis=-1)
    return (jax.nn.silu(gate) * up) @ w_out


def even_project(h, w_in, b_f):
    B, T, _ = h.shape
    cuts = [int(c) for c in np.cumsum(EVEN_SPLITS)[:-1]]
    q_f, k_f, v_f, q_d, k_d, v_d, f = jnp.split(h @ w_in, cuts, axis=-1)
    heads = lambda a: a.reshape(B, T, -1, HEAD_DIM)
    q = jnp.concatenate([heads(q_f), heads(q_d)], axis=2)
    k = jnp.concatenate([heads(k_f), heads(k_d)], axis=2)
    v = jnp.concatenate([heads(v_f), heads(v_d)], axis=2)
    logf = jax.nn.log_sigmoid(f.astype(jnp.float32) + b_f.astype(jnp.float32))
    return q, k, v, logf


def even_partial(q, k, v, bias):
    B, Tq = q.shape[:2]
    Tk = k.shape[1]
    logits = jnp.einsum('bqhd,bkhd->bhqk', q, k).astype(jnp.float32) * SCALE + bias
    m = jnp.max(logits, axis=-1)
    e = jnp.exp(logits - m[..., None])
    l = jnp.sum(e, axis=-1)
    v32 = v.astype(jnp.float32)
    acc_f = jnp.einsum('bhqk,bkhd->bqhd', e[:, :H_FOX], v32[:, :, :H_FOX])
    e_d = e[:, H_FOX:].reshape(B, H_DIFF, 2, Tq, Tk)
    v_d = v32[:, :, H_FOX:].reshape(B, Tk, H_DIFF, 2 * HEAD_DIM)
    acc_d = jnp.einsum('bhmqk,bkhe->bqhme', e_d, v_d)
    return m, l, acc_f, acc_d


def even_finish(m, l, acc_f, acc_d, lam, lam_init, subln_g):
    C, B, _, Tq = m.shape
    m_all = jnp.max(m, axis=0)
    s = jnp.exp(m - m_all)
    l_all = jnp.sum(s * l, axis=0).transpose(0, 2, 1)
    s = s.transpose(0, 1, 3, 2)
    o_f = jnp.sum(s[..., :H_FOX, None] * acc_f, axis=0) / l_all[..., :H_FOX, None]
    s_d = s[..., H_FOX:].reshape(C, B, Tq, H_DIFF, 2, 1)
    a_d = jnp.sum(s_d * acc_d, axis=0) / l_all[..., H_FOX:].reshape(B, Tq, H_DIFF, 2, 1)
    o_d = a_d[..., 0, :] - lam * a_d[..., 1, :]
    o_d = rmsnorm(o_d, subln_g) * (1.0 - lam_init)
    return jnp.concatenate([o_f.reshape(B, Tq, D_FOX), o_d.reshape(B, Tq, D_DIFF)], axis=-1)


def diff_lambda_value(lp, lam_init):
    lp = lp.astype(jnp.float32)
    return jnp.exp(jnp.sum(lp[0] * lp[1])) - jnp.exp(jnp.sum(lp[2] * lp[3])) + lam_init


def diff_bias(t, k_pos, rel_table):
    rb = rel_table[:, :H_DIFF][rel_bucket(t[:, None] - k_pos[None, :])]
    return jnp.repeat(rb.transpose(2, 0, 1), 2, axis=0)


def even_prompt(h, w_in, b_f, lam, lam_init, subln_g, w_o, rel_table):
    B, T, _ = h.shape
    q, k, v, logf = even_project(h, w_in, b_f)
    cum = jnp.cumsum(logf, axis=1).transpose(0, 2, 1)
    k_pos = jnp.arange(T)

    def block(i):
        qs = i * Q_BLOCK
        t = qs + jnp.arange(Q_BLOCK)
        qb = lax.dynamic_slice_in_dim(q, qs, Q_BLOCK, axis=1)
        c_q = lax.dynamic_slice_in_dim(cum, qs, Q_BLOCK, axis=2)
        fox_b = c_q[..., :, None] - cum[..., None, :]
        rb = diff_bias(t, k_pos, rel_table)
        bias = jnp.concatenate([fox_b, jnp.broadcast_to(rb[None], (B,) + rb.shape)], axis=1)
        bias = jnp.where(k_pos[None, :] <= t[:, None], bias, NEG)
        m, l, af, ad = even_partial(qb, k, v, bias)
        return even_finish(m[None], l[None], af[None], ad[None], lam, lam_init, subln_g)

    o = lax.map(block, jnp.arange(T // Q_BLOCK))
    o = o.transpose(1, 0, 2, 3).reshape(B, T, -1).astype(h.dtype)
    return o @ w_o, jnp.stack([k, v], axis=2), logf


def even_sample(h, pool_kv, pool_logf, li, page_table, w_in, b_f, lam, lam_init, subln_g, w_o, rel_table):
    DB, S, _ = h.shape
    n_pages = page_table.shape[1]
    past = n_pages * PAGE_SIZE
    q, k, v, logf = even_project(h, w_in, b_f)
    t = past + jnp.arange(S)
    c_new = jnp.cumsum(logf, axis=1).transpose(0, 2, 1)
    lf_past = pool_logf[li, page_table].reshape(DB, past, H_FOX).astype(jnp.float32)
    suffix = jnp.flip(jnp.cumsum(jnp.flip(lf_past, 1), axis=1), 1)
    r_past = jnp.concatenate([suffix[:, 1:], jnp.zeros_like(suffix[:, :1])], axis=1).transpose(0, 2, 1)
    ppc = math.gcd(n_pages, PAGES_PER_CHUNK)
    chunk_len = ppc * PAGE_SIZE

    def chunk(c):
        pages = lax.dynamic_slice_in_dim(page_table, c * ppc, ppc, axis=1)
        kv = pool_kv[li, pages].reshape(DB, chunk_len, 2, N_MAPS, HEAD_DIM)
        k_pos = c * chunk_len + jnp.arange(chunk_len)
        fox_b = c_new[..., :, None] + lax.dynamic_slice_in_dim(r_past, c * chunk_len, chunk_len, axis=2)[..., None, :]
        rb = diff_bias(t, k_pos, rel_table)
        bias = jnp.concatenate([fox_b, jnp.broadcast_to(rb[None], (DB,) + rb.shape)], axis=1)
        return even_partial(q, kv[:, :, 0], kv[:, :, 1], bias)

    parts = lax.map(chunk, jnp.arange(n_pages // ppc))
    s_pos = jnp.arange(S)
    fox_b = c_new[..., :, None] - c_new[..., None, :]
    rb = diff_bias(t, t, rel_table)
    bias = jnp.concatenate([fox_b, jnp.broadcast_to(rb[None], (DB,) + rb.shape)], axis=1)
    bias = jnp.where(s_pos[None, :] <= s_pos[:, None], bias, NEG)
    own = even_partial(q, k, v, bias)
    m, l, af, ad = [jnp.concatenate([p, o[None]], axis=0) for p, o in zip(parts, own)]
    o = even_finish(m, l, af, ad, lam, lam_init, subln_g).astype(h.dtype)
    return o @ w_o, jnp.stack([k, v], axis=2), logf


def odd_project(h, w_in, b_gate):
    B, T, _ = h.shape
    cuts = [int(c) for c in np.cumsum(ODD_SPLITS)[:-1]]
    q, c_kv, s_kv, w_kv, g = jnp.split(h @ w_in, cuts, axis=-1)
    q = q.reshape(B, T, KV_NSA, GQA, HEAD_DIM)
    kv = lambda a: a.reshape(B, T, 2, KV_NSA, HEAD_DIM)
    gates = jax.nn.sigmoid(g.astype(jnp.float32) + b_gate.astype(jnp.float32)).reshape(B, T, KV_NSA, GQA, 3)
    return q, kv(c_kv), kv(s_kv), kv(w_kv), gates


def cmp_parts(rows, w_pos):
    B, N = rows.shape[:2]
    ch = rows.reshape(B, N // CMP_STRIDE, CMP_STRIDE, 2, KV_NSA, HEAD_DIM)
    wp = w_pos.reshape(CMP_R, CMP_STRIDE, 2, KV_NSA, HEAD_DIM)
    return jnp.einsum('bcjtgd,rjtgd->rbctgd', ch, wp)


def cmp_finish(parts, b_cmp, w_out):
    n_blk = parts.shape[2] - CMP_R + 1
    pooled = sum(parts[r, :, r:r + n_blk] for r in range(CMP_R))
    z = jax.nn.silu(pooled.astype(jnp.float32) + b_cmp)
    out = jnp.einsum('bntgd,tde->bntge', z, w_out)
    return out[:, :, 0], out[:, :, 1]


def keys_attend(q, t, k, v, k_pos, valid, table_r):
    logits = jnp.einsum('bqgjd,bkgd->bgjqk', q, k).astype(jnp.float32) * SCALE
    bias = table_r[rel_bucket(t[:, None] - k_pos[None, :])]
    p = masked_softmax(logits + bias.transpose(2, 3, 0, 1), valid)
    return jnp.einsum('bgjqk,bkgd->bqgjd', p, v.astype(jnp.float32)), p


def gathered_attend(q, t, k, v, pos, table_r):
    logits = jnp.einsum('bqgjd,bgqkd->bgjqk', q, k).astype(jnp.float32) * SCALE
    g_ix = jnp.arange(KV_NSA)[None, :, None, None]
    bias = table_r[rel_bucket(t[:, None] - pos), g_ix]
    valid = (pos <= t[:, None])[:, :, None]
    p = masked_softmax(logits + jnp.moveaxis(bias, -1, 2), valid)
    return jnp.einsum('bgjqk,bgqkd->bqgjd', p, v.astype(jnp.float32))


def cover_matrix(n_blk, n_selb):
    i = np.arange(n_blk)[:, None]
    j = np.arange(n_selb)[None, :]
    m = (i * CMP_STRIDE < (j + 1) * SEL_BLOCK) & (i * CMP_STRIDE + CMP_BLOCK > j * SEL_BLOCK)
    return jnp.asarray(m.astype(np.float32))


def select_blocks(p_cmp, t, cover):
    n_selb = cover.shape[1]
    score = jnp.einsum('bgjqn,ns->bgqs', p_cmp, cover)
    j = jnp.arange(n_selb)[None, :]
    cur = (t // SEL_BLOCK)[:, None]
    score = jnp.where((j == 0) | (j == cur) | (j == cur - 1), FORCE, score)
    score = jnp.where(j * SEL_BLOCK <= t[:, None], score, NEG)
    return lax.top_k(score, min(N_SEL, n_selb))[1]


def nsa_combine(o_c, o_s, o_w, gates):
    o = gates[..., 0:1] * o_c + gates[..., 1:2] * o_s + gates[..., 2:3] * o_w
    return o.reshape(o.shape[0], o.shape[1], H_NSA * HEAD_DIM)


def odd_prompt(h, w_in, b_gate, w_pos, b_cmp, w_cout, w_o, rel_table):
    B, T, _ = h.shape
    q, c_kv, s_kv, w_kv, gates = odd_project(h, w_in, b_gate)
    table_r = rel_table.reshape(N_BUCKETS, KV_NSA, GQA)
    ck, cv = cmp_finish(cmp_parts(c_kv, w_pos), b_cmp, w_cout)
    n_blk = ck.shape[1]
    e_pos = jnp.arange(n_blk) * CMP_STRIDE + CMP_BLOCK - 1
    n_selb = T // SEL_BLOCK
    cover = cover_matrix(n_blk, n_selb)
    sel = s_kv.reshape(B, n_selb, SEL_BLOCK, 2, KV_NSA, HEAD_DIM).transpose(0, 4, 1, 2, 3, 5)
    win = jnp.pad(w_kv, ((0, 0), (WINDOW, 0), (0, 0), (0, 0), (0, 0)))
    b_ix = jnp.arange(B)[:, None, None, None]
    g_ix = jnp.arange(KV_NSA)[None, :, None, None]

    def block(i):
        qs = i * NSA_Q_BLOCK
        t = qs + jnp.arange(NSA_Q_BLOCK)
        qb = lax.dynamic_slice_in_dim(q, qs, NSA_Q_BLOCK, axis=1)
        o_c, p_c = keys_attend(qb, t, ck, cv, e_pos, e_pos[None, :] <= t[:, None], table_r)
        idx = select_blocks(p_c, t, cover)
        blk = sel[b_ix, g_ix, idx].reshape(B, KV_NSA, NSA_Q_BLOCK, -1, 2, HEAD_DIM)
        pos = (idx[..., None] * SEL_BLOCK + jnp.arange(SEL_BLOCK)).reshape(B, KV_NSA, NSA_Q_BLOCK, -1)
        o_s = gathered_attend(qb, t, blk[..., 0, :], blk[..., 1, :], pos, table_r)
        wkv = lax.dynamic_slice_in_dim(win, qs, WINDOW + NSA_Q_BLOCK, axis=1)
        wp = qs - WINDOW + jnp.arange(WINDOW + NSA_Q_BLOCK)
        d = t[:, None] - wp[None, :]
        o_w, _ = keys_attend(qb, t, wkv[:, :, 0], wkv[:, :, 1], wp, (wp[None, :] >= 0) & (d >= 0) & (d <= WINDOW), table_r)
        gb = lax.dynamic_slice_in_dim(gates, qs, NSA_Q_BLOCK, axis=1)
        return nsa_combine(o_c, o_s, o_w, gb)

    o = lax.map(block, jnp.arange(T // NSA_Q_BLOCK))
    o = o.transpose(1, 0, 2, 3).reshape(B, T, -1).astype(h.dtype)
    rows = jnp.concatenate([c_kv, s_kv], axis=2)
    return o @ w_o, rows, w_kv[:, T - min(WINDOW, T):]


def odd_sample(h, pool_kv, win_state, li, page_table, w_in, b_gate, w_pos, b_cmp, w_cout, w_o, rel_table):
    DB, S, _ = h.shape
    n_pages = page_table.shape[1]
    past = n_pages * PAGE_SIZE
    q, c_kv, s_kv, w_kv, gates = odd_project(h, w_in, b_gate)
    table_r = rel_table.reshape(N_BUCKETS, KV_NSA, GQA)
    t = past + jnp.arange(S)
    ppc = math.gcd(n_pages, PAGES_PER_CHUNK)
    chunk_len = ppc * PAGE_SIZE

    def chunk(c):
        pages = lax.dynamic_slice_in_dim(page_table, c * ppc, ppc, axis=1)
        rows = pool_kv[li, pages, :, :2].reshape(DB, chunk_len, 2, KV_NSA, HEAD_DIM)
        return cmp_parts(rows, w_pos)

    parts = lax.map(chunk, jnp.arange(n_pages // ppc))
    parts = jnp.moveaxis(parts, 0, 2).reshape(CMP_R, DB, -1, 2, KV_NSA, HEAD_DIM)
    s_pad = -(-S // CMP_STRIDE) * CMP_STRIDE
    new_parts = cmp_parts(jnp.pad(c_kv, ((0, 0), (0, s_pad - S), (0, 0), (0, 0), (0, 0))), w_pos)
    ck, cv = cmp_finish(jnp.concatenate([parts, new_parts], axis=2), b_cmp, w_cout)
    n_blk = ck.shape[1]
    e_pos = jnp.arange(n_blk) * CMP_STRIDE + CMP_BLOCK - 1
    o_c, p_c = keys_attend(q, t, ck, cv, e_pos, e_pos[None, :] <= t[:, None], table_r)
    n_past_blk = past // SEL_BLOCK
    n_new_blk = -(-S // SEL_BLOCK)
    idx = select_blocks(p_c, t, cover_matrix(n_blk, n_past_blk + n_new_blk))
    b_ix = jnp.arange(DB)[:, None, None, None]
    g_ix = jnp.arange(KV_NSA)[None, :, None, None]
    r = jnp.arange(SEL_BLOCK)
    bpp = PAGE_SIZE // SEL_BLOCK
    phys = page_table[b_ix, jnp.minimum(idx // bpp, n_pages - 1)]
    off = (idx % bpp)[..., None] * SEL_BLOCK + r
    from_pool = pool_kv[li, phys[..., None], off, 2:, g_ix[..., None]]
    new_b = jnp.pad(s_kv, ((0, 0), (0, n_new_blk * SEL_BLOCK - S), (0, 0), (0, 0), (0, 0)))
    new_b = new_b.reshape(DB, n_new_blk, SEL_BLOCK, 2, KV_NSA, HEAD_DIM)
    from_new = new_b[b_ix, jnp.clip(idx - n_past_blk, 0, n_new_blk - 1), :, :, g_ix]
    blk = jnp.where((idx >= n_past_blk)[..., None, None, None], from_new, from_pool)
    blk = blk.reshape(DB, KV_NSA, S, -1, 2, HEAD_DIM)
    pos = (idx[..., None] * SEL_BLOCK + r).reshape(DB, KV_NSA, S, -1)
    o_s = gathered_attend(q, t, blk[..., 0, :], blk[..., 1, :], pos, table_r)
    buf = win_state[li]
    wb = buf.shape[1]
    keys = jnp.concatenate([buf, w_kv], axis=1)
    wp = past - wb + jnp.arange(wb + S)
    d = t[:, None] - wp[None, :]
    o_w, _ = keys_attend(q, t, keys[:, :, 0], keys[:, :, 1], wp, (d >= 0) & (d <= WINDOW), table_r)
    o = nsa_combine(o_c, o_s, o_w, gates).astype(h.dtype)
    rows = jnp.concatenate([c_kv, s_kv], axis=2)
    return o @ w_o, rows, keys[:, keys.shape[1] - min(WINDOW, wb + S):]


def setup_inputs(seed: int = 0) -> dict:
    key = jax.random.key(seed)
    ks = jax.random.split(key, 24)
    n_pages = PAST_LEN // PAGE_SIZE
    n_used = DEC_BATCH * n_pages
    n_pool = n_used + n_used // 4
    win_buf = min(WINDOW, PAST_LEN)
    f32 = jnp.float32
    nrm = lambda k, shape, scale: scale * jax.random.normal(k, shape, f32)
    page_table = jax.random.permutation(ks[6], n_pool)[:n_used].reshape(DEC_BATCH, n_pages).astype(jnp.int32)
    return {
        "x_prompt": jax.random.normal(ks[0], (BATCH, SEQ, D_MODEL), f32),
        "x_sample": jax.random.normal(ks[1], (DEC_BATCH, DEC_SEQ, D_MODEL), f32),
        "cache_even_kv": jax.random.normal(ks[2], (N_EVEN, n_pool, PAGE_SIZE, 2, N_MAPS, HEAD_DIM), f32),
        "cache_even_logf": jax.nn.log_sigmoid(FORGET_BIAS + jax.random.normal(ks[3], (N_EVEN, n_pool, PAGE_SIZE, H_FOX), f32)),
        "cache_odd_kv": jax.random.normal(ks[4], (N_ODD, n_pool, PAGE_SIZE, 4, KV_NSA, HEAD_DIM), f32),
        "state_odd_win": jax.random.normal(ks[5], (N_ODD, DEC_BATCH, win_buf, 2, KV_NSA, HEAD_DIM), f32),
        "page_table": page_table,
        "rel_table": nrm(ks[7], (N_BUCKETS, H_NSA), 0.3),
        "norm_g": 1.0 + nrm(ks[8], (DEPTH, 4, D_MODEL), 0.02),
        "w_even_in": nrm(ks[9], (N_EVEN, D_MODEL, sum(EVEN_SPLITS)), D_MODEL ** -0.5),
        "b_even_f": FORGET_BIAS + nrm(ks[10], (N_EVEN, H_FOX), 0.1),
        "diff_lambda": nrm(ks[11], (N_EVEN, 4, HEAD_DIM), 0.1),
        "diff_subln_g": 1.0 + nrm(ks[12], (N_EVEN, 2 * HEAD_DIM), 0.02),
        "w_even_out": nrm(ks[13], (N_EVEN, D_FOX + D_DIFF, D_MODEL), (D_FOX + D_DIFF) ** -0.5),
        "w_odd_in": nrm(ks[14], (N_ODD, D_MODEL, sum(ODD_SPLITS)), D_MODEL ** -0.5),
        "b_odd_gate": nrm(ks[15], (N_ODD, 3 * H_NSA), 0.1),
        "w_cmp_pos": nrm(ks[16], (N_ODD, CMP_BLOCK, 2, KV_NSA, HEAD_DIM), CMP_BLOCK ** -0.5),
        "b_cmp": nrm(ks[17], (N_ODD, 2, KV_NSA, HEAD_DIM), 0.02),
        "w_cmp_out": nrm(ks[18], (N_ODD, 2, HEAD_DIM, HEAD_DIM), HEAD_DIM ** -0.5),
        "w_odd_out": nrm(ks[19], (N_ODD, H_NSA * HEAD_DIM, D_MODEL), (H_NSA * HEAD_DIM) ** -0.5),
        "w_ffn_in": nrm(ks[20], (DEPTH, D_MODEL, 2 * D_FF), D_MODEL ** -0.5),
        "w_ffn_out": nrm(ks[21], (DEPTH, D_FF, D_MODEL), D_FF ** -0.5),
    }


def reference(x_prompt, x_sample, cache_even_kv, cache_even_logf, cache_odd_kv, state_odd_win, page_table,
              rel_table, norm_g, w_even_in, b_even_f, diff_lambda, diff_subln_g, w_even_out,
              w_odd_in, b_odd_gate, w_cmp_pos, b_cmp, w_cmp_out, w_odd_out, w_ffn_in, w_ffn_out):
    hp, hs = x_prompt, x_sample
    ekv_p, ekv_s, elf_p, elf_s = [], [], [], []
    okv_p, okv_s, win_p, win_s = [], [], [], []
    for layer in range(DEPTH):
        g = norm_g[layer]
        li = layer // 2
        a_p, a_s = rmsnorm(hp, g[0]), rmsnorm(hs, g[0])
        if layer % 2 == 0:
            lam_init = 0.8 - 0.6 * math.exp(-0.3 * layer)
            lam = diff_lambda_value(diff_lambda[li], lam_init)
            a_p, kv, lf = even_prompt(a_p, w_even_in[li], b_even_f[li], lam, lam_init, diff_subln_g[li], w_even_out[li], rel_table)
            ekv_p.append(kv)
            elf_p.append(lf)
            a_s, kv, lf = even_sample(a_s, cache_even_kv, cache_even_logf, li, page_table, w_even_in[li], b_even_f[li],
                                      lam, lam_init, diff_subln_g[li], w_even_out[li], rel_table)
            ekv_s.append(kv)
            elf_s.append(lf)
        else:
            a_p, kv, wst = odd_prompt(a_p, w_odd_in[li], b_odd_gate[li], w_cmp_pos[li], b_cmp[li], w_cmp_out[li], w_odd_out[li], rel_table)
            okv_p.append(kv)
            win_p.append(wst)
            a_s, kv, wst = odd_sample(a_s, cache_odd_kv, state_odd_win, li, page_table, w_odd_in[li], b_odd_gate[li],
                                      w_cmp_pos[li], b_cmp[li], w_cmp_out[li], w_odd_out[li], rel_table)
            okv_s.append(kv)
            win_s.append(wst)
        hp = hp + rmsnorm(a_p, g[1])
        hs = hs + rmsnorm(a_s, g[1])
        hp = hp + rmsnorm(swiglu(rmsnorm(hp, g[2]), w_ffn_in[layer], w_ffn_out[layer]), g[3])
        hs = hs + rmsnorm(swiglu(rmsnorm(hs, g[2]), w_ffn_in[layer], w_ffn_out[layer]), g[3])
    return (hp, hs, jnp.stack(ekv_p), jnp.stack(ekv_s), jnp.stack(elf_p), jnp.stack(elf_s),
            jnp.stack(okv_p), jnp.stack(okv_s), jnp.stack(win_p), jnp.stack(win_s))
```

```python
import functools
import math

import numpy as np
import jax
import jax.numpy as jnp
from jax import lax
from jax.experimental import pallas as pl
from jax.experimental.pallas import tpu as pltpu

F32 = jnp.float32
BF16 = jnp.bfloat16
I32 = jnp.int32

HEAD_DIM = 64
LANES = 128
PAGE_SIZE = 128
CMP_BLOCK = 32
CMP_STRIDE = 16
SEL_BLOCK = 64
N_SEL = 16
WINDOW = 512
N_BUCKETS = 32
MAX_DISTANCE = 1024
FORGET_BIAS = 3.0
SCALE = HEAD_DIM ** -0.5
EPS = 1e-6
NEG = -1e30
FORCE = 1e9
PAD_SCORE = -3.0e38
VMEM_LIMIT = 56 * 1024 * 1024


def _cparams(n_axes):
    return pltpu.CompilerParams(dimension_semantics=("arbitrary",) * n_axes,
                                vmem_limit_bytes=VMEM_LIMIT)


def _dot(a, b):
    return jnp.dot(a, b, preferred_element_type=F32)


def _dot_nt(a, b):
    return lax.dot_general(a, b, (((1,), (1,)), ((), ())), preferred_element_type=F32)


def _split3(x):
    hi = x.astype(BF16)
    r1 = x - hi.astype(F32)
    mid = r1.astype(BF16)
    lo = (r1 - mid.astype(F32)).astype(BF16)
    return hi, mid, lo


def _dot_exact_rhs01(x, m01):
    hi, mid, lo = _split3(x)
    return _dot(hi, m01) + _dot(mid, m01) + _dot(lo, m01)


def _dot_exact_lhs01(m01, x):
    hi, mid, lo = _split3(x)
    return _dot(m01, hi) + _dot(m01, mid) + _dot(m01, lo)


def rel_bucket(dist):
    n = jnp.maximum(dist, 0)
    exact = N_BUCKETS // 2
    nf = jnp.maximum(n, 1).astype(F32)
    large = exact + (jnp.log(nf / exact) / math.log(MAX_DISTANCE / exact) * (N_BUCKETS - exact)).astype(I32)
    return jnp.where(n < exact, n, jnp.minimum(large, N_BUCKETS - 1))


def _proj_kernel(x_ref, g_ref, w_ref, o_ref, h_ref):
    @pl.when(pl.program_id(1) == 0)
    def _():
        x = x_ref[...]
        ms = jnp.mean(x * x, axis=-1, keepdims=True)
        h_ref[...] = (x * lax.rsqrt(ms + EPS) * g_ref[...]).astype(BF16)

    o_ref[...] = _dot(h_ref[...], w_ref[...])


def norm_proj(x, g, w, tn):
    M, D = x.shape
    N = w.shape[1]
    tm = min(M, 512)
    return pl.pallas_call(
        _proj_kernel,
        grid=(M // tm, N // tn),
        in_specs=[pl.BlockSpec((tm, D), lambda i, j: (i, 0)),
                  pl.BlockSpec((1, D), lambda i, j: (0, 0)),
                  pl.BlockSpec((D, tn), lambda i, j: (0, j))],
        out_specs=pl.BlockSpec((tm, tn), lambda i, j: (i, j)),
        out_shape=jax.ShapeDtypeStruct((M, N), F32),
        scratch_shapes=[pltpu.VMEM((tm, D), BF16)],
        compiler_params=_cparams(2),
    )(x, g.reshape(1, D), w)


def _out_res_kernel(a_ref, w_ref, g_ref, r_ref, o_ref):
    y = _dot(a_ref[...].astype(BF16), w_ref[...])
    ms = jnp.mean(y * y, axis=-1, keepdims=True)
    o_ref[...] = r_ref[...] + y * lax.rsqrt(ms + EPS) * g_ref[...]


def out_proj_residual(a, w, g, res):
    M, K = a.shape
    D = w.shape[1]
    tm = min(M, 512)
    return pl.pallas_call(
        _out_res_kernel,
        grid=(M // tm,),
        in_specs=[pl.BlockSpec((tm, K), lambda i: (i, 0)),
                  pl.BlockSpec((K, D), lambda i: (0, 0)),
                  pl.BlockSpec((1, D), lambda i: (0, 0)),
                  pl.BlockSpec((tm, D), lambda i: (i, 0))],
        out_specs=pl.BlockSpec((tm, D), lambda i: (i, 0)),
        out_shape=jax.ShapeDtypeStruct((M, D), F32),
        compiler_params=_cparams(1),
    )(a, w, g.reshape(1, D), res)


def _ffn_kernel(x_ref, g2_ref, wg_ref, wu_ref, wo_ref, g3_ref, o_ref, h_ref, acc_ref):
    f = pl.program_id(1)

    @pl.when(f == 0)
    def _():
        x = x_ref[...]
        ms = jnp.mean(x * x, axis=-1, keepdims=True)
        h_ref[...] = (x * lax.rsqrt(ms + EPS) * g2_ref[...]).astype(BF16)
        acc_ref[...] = jnp.zeros_like(acc_ref)

    h = h_ref[...]
    gate = _dot(h, wg_ref[...])
    up = _dot(h, wu_ref[...])
    act = gate * (1.0 / (1.0 + jnp.exp(-gate))) * up
    acc_ref[...] += _dot(act.astype(BF16), wo_ref[...])

    @pl.when(f == pl.num_programs(1) - 1)
    def _():
        y = acc_ref[...]
        ms = jnp.mean(y * y, axis=-1, keepdims=True)
        o_ref[...] = x_ref[...] + y * lax.rsqrt(ms + EPS) * g3_ref[...]


def ffn_residual(x, g2, w_in, w_out, g3, tf):
    M, D = x.shape
    Fdim = w_out.shape[0]
    nf = Fdim // tf
    tm = min(M, 512)
    return pl.pallas_call(
        _ffn_kernel,
        grid=(M // tm, nf),
        in_specs=[pl.BlockSpec((tm, D), lambda i, f: (i, 0)),
                  pl.BlockSpec((1, D), lambda i, f: (0, 0)),
                  pl.BlockSpec((D, tf), lambda i, f: (0, f)),
                  pl.BlockSpec((D, tf), lambda i, f: (0, nf + f)),
                  pl.BlockSpec((tf, D), lambda i, f: (f, 0)),
                  pl.BlockSpec((1, D), lambda i, f: (0, 0))],
        out_specs=pl.BlockSpec((tm, D), lambda i, f: (i, 0)),
        out_shape=jax.ShapeDtypeStruct((M, D), F32),
        scratch_shapes=[pltpu.VMEM((tm, D), BF16), pltpu.VMEM((tm, D), F32)],
        compiler_params=_cparams(2),
    )(x, g2.reshape(1, D), w_in, w_in, w_out, g3.reshape(1, D))


def _logf_kernel(f_ref, b_ref, tri_ref, lf_ref, cum_ref, carry_ref):
    @pl.when(pl.program_id(1) == 0)
    def _():
        carry_ref[...] = jnp.zeros_like(carry_ref)

    x = f_ref[...] + b_ref[...]
    lf = jnp.minimum(x, 0.0) - jnp.log(1.0 + jnp.exp(-jnp.abs(x)))
    lf_ref[...] = lf
    cum = _dot_exact_lhs01(tri_ref[...], lf) + carry_ref[...]
    cum_ref[...] = cum
    carry_ref[...] = cum[cum.shape[0] - 1:, :]


def logf_cumsum(fq, col_block, b_pad, tq):
    B, T, _ = fq.shape
    tri = jnp.asarray(np.tril(np.ones((tq, tq), np.float32)), BF16)
    return pl.pallas_call(
        _logf_kernel,
        grid=(B, T // tq),
        in_specs=[pl.BlockSpec((None, tq, LANES), lambda b, i: (b, i, col_block)),
                  pl.BlockSpec((1, LANES), lambda b, i: (0, 0)),
                  pl.BlockSpec((tq, tq), lambda b, i: (0, 0))],
        out_specs=[pl.BlockSpec((None, tq, LANES), lambda b, i: (b, i, 0)),
                   pl.BlockSpec((None, tq, LANES), lambda b, i: (b, i, 0))],
        out_shape=[jax.ShapeDtypeStruct((B, T, LANES), F32)] * 2,
        scratch_shapes=[pltpu.VMEM((1, LANES), F32)],
        compiler_params=_cparams(2),
    )(fq, b_pad, tri)


def _lambda_value(lp, lam_init):
    a = jnp.sum(lp[0:1, :] * lp[1:2, :], axis=-1, keepdims=True)
    b = jnp.sum(lp[2:3, :] * lp[3:4, :], axis=-1, keepdims=True)
    return jnp.exp(a) - jnp.exp(b) + lam_init


def _even_attn_kernel(lp_ref, q_ref, k_ref, v_ref, c_ref, tb_ref, sg_ref, o_ref, kb, vb, *, tq, n_fox, lam_init):
    c = pl.program_id(1)
    qi = pl.program_id(2)

    @pl.when(qi == 0)
    def _():
        kb[...] = k_ref[...].astype(BF16)
        vb[...] = v_ref[...].astype(BF16)

    lo = lax.broadcasted_iota(I32, (1, LANES), 1) < HEAD_DIM
    q = q_ref[...] * SCALE
    qa = jnp.where(lo, q, 0.0).astype(BF16)
    qb = jnp.where(lo, 0.0, q).astype(BF16)
    row = lax.broadcasted_iota(I32, (tq, tq), 0)
    col = lax.broadcasted_iota(I32, (tq, tq), 1)
    causal = col <= row

    def run(bias_fn):
        def step(ki, carry, diag):
            ma, la, acca, mb, lb, accb = carry
            start = pl.multiple_of(ki * tq, tq)
            k = kb[pl.ds(start, tq), :]
            v = vb[pl.ds(start, tq), :]
            ba, bb = bias_fn(ki, start)
            out = []
            for qh, bias, m, l, acc in ((qa, ba, ma, la, acca), (qb, bb, mb, lb, accb)):
                s = _dot_nt(qh, k) + bias
                if diag:
                    s = jnp.where(causal, s, NEG)
                m_new = jnp.maximum(m, jnp.max(s, axis=-1, keepdims=True))
                alpha = jnp.exp(m - m_new)
                p = jnp.exp(s - m_new)
                l = alpha * l + jnp.sum(p, axis=-1, keepdims=True)
                acc = alpha * acc + _dot(p.astype(BF16), v)
                out += [m_new, l, acc]
            return tuple(out)

        init = (jnp.full((tq, 1), NEG, F32), jnp.zeros((tq, 1), F32), jnp.zeros((tq, LANES), F32)) * 2
        carry = lax.fori_loop(0, qi, lambda ki, cr: step(ki, cr, False), init)
        return step(qi, carry, True)

    @pl.when(c < n_fox)
    def _():
        def bias_fn(ki, start):
            ck = c_ref[:, pl.ds(start, tq)]
            return -ck[0:1, :], -ck[1:2, :]

        ma, la, acca, mb, lb, accb = run(bias_fn)
        o_ref[...] = jnp.where(lo, acca / la, accb / lb)

    @pl.when(c >= n_fox)
    def _():
        def bias_fn(ki, start):
            t = tb_ref[qi - ki]
            return t, t

        ma, la, acca, mb, lb, accb = run(bias_fn)
        lam = _lambda_value(lp_ref[...], lam_init)
        o = acca / la - lam * (accb / lb)
        ms = jnp.mean(o * o, axis=-1, keepdims=True)
        o_ref[...] = o * lax.rsqrt(ms + EPS) * sg_ref[...] * (1.0 - lam_init)


def even_prompt_attention(q_arr, kv_arr, cum_pairs, toe, lam_p, subln_g, lam_init, tq):
    B, T, _ = q_arr.shape
    n_fox = 4
    n_tiles = 8
    nd = T // tq
    kern = functools.partial(_even_attn_kernel, tq=tq, n_fox=n_fox, lam_init=lam_init)
    return pl.pallas_call(
        kern,
        grid=(B, n_tiles, T // tq),
        in_specs=[pl.BlockSpec((4, HEAD_DIM), lambda b, c, i: (0, 0)),
                  pl.BlockSpec((None, tq, LANES), lambda b, c, i: (b, i, c)),
                  pl.BlockSpec((None, T, LANES), lambda b, c, i: (b, 0, c)),
                  pl.BlockSpec((None, T, LANES), lambda b, c, i: (b, 0, n_tiles + c)),
                  pl.BlockSpec((None, None, 2, T), lambda b, c, i: (b, jnp.minimum(c, n_fox - 1), 0, 0)),
                  pl.BlockSpec((None, nd, tq, tq), lambda b, c, i: (jnp.maximum(c - n_fox, 0), 0, 0, 0)),
                  pl.BlockSpec((1, LANES), lambda b, c, i: (0, 0))],
        out_specs=pl.BlockSpec((None, tq, LANES), lambda b, c, i: (b, i, c)),
        out_shape=jax.ShapeDtypeStruct((B, T, n_tiles * LANES), F32),
        scratch_shapes=[pltpu.VMEM((T, LANES), BF16), pltpu.VMEM((T, LANES), BF16)],
        compiler_params=_cparams(3),
    )(lam_p, q_arr, kv_arr, kv_arr, cum_pairs, toe, subln_g.reshape(1, LANES))


def toeplitz_tiles(bvec, tq, nd):
    i = np.arange(tq)[:, None]
    j = np.arange(tq)[None, :]
    d = np.arange(nd)[:, None, None] * tq + (i - j)[None]
    return bvec[:, np.maximum(d, 0)]


def _silu(x):
    return x * (1.0 / (1.0 + jnp.exp(-x)))


def _strided_pool(row_ref, wp_ref, n_groups, tiles_per_row, n_tiles):
    acc0, acc1 = [], []
    for lt in range(n_tiles):
        a0 = jnp.zeros((n_groups, LANES), F32)
        a1 = jnp.zeros((n_groups, LANES), F32)
        cols = slice(lt * LANES, (lt + 1) * LANES)
        for p in range(CMP_STRIDE):
            x = row_ref[pl.ds(p * tiles_per_row + lt, n_groups, stride=CMP_STRIDE * tiles_per_row), :]
            a0 = a0 + x * wp_ref[p:p + 1, cols]
            a1 = a1 + x * wp_ref[CMP_STRIDE + p:CMP_STRIDE + p + 1, cols]
        acc0.append(a0)
        acc1.append(a1)
    return acc0, acc1


def _cmp_build_kernel(c_ref, wp_ref, b_ref, w0_ref, w1_ref, ck_ref, cv_ref, *, nb, tiles_per_row):
    half = w0_ref.shape[0]
    acc0, acc1 = _strided_pool(c_ref, wp_ref, nb, tiles_per_row, 2 * half // LANES)
    acc0 = jnp.concatenate(acc0, axis=1)
    acc1 = jnp.concatenate(acc1, axis=1)
    pooled = acc0 + pltpu.roll(acc1, nb - 1, 0)
    z = _silu(pooled + b_ref[...])
    ck_ref[...] = _dot(z[:, :half].astype(BF16), w0_ref[...])
    cv_ref[...] = _dot(z[:, half:].astype(BF16), w1_ref[...])


def cmp_build(cs_arr, wp, b, w0bd, w1bd):
    B, T, c_all = cs_arr.shape
    nb = T // CMP_STRIDE
    width = wp.shape[1]
    half = width // 2
    tiles_per_row = c_all // LANES
    kern = functools.partial(_cmp_build_kernel, nb=nb, tiles_per_row=tiles_per_row)
    cs_arr = cs_arr.reshape(B, T * tiles_per_row, LANES)
    return pl.pallas_call(
        kern,
        grid=(B,),
        in_specs=[pl.BlockSpec((None, T * tiles_per_row, LANES), lambda b: (b, 0, 0)),
                  pl.BlockSpec((CMP_BLOCK, width), lambda b: (0, 0)),
                  pl.BlockSpec((1, width), lambda b: (0, 0)),
                  pl.BlockSpec((half, half), lambda b: (0, 0)),
                  pl.BlockSpec((half, half), lambda b: (0, 0))],
        out_specs=[pl.BlockSpec((None, nb, half), lambda b: (b, 0, 0))] * 2,
        out_shape=[jax.ShapeDtypeStruct((B, nb, half), F32)] * 2,
        compiler_params=_cparams(1),
    )(cs_arr, wp, b, w0bd, w1bd)


def _nsa_kernel(q_ref, ck_ref, cv_ref, sk_ref, sv_ref, wk_ref, wv_ref, gt_ref, bg_ref, cb_ref, tb_ref, cov_ref, ex_ref,
                o_ref, skb, svb, wkb, wvb, ckb, cvb, mfull, *, tq, n_blk, n_selb, n_top, win_tiles):
    qi = pl.program_id(2)

    @pl.when(qi == 0)
    def _():
        skb[...] = sk_ref[...].astype(BF16)
        svb[...] = sv_ref[...].astype(BF16)
        wkb[...] = wk_ref[...].astype(BF16)
        wvb[...] = wv_ref[...].astype(BF16)
        ckb[...] = ck_ref[...].astype(BF16)
        cvb[...] = cv_ref[...].astype(BF16)

    nb = ck_ref.shape[0]
    t0 = qi * tq
    lane = lax.broadcasted_iota(I32, (1, LANES), 1)
    lo = lane < HEAD_DIM
    tcol = t0 + lax.broadcasted_iota(I32, (tq, 1), 0)
    q = q_ref[...]
    gx = gt_ref[...] + bg_ref[...]
    gates = 1.0 / (1.0 + jnp.exp(-gx))
    n_idx = lax.broadcasted_iota(I32, (1, nb), 1)
    validc = jnp.logical_and(n_idx * CMP_STRIDE + (CMP_BLOCK - 1) <= tcol, n_idx < n_blk)
    validc4 = jnp.concatenate([validc] * 4, axis=0)
    n_selp = -(-n_selb // 8) * 8
    jrow = lax.broadcasted_iota(I32, (n_selp, 1), 0)
    kcol = lax.broadcasted_iota(I32, (1, tq), 1)

    for gl in range(2):
        keep = lo if gl == 0 else jnp.logical_not(lo)
        parts = []
        for j in range(4):
            tile = q[:, (gl * 2 + j // 2) * LANES:(gl * 2 + j // 2 + 1) * LANES]
            if j % 2 != gl:
                tile = pltpu.roll(tile, HEAD_DIM, 1)
            parts.append(jnp.where(keep, tile, 0.0) * SCALE)
        qs = jnp.concatenate(parts, axis=0).astype(BF16)

        s = _dot_nt(qs, ckb[...]) + jnp.concatenate([cb_ref[gl * 4 + j] for j in range(4)], axis=0)
        sm = jnp.where(validc4, s, NEG)
        m = jnp.max(sm, axis=-1, keepdims=True)
        e = jnp.where(validc4, jnp.exp(sm - m), 0.0)
        l = jnp.sum(e, axis=-1, keepdims=True)
        p = e / jnp.where(l > 0.0, l, 1.0)
        o_c = _dot(p.astype(BF16), cvb[...])
        psum = p[0:tq] + p[tq:2 * tq] + p[2 * tq:3 * tq] + p[3 * tq:4 * tq]
        score = _dot_exact_rhs01(psum, cov_ref[...])
        cur = tcol // SEL_BLOCK
        forced = jnp.logical_or(lane == 0, jnp.logical_or(lane == cur, lane == cur - 1))
        score = jnp.where(forced, FORCE, score)
        score = jnp.where(lane * SEL_BLOCK <= tcol, score, NEG)
        score = jnp.where(lane < n_selb, score, PAD_SCORE)
        sc_t = score.T[:n_selp]
        rank = jnp.zeros((n_selp, tq), F32)
        for i in range(n_selb):
            ri = sc_t[i:i + 1, :]
            tie = jnp.where(i < jrow, 1.0, 0.0)
            rank = rank + jnp.where(ri > sc_t, 1.0, jnp.where(ri == sc_t, tie, 0.0))
        sel_t = jnp.where(rank < n_top, 1.0, 0.0)
        if n_selp < LANES:
            sel_t = jnp.concatenate([sel_t, jnp.zeros((LANES - n_selp, tq), F32)], axis=0)
        mfull[...] = _dot(sel_t.T.astype(BF16), ex_ref[...])

        def attend(kref, vref, ki_lo, ki_hi, selected):
            def body(ki, carry):
                m, l, acc = carry
                start = pl.multiple_of(ki * tq, tq)
                ks = kref[pl.ds(start, tq), :]
                vs = vref[pl.ds(start, tq), :]
                d = qi - ki
                s = _dot_nt(qs, ks) + jnp.concatenate([tb_ref[gl * 4 + j, d] for j in range(4)], axis=0)
                dist = tcol - (start + kcol)
                if selected:
                    valid = jnp.logical_and(mfull[:, pl.ds(start, tq)] > 0.5, dist >= 0)
                else:
                    valid = jnp.logical_and(dist >= 0, dist <= WINDOW)
                valid4 = jnp.concatenate([valid] * 4, axis=0)
                sm = jnp.where(valid4, s, NEG)
                m_new = jnp.maximum(m, jnp.max(sm, axis=-1, keepdims=True))
                alpha = jnp.exp(m - m_new)
                p = jnp.where(valid4, jnp.exp(sm - m_new), 0.0)
                l = alpha * l + jnp.sum(p, axis=-1, keepdims=True)
                acc = alpha * acc + _dot(p.astype(BF16), vs)
                return m_new, l, acc

            init = (jnp.full((4 * tq, 1), NEG, F32), jnp.zeros((4 * tq, 1), F32), jnp.zeros((4 * tq, LANES), F32))
            _, l, acc = lax.fori_loop(ki_lo, ki_hi + 1, body, init)
            return acc / jnp.where(l > 0.0, l, 1.0)

        o_s = attend(skb, svb, 0, qi, True)
        o_w = attend(wkb, wvb, jnp.maximum(qi - win_tiles, 0), qi, False)

        outs = []
        for j in range(4):
            cg = (gl * 4 + j) * 3
            rows = slice(j * tq, (j + 1) * tq)
            oj = gates[:, cg:cg + 1] * o_c[rows] + gates[:, cg + 1:cg + 2] * o_s[rows] + gates[:, cg + 2:cg + 3] * o_w[rows]
            if j % 2 != gl:
                oj = pltpu.roll(oj, HEAD_DIM, 1)
            outs.append(oj)
        o_ref[:, (gl * 2) * LANES:(gl * 2 + 1) * LANES] = jnp.where(lo, outs[0], outs[1])
        o_ref[:, (gl * 2 + 1) * LANES:(gl * 2 + 2) * LANES] = jnp.where(lo, outs[2], outs[3])


def nsa_prompt_attention(q_arr, ck, cv, cs_arr, wg_arr, bg_pad, cbias, toe, cover, expand, tq):
    B, T, _ = q_arr.shape
    nb = ck.shape[1]
    nd = T // tq
    n_selb = T // SEL_BLOCK
    kern = functools.partial(_nsa_kernel, tq=tq, n_blk=nb - 1, n_selb=n_selb, n_top=min(N_SEL, n_selb),
                             win_tiles=WINDOW // tq)
    seq = lambda c: pl.BlockSpec((None, T, LANES), lambda gp, b, i: (b, 0, c + gp))
    return pl.pallas_call(
        kern,
        grid=(2, B, T // tq),
        in_specs=[pl.BlockSpec((None, tq, 4 * LANES), lambda gp, b, i: (b, i, gp)),
                  pl.BlockSpec((None, nb, LANES), lambda gp, b, i: (b, 0, gp)),
                  pl.BlockSpec((None, nb, LANES), lambda gp, b, i: (b, 0, gp)),
                  seq(4), seq(6), seq(0), seq(2),
                  pl.BlockSpec((None, tq, LANES), lambda gp, b, i: (b, i, 4 + gp)),
                  pl.BlockSpec((None, 1, LANES), lambda gp, b, i: (gp, 0, 0)),
                  pl.BlockSpec((8, tq, nb), lambda gp, b, i: (gp, i, 0)),
                  pl.BlockSpec((8, nd, tq, tq), lambda gp, b, i: (gp, 0, 0, 0)),
                  pl.BlockSpec((nb, LANES), lambda gp, b, i: (0, 0)),
                  pl.BlockSpec((LANES, T), lambda gp, b, i: (0, 0))],
        out_specs=pl.BlockSpec((None, tq, 4 * LANES), lambda gp, b, i: (b, i, gp)),
        out_shape=jax.ShapeDtypeStruct((B, T, 8 * LANES), F32),
        scratch_shapes=[pltpu.VMEM((T, LANES), BF16)] * 4 + [pltpu.VMEM((nb, LANES), BF16)] * 2
                       + [pltpu.VMEM((tq, T), F32)],
        compiler_params=_cparams(3),
    )(q_arr, ck, cv, cs_arr, cs_arr, wg_arr, wg_arr, wg_arr, bg_pad, cbias, toe, cover, expand)


def cover_matrix_np(n_blk, n_selb, rows, cols):
    i = np.arange(rows)[:, None]
    j = np.arange(cols)[None, :]
    m = (i * CMP_STRIDE < (j + 1) * SEL_BLOCK) & (i * CMP_STRIDE + CMP_BLOCK > j * SEL_BLOCK) & (i < n_blk) & (j < n_selb)
    return m.astype(np.float32)


def _even_dec_kernel(pt_ref, *refs, pps, n_steps, lam_init):
    k_refs = refs[0:pps]
    v_refs = refs[pps:2 * pps]
    lf_refs = refs[2 * pps:3 * pps]
    (qt_ref, cn_ref, tbt_ref, ut_ref, kn_ref, vn_ref, bo_ref, lp_ref, sg_ref,
     o_ref, m_ref, l_ref, acc_ref, carry_ref, lfpad) = refs[3 * pps:]
    s = pl.program_id(1)
    n_maps = qt_ref.shape[0]
    width = qt_ref.shape[1]

    @pl.when(s == 0)
    def _():
        m_ref[...] = jnp.full_like(m_ref, NEG)
        l_ref[...] = jnp.zeros_like(l_ref)
        acc_ref[...] = jnp.zeros_like(acc_ref)
        carry_ref[...] = jnp.zeros_like(carry_ref)
        lfpad[...] = jnp.zeros_like(lfpad)

    qt = qt_ref[...]
    rowi = lax.broadcasted_iota(I32, (n_maps, 1), 0)
    n_fox = lf_refs[0].shape[1]
    is_fox = rowi < n_fox
    for i in reversed(range(pps)):
        kpg = k_refs[i][...].astype(BF16)
        vpg = v_refs[i][...].astype(BF16)
        lfpad[:, 0:n_fox] = lf_refs[i][...]
        lft = lfpad[...].T[:n_maps]
        suffix = _dot_exact_rhs01(lft, ut_ref[...]) + carry_ref[...] + cn_ref[...]
        bias = jnp.where(is_fox, suffix, tbt_ref[i])
        st = _dot_nt(qt, kpg) + bias
        m = m_ref[...]
        m_new = jnp.maximum(m, jnp.max(st, axis=-1, keepdims=True))
        alpha = jnp.exp(m - m_new)
        p = jnp.exp(st - m_new)
        l_ref[...] = alpha * l_ref[...] + jnp.sum(p, axis=-1, keepdims=True)
        acc_ref[...] = alpha * acc_ref[...] + _dot(p.astype(BF16), vpg)
        m_ref[...] = m_new
        carry_ref[...] = carry_ref[...] + jnp.sum(lft, axis=-1, keepdims=True)

    @pl.when(s == n_steps - 1)
    def _():
        kn = kn_ref[...].astype(BF16).astype(F32)
        vn = vn_ref[...].astype(BF16).astype(F32)
        s_own = jnp.sum(qt.astype(F32) * kn, axis=-1, keepdims=True) + bo_ref[...]
        m = m_ref[...]
        m_all = jnp.maximum(m, s_own)
        a = jnp.exp(m - m_all)
        e_own = jnp.exp(s_own - m_all)
        l_all = a * l_ref[...] + e_own
        o = (a * acc_ref[...] + e_own * vn) / l_all
        r = lax.broadcasted_iota(I32, (n_maps, width), 0)
        cidx = lax.broadcasted_iota(I32, (n_maps, width), 1)
        d_fox = n_fox * HEAD_DIM
        fox_sel = jnp.logical_and(cidx < d_fox, r == cidx // HEAD_DIM)
        dh = (cidx - d_fox) // LANES
        d1_sel = jnp.logical_and(cidx >= d_fox, r == n_fox + 2 * dh)
        d2_sel = jnp.logical_and(cidx >= d_fox, r == n_fox + 2 * dh + 1)
        o_f = jnp.sum(jnp.where(fox_sel, o, 0.0), axis=0, keepdims=True)
        a1 = jnp.sum(jnp.where(d1_sel, o, 0.0), axis=0, keepdims=True)
        a2 = jnp.sum(jnp.where(d2_sel, o, 0.0), axis=0, keepdims=True)
        lam = _lambda_value(lp_ref[...], lam_init)
        o_d = a1 - lam * a2
        for tl in range(width // LANES):
            sl = slice(tl * LANES, (tl + 1) * LANES)
            if tl * LANES < d_fox:
                o_ref[:, sl] = o_f[:, sl]
            else:
                x = o_d[:, sl]
                ms = jnp.mean(x * x, axis=-1, keepdims=True)
                o_ref[:, sl] = x * lax.rsqrt(ms + EPS) * sg_ref[...] * (1.0 - lam_init)


def even_decode(pool_kv, pool_lf, li, pt_flat, n_pages, qt, cn, tbt, kn, vn, bo, lam_p, subln_g, lam_init, pps):
    DB, n_maps, width = qt.shape
    n_fox = pool_lf.shape[-1]
    n_steps = n_pages // pps
    ut = jnp.asarray(np.tril(np.ones((PAGE_SIZE, PAGE_SIZE), np.float32), -1), BF16)

    def page(i):
        return lambda b, s, pt: pt[b * n_pages + (n_steps - 1 - s) * pps + i]

    k_specs = [pl.BlockSpec((None, None, PAGE_SIZE, width), lambda b, s, pt, f=page(i): (li, f(b, s, pt), 0, 0)) for i in range(pps)]
    v_specs = [pl.BlockSpec((None, None, PAGE_SIZE, width), lambda b, s, pt, f=page(i): (li, f(b, s, pt), 0, 1)) for i in range(pps)]
    lf_specs = [pl.BlockSpec((None, None, PAGE_SIZE, n_fox), lambda b, s, pt, f=page(i): (li, f(b, s, pt), 0, 0)) for i in range(pps)]
    per_b = lambda shape: pl.BlockSpec((None,) + shape, lambda b, s, pt: (b,) + (0,) * len(shape))
    const = lambda shape: pl.BlockSpec(shape, lambda b, s, pt: (0,) * len(shape))
    grid_spec = pltpu.PrefetchScalarGridSpec(
        num_scalar_prefetch=1,
        grid=(DB, n_steps),
        in_specs=k_specs + v_specs + lf_specs + [
            per_b((n_maps, width)), per_b((n_maps, 1)),
            pl.BlockSpec((pps, n_maps, PAGE_SIZE), lambda b, s, pt: (n_steps - 1 - s, 0, 0)),
            const((PAGE_SIZE, PAGE_SIZE)),
            per_b((1, width)), per_b((1, width)),
            const((n_maps, 1)), const((4, HEAD_DIM)), const((1, LANES))],
        out_specs=per_b((1, width)),
        scratch_shapes=[pltpu.VMEM((n_maps, 1), F32), pltpu.VMEM((n_maps, 1), F32), pltpu.VMEM((n_maps, width), F32),
                        pltpu.VMEM((n_maps, 1), F32), pltpu.VMEM((PAGE_SIZE, LANES), F32)])
    kern = functools.partial(_even_dec_kernel, pps=pps, n_steps=n_steps, lam_init=lam_init)
    return pl.pallas_call(
        kern, grid_spec=grid_spec,
        out_shape=jax.ShapeDtypeStruct((DB, 1, width), F32),
        compiler_params=_cparams(2),
    )(pt_flat, *([pool_kv] * (2 * pps)), *([pool_lf] * pps), qt, cn, tbt, ut, kn, vn, bo, lam_p, subln_g.reshape(1, LANES))


def _odd_cmp_kernel(pt_ref, *refs, pps, n_steps, n_selb, n_top, t_pos):
    r_refs = refs[0:pps]
    (wp_ref, b_ref, w0_ref, w1_ref, qz_ref, cb_ref, cov_ref, oc_ref, idx_ref, p0, p1) = refs[pps:]
    s = pl.program_id(1)
    width = wp_ref.shape[1]
    half = width // 2
    spp = PAGE_SIZE // CMP_STRIDE
    tiles_per_row = r_refs[0].shape[0] // PAGE_SIZE
    for i in range(pps):
        acc0, acc1 = _strided_pool(r_refs[i], wp_ref, spp, tiles_per_row, width // LANES)
        base = pl.multiple_of((s * pps + i) * spp, spp)
        for lt in range(width // LANES):
            p0[pl.ds(base, spp), lt * LANES:(lt + 1) * LANES] = acc0[lt]
            p1[pl.ds(base, spp), lt * LANES:(lt + 1) * LANES] = acc1[lt]

    @pl.when(s == n_steps - 1)
    def _():
        nb = p0.shape[0]
        pooled = p0[...] + pltpu.roll(p1[...], nb - 1, 0)
        z = _silu(pooled + b_ref[...])
        ck = _dot(z[:, :half].astype(BF16), w0_ref[...]).astype(BF16)
        cv = _dot(z[:, half:].astype(BF16), w1_ref[...]).astype(BF16)
        st = _dot_nt(qz_ref[...], ck) + cb_ref[...]
        n_idx = lax.broadcasted_iota(I32, (1, nb), 1)
        valid = n_idx * CMP_STRIDE + (CMP_BLOCK - 1) <= t_pos
        sm = jnp.where(valid, st, NEG)
        m = jnp.max(sm, axis=-1, keepdims=True)
        e = jnp.where(valid, jnp.exp(sm - m), 0.0)
        l = jnp.sum(e, axis=-1, keepdims=True)
        p = e / jnp.where(l > 0.0, l, 1.0)
        oc_ref[...] = _dot(p.astype(BF16), cv)
        n_heads = p.shape[0]
        gqa = n_heads // 4
        rows = [jnp.sum(p[g * gqa:(g + 1) * gqa], axis=0, keepdims=True) for g in range(4)]
        psum = jnp.concatenate(rows + rows, axis=0)
        score = _dot_exact_rhs01(psum, cov_ref[...])
        nsp = score.shape[1]
        jb = lax.broadcasted_iota(I32, (1, nsp), 1)
        cur = t_pos // SEL_BLOCK
        forced = jnp.logical_or(jb == 0, jnp.logical_or(jb == cur, jb == cur - 1))
        score = jnp.where(forced, FORCE, score)
        score = jnp.where(jb * SEL_BLOCK <= t_pos, score, NEG)
        score = jnp.where(jb < n_selb, score, PAD_SCORE)
        jf = jb.astype(F32)
        slot = lax.broadcasted_iota(I32, (1, LANES), 1)
        picks = jnp.zeros((8, LANES), F32)
        for it in range(n_top):
            mx = jnp.max(score, axis=-1, keepdims=True)
            ix = jnp.min(jnp.where(score == mx, jf, 1e9), axis=-1, keepdims=True)
            picks = jnp.where(slot == it, ix, picks)
            score = jnp.where(jf == ix, PAD_SCORE, score)
        idx_ref[...] = picks.astype(I32)


def odd_decode_cmp(pool, li, pt_flat, n_pages, wp, b, w0bd, w1bd, qz16, cbias, cover, n_selb, t_pos, pps):
    DB = qz16.shape[0]
    width = wp.shape[1]
    half = width // 2
    n_steps = n_pages // pps
    nb = n_pages * PAGE_SIZE // CMP_STRIDE
    nsp = cover.shape[1]
    page_rows = pool.shape[2]
    r_specs = [pl.BlockSpec((None, None, page_rows, LANES), lambda b, s, pt, i=i: (li, pt[b * n_pages + s * pps + i], 0, 0))
               for i in range(pps)]
    per_b = lambda shape: pl.BlockSpec((None,) + shape, lambda b, s, pt: (b,) + (0,) * len(shape))
    const = lambda shape: pl.BlockSpec(shape, lambda b, s, pt: (0,) * len(shape))
    grid_spec = pltpu.PrefetchScalarGridSpec(
        num_scalar_prefetch=1,
        grid=(DB, n_steps),
        in_specs=r_specs + [const((CMP_BLOCK, width)), const((1, width)), const((half, half)), const((half, half)),
                            per_b((16, half)), const((16, nb)), const((nb, nsp))],
        out_specs=[per_b((16, half)), per_b((8, LANES))],
        scratch_shapes=[pltpu.VMEM((nb, width), F32), pltpu.VMEM((nb, width), F32)])
    kern = functools.partial(_odd_cmp_kernel, pps=pps, n_steps=n_steps, n_selb=n_selb, n_top=min(N_SEL, n_selb), t_pos=t_pos)
    return pl.pallas_call(
        kern, grid_spec=grid_spec,
        out_shape=[jax.ShapeDtypeStruct((DB, 16, half), F32), jax.ShapeDtypeStruct((DB, 8, LANES), I32)],
        compiler_params=_cparams(2),
    )(pt_flat, *([pool] * pps), wp, b, w0bd, w1bd, qz16, cbias, cover)


def _odd_sel_kernel(idx_ref, pt_ref, *refs, n_top, n_past_blk, t_pos):
    blk_refs = refs[0:n_top]
    (qz_ref, sn_ref, tb_ref, wk_ref, wv_ref, wn_ref, tw_ref, tw0_ref, oc_ref, gt_ref, o_ref) = refs[n_top:]
    b = pl.program_id(0)
    g = pl.program_id(1)
    pair = pl.multiple_of((g // 2) * LANES, LANES)
    half_w = sn_ref.shape[1] // 2
    vpair = pl.multiple_of(half_w + (g // 2) * LANES, LANES)
    qz = qz_ref[...]
    r64 = lax.broadcasted_iota(I32, (1, SEL_BLOCK), 1)
    row0 = lax.broadcasted_iota(I32, (SEL_BLOCK, 1), 0) == 0
    knew = jnp.where(row0, sn_ref[:, pl.ds(pair, LANES)], 0.0).astype(BF16)
    vnew = jnp.where(row0, sn_ref[:, pl.ds(vpair, LANES)], 0.0).astype(BF16)

    logits, vals, valids = [], [], []
    for k in range(n_top):
        blk = idx_ref[(b * 4 + g) * n_top + k]
        is_new = blk >= n_past_blk
        kt = jnp.where(is_new, knew, blk_refs[k][:, pl.ds(pair, LANES)].astype(BF16))
        vt = jnp.where(is_new, vnew, blk_refs[k][:, pl.ds(vpair, LANES)].astype(BF16))
        bias = tb_ref[pl.ds(pl.multiple_of((blk * 4 + g) * 8, 8), 8), :]
        s = _dot_nt(qz, kt) + bias[:, :SEL_BLOCK]
        valid = blk * SEL_BLOCK + r64 <= t_pos
        logits.append(jnp.where(valid, s, NEG))
        vals.append(vt)
        valids.append(valid)
    m = logits[0].max(axis=-1, keepdims=True)
    for s in logits[1:]:
        m = jnp.maximum(m, s.max(axis=-1, keepdims=True))
    l = jnp.zeros((8, 1), F32)
    acc = jnp.zeros((8, LANES), F32)
    for s, vt, valid in zip(logits, vals, valids):
        p = jnp.where(valid, jnp.exp(s - m), 0.0)
        l = l + jnp.sum(p, axis=-1, keepdims=True)
        acc = acc + _dot(p.astype(BF16), vt)
    o_s = acc / jnp.where(l > 0.0, l, 1.0)

    kw = wk_ref[...].astype(BF16)
    vw = wv_ref[...].astype(BF16)
    sw = _dot_nt(qz, kw) + tw_ref[...]
    kwn = wn_ref[:, pl.ds(pair, LANES)].astype(BF16).astype(F32)
    vwn = wn_ref[:, pl.ds(vpair, LANES)].astype(BF16).astype(F32)
    s_new = jnp.sum(qz.astype(F32) * kwn, axis=-1, keepdims=True) + tw0_ref[:, 0:1]
    mw = jnp.maximum(jnp.max(sw, axis=-1, keepdims=True), s_new)
    pw = jnp.exp(sw - mw)
    p_new = jnp.exp(s_new - mw)
    lw = jnp.sum(pw, axis=-1, keepdims=True) + p_new
    o_w = (_dot(pw.astype(BF16), vw) + p_new * vwn) / lw

    gates = 1.0 / (1.0 + jnp.exp(-gt_ref[...]))
    o_ref[...] = gates[:, 0:1] * oc_ref[...] + gates[:, 1:2] * o_s + gates[:, 2:3] * o_w


def odd_decode_sel(pool5, li, idx_flat, pt_flat, n_pages, win, qz, s_new, tb3, w_new, tw, tw0, oc_g, graw, n_top, t_pos):
    DB = qz.shape[0]
    bpp = pool5.shape[2]
    width = pool5.shape[-1] // 2
    wb = win.shape[2]
    n_past_blk = n_pages * bpp

    def blk_spec(k):
        def imap(b, g, idx, pt):
            blk = idx[(b * 4 + g) * n_top + k]
            page = pt[b * n_pages + jnp.minimum(blk // bpp, n_pages - 1)]
            return (li, page, blk % bpp, 0, 1)
        return pl.BlockSpec((None, None, None, SEL_BLOCK, width), imap)

    per_bg = lambda shape: pl.BlockSpec((None, None) + shape, lambda b, g, idx, pt: (b, g) + (0,) * len(shape))
    per_b = lambda shape: pl.BlockSpec((None,) + shape, lambda b, g, idx, pt: (b,) + (0,) * len(shape))
    grid_spec = pltpu.PrefetchScalarGridSpec(
        num_scalar_prefetch=2,
        grid=(DB, 4),
        in_specs=[blk_spec(k) for k in range(n_top)] + [
            per_bg((8, LANES)), per_b((1, width)),
            pl.BlockSpec(tb3.shape, lambda b, g, idx, pt: (0, 0)),
            pl.BlockSpec((None, None, wb, LANES), lambda b, g, idx, pt: (li, b, 0, g // 2)),
            pl.BlockSpec((None, None, wb, LANES), lambda b, g, idx, pt: (li, b, 0, 2 + g // 2)),
            per_b((1, width)),
            pl.BlockSpec((8, wb), lambda b, g, idx, pt: (g, 0)),
            pl.BlockSpec((8, LANES), lambda b, g, idx, pt: (g, 0)),
            per_bg((8, LANES)), per_bg((8, LANES))],
        out_specs=per_bg((8, LANES)))
    kern = functools.partial(_odd_sel_kernel, n_top=n_top, n_past_blk=n_past_blk, t_pos=t_pos)
    return pl.pallas_call(
        kern, grid_spec=grid_spec,
        out_shape=jax.ShapeDtypeStruct((DB, 4, 8, LANES), F32),
        compiler_params=_cparams(2),
    )(idx_flat, pt_flat, *([pool5] * n_top), qz, s_new, tb3, win, win, w_new, tw, tw0, oc_g, graw)


def _pad_cols(w, n):
    return jnp.pad(w, ((0, 0), (0, n - w.shape[1])))


def _group_half_place(x4, scale=1.0):
    DB = x4.shape[0]
    out = jnp.zeros((DB, 4, 8, LANES), F32)
    for g in range(4):
        h = (g % 2) * HEAD_DIM
        out = out.at[:, g, 0:4, h:h + HEAD_DIM].set(x4[:, g] * scale)
    return out


def even_layer(hp, hs, pool_kv, pool_lf, li, page_table, g0, w_in, b_f, lam_p, subln_g, w_o, rel_table, layer):
    B, T, D = hp.shape
    DB, S, _ = hs.shape
    assert S == 1
    n_fox, d_fox = 8, 512
    lam_init = 0.8 - 0.6 * math.exp(-0.3 * layer)
    cuts = np.cumsum((d_fox,) * 6)
    q_f, k_f, v_f, q_d, k_d, v_d, f_w = jnp.split(w_in, [int(c) for c in cuts], axis=1)
    w_q = jnp.concatenate([q_f, q_d, _pad_cols(f_w, LANES)], axis=1).astype(BF16)
    w_kv = jnp.concatenate([k_f, k_d, v_f, v_d], axis=1).astype(BF16)
    bf_pad = jnp.pad(b_f.astype(F32), (0, LANES - n_fox)).reshape(1, LANES)
    bvec = rel_table[rel_bucket(jnp.arange(T)), :4].T

    x2 = hp.reshape(B * T, D)
    q_arr = norm_proj(x2, g0, w_q, 384).reshape(B, T, -1)
    kv_arr = norm_proj(x2, g0, w_kv, 512).reshape(B, T, -1)
    tq = min(T, 256)
    lf_pad, cum_pad = logf_cumsum(q_arr, 8, bf_pad, tq)
    cum_pairs = cum_pad[:, :, :n_fox].transpose(0, 2, 1).reshape(B, 4, 2, T)
    toe = toeplitz_tiles(bvec, tq, T // tq)
    attn_p = even_prompt_attention(q_arr, kv_arr, cum_pairs, toe, lam_p, subln_g, lam_init, tq)
    ekv_p = kv_arr.reshape(B, T, 2, 16, HEAD_DIM)
    elf_p = lf_pad[:, :, :n_fox]

    n_pages = page_table.shape[1]
    past = n_pages * PAGE_SIZE
    xs = hs.reshape(DB, D)
    qs_arr = norm_proj(xs, g0, w_q, 384)
    kvs_arr = norm_proj(xs, g0, w_kv, 512)
    lfs_pad, _ = logf_cumsum(qs_arr.reshape(1, DB, -1), 8, bf_pad, DB)
    lf_new = lfs_pad[0, :, :n_fox]
    seg = np.zeros((16, 1024), np.float32)
    for m_ in range(16):
        seg[m_, m_ * HEAD_DIM:(m_ + 1) * HEAD_DIM] = 1.0
    qt = (qs_arr[:, None, :1024] * SCALE * seg[None]).astype(BF16)
    cn = jnp.pad(lf_new, ((0, 0), (0, 8)))[:, :, None]
    k_pos = jnp.arange(past)
    rb = rel_table[rel_bucket(past - k_pos), :4]
    rb2 = jnp.repeat(rb, 2, axis=1)
    tbt = jnp.concatenate([jnp.zeros((past, 8), F32), rb2], axis=1).reshape(n_pages, PAGE_SIZE, 16).transpose(0, 2, 1)
    bo = jnp.concatenate([jnp.zeros((8,), F32), jnp.repeat(rel_table[0, :4], 2)]).reshape(16, 1)
    pool_kv4 = pool_kv.reshape(pool_kv.shape[0], pool_kv.shape[1], PAGE_SIZE, 2048)
    attn_s = even_decode(pool_kv4, pool_lf, li, page_table.reshape(-1), n_pages, qt, cn, tbt,
                         kvs_arr[:, None, :1024], kvs_arr[:, None, 1024:], bo, lam_p, subln_g, lam_init,
                         pps=math.gcd(n_pages, 8))
    ekv_s = kvs_arr.reshape(DB, 1, 2, 16, HEAD_DIM)
    elf_s = lf_new.reshape(DB, 1, n_fox)
    w_o16 = w_o.astype(BF16)
    return (attn_p.reshape(B * T, -1), attn_s.reshape(DB, -1), w_o16, ekv_p, ekv_s, elf_p, elf_s)


def odd_layer(hp, hs, pool, win_state, li, page_table, g0, w_in, b_gate, w_pos, b_cmp, w_cout, w_o, rel_table):
    B, T, D = hp.shape
    DB, S, _ = hs.shape
    assert S == 1
    d_q, d_kv = 1024, 512
    w_q, w_c, w_s, w_w, w_g = jnp.split(w_in, [d_q, d_q + d_kv, d_q + 2 * d_kv, d_q + 3 * d_kv], axis=1)
    n_gate = 24
    w_g2 = jnp.concatenate([_pad_cols(w_g[:, :n_gate], LANES), _pad_cols(w_g[:, n_gate:], LANES)], axis=1)
    w_q16 = w_q.astype(BF16)
    w_cs = jnp.concatenate([w_c, w_s], axis=1).astype(BF16)
    w_wg = jnp.concatenate([w_w, w_g2], axis=1).astype(BF16)
    bg = b_gate.astype(F32)
    bg_pad = jnp.stack([jnp.pad(bg[:n_gate], (0, LANES - n_gate)), jnp.pad(bg[n_gate:], (0, LANES - n_gate))]).reshape(2, 1, LANES)
    wp = w_pos.reshape(CMP_BLOCK, d_kv).astype(F32)
    bc = b_cmp.reshape(1, d_kv).astype(F32)
    eye4 = jnp.eye(4, dtype=F32)
    w0bd = jnp.kron(eye4, w_cout[0]).astype(BF16)
    w1bd = jnp.kron(eye4, w_cout[1]).astype(BF16)
    table = rel_table

    x2 = hp.reshape(B * T, D)
    q_arr = norm_proj(x2, g0, w_q16, 512).reshape(B, T, -1)
    cs_arr = norm_proj(x2, g0, w_cs, 512).reshape(B, T, -1)
    wg_arr = norm_proj(x2, g0, w_wg, 384).reshape(B, T, -1)
    ck, cv = cmp_build(cs_arr, wp, bc, w0bd, w1bd)
    nb = T // CMP_STRIDE
    n_blk = nb - 1
    n_selb = T // SEL_BLOCK
    tq = min(T, 128)
    t_all = jnp.arange(T)
    e_pos = jnp.arange(nb) * CMP_STRIDE + CMP_BLOCK - 1
    cbias = table[rel_bucket(t_all[:, None] - e_pos[None, :])].transpose(2, 0, 1)
    bvec = table[rel_bucket(t_all)].T
    toe = toeplitz_tiles(bvec, tq, T // tq)
    cover = jnp.asarray(cover_matrix_np(n_blk, n_selb, nb, LANES), BF16)
    expand = np.zeros((LANES, T), np.float32)
    expand[np.arange(T) // SEL_BLOCK, np.arange(T)] = 1.0
    attn_p = nsa_prompt_attention(q_arr, ck, cv, cs_arr, wg_arr, bg_pad, cbias, toe, cover, jnp.asarray(expand, BF16), tq)
    okv_p = cs_arr.reshape(B, T, 4, 4, HEAD_DIM)
    wn = min(WINDOW, T)
    win_p = wg_arr[:, T - wn:, :d_kv].reshape(B, wn, 2, 4, HEAD_DIM)

    n_pages = page_table.shape[1]
    past = n_pages * PAGE_SIZE
    xs = hs.reshape(DB, D)
    qs = norm_proj(xs, g0, w_q16, 512)
    css = norm_proj(xs, g0, w_cs, 512)
    wgs = norm_proj(xs, g0, w_wg, 384)
    pt_flat = page_table.reshape(-1)
    nbs = past // CMP_STRIDE
    n_past_blk = past // SEL_BLOCK
    n_selb_s = n_past_blk + 1
    nsp = -(-n_selb_s // LANES) * LANES
    q4 = qs.reshape(DB, 4, 4, HEAD_DIM)
    segz = np.zeros((16, 256), np.float32)
    for h_ in range(16):
        segz[h_, (h_ // 4) * HEAD_DIM:(h_ // 4 + 1) * HEAD_DIM] = 1.0
    qz16 = (jnp.tile(qs.reshape(DB, 16, 1, HEAD_DIM), (1, 1, 4, 1)).reshape(DB, 16, 256) * SCALE * segz[None]).astype(BF16)
    e_pos_s = jnp.arange(nbs) * CMP_STRIDE + CMP_BLOCK - 1
    cbias_s = table[rel_bucket(past - e_pos_s)].T
    cover_s = jnp.asarray(cover_matrix_np(nbs, n_selb_s, nbs, nsp), BF16)
    pool4 = pool.reshape(pool.shape[0], pool.shape[1], PAGE_SIZE * 1024 // LANES, LANES)
    n_top = min(N_SEL, n_selb_s)
    oc16, idx8 = odd_decode_cmp(pool4, li, pt_flat, n_pages, wp, bc, w0bd, w1bd, qz16, cbias_s, cover_s,
                                n_selb_s, past, pps=math.gcd(n_pages, 8))
    idx_flat = idx8[:, :4, :n_top].reshape(-1)
    oc4 = oc16.reshape(DB, 4, 4, 4, HEAD_DIM)[:, np.arange(4), :, np.arange(4)].transpose(1, 0, 2, 3)
    oc_g = _group_half_place(oc4)
    qz = _group_half_place(q4, SCALE).astype(BF16)
    pos_b = jnp.arange(n_selb_s)[:, None] * SEL_BLOCK + jnp.arange(SEL_BLOCK)[None, :]
    tb = table[rel_bucket(past - pos_b)]
    tb = tb.reshape(n_selb_s, SEL_BLOCK, 4, 4).transpose(0, 2, 3, 1)
    tb3 = jnp.zeros((n_selb_s, 4, 8, LANES), F32).at[:, :, :4, :SEL_BLOCK].set(tb).reshape(n_selb_s * 32, LANES)
    wb = win_state.shape[2]
    w_pos_k = past - wb + jnp.arange(wb)
    tw_ = table[rel_bucket(past - w_pos_k)].T.reshape(4, 4, wb)
    tw = jnp.zeros((4, 8, wb), F32).at[:, :4].set(tw_).reshape(32, wb)
    tw0 = jnp.zeros((4, 8, LANES), F32).at[:, :4, :].set(jnp.broadcast_to(table[0].reshape(4, 4, 1), (4, 4, LANES))).reshape(32, LANES)
    graw_all = wgs[:, d_kv:]
    g48 = jnp.concatenate([graw_all[:, :n_gate], graw_all[:, LANES:LANES + n_gate]], axis=1) + bg[None]
    graw = jnp.zeros((DB, 4, 8, LANES), F32).at[:, :, :4, :3].set(g48.reshape(DB, 4, 4, 3))
    pool5 = pool.reshape(pool.shape[0], pool.shape[1], PAGE_SIZE // SEL_BLOCK, SEL_BLOCK, 1024)
    win4 = win_state.reshape(win_state.shape[0], DB, wb, d_kv)
    o4 = odd_decode_sel(pool5, li, idx_flat, pt_flat, n_pages, win4, qz, css[:, None, d_kv:], tb3,
                        wgs[:, None, :d_kv], tw, tw0, oc_g, graw, n_top, past)
    heads = [o4[:, g, j, (g % 2) * HEAD_DIM:(g % 2 + 1) * HEAD_DIM] for g in range(4) for j in range(4)]
    attn_s = jnp.concatenate(heads, axis=-1)
    okv_s = css.reshape(DB, 1, 4, 4, HEAD_DIM)
    keys = jnp.concatenate([win_state[li], wgs[:, :d_kv].reshape(DB, 1, 2, 4, HEAD_DIM)], axis=1)
    win_s = keys[:, keys.shape[1] - min(WINDOW, wb + 1):]
    return (attn_p.reshape(B * T, -1), attn_s, w_o.astype(BF16), okv_p, okv_s, win_p, win_s)


def kernel(x_prompt, x_sample, cache_even_kv, cache_even_logf, cache_odd_kv, state_odd_win, page_table, rel_table, norm_g, w_even_in, b_even_f, diff_lambda, diff_subln_g, w_even_out, w_odd_in, b_odd_gate, w_cmp_pos, b_cmp, w_cmp_out, w_odd_out, w_ffn_in, w_ffn_out):
    B, T, D = x_prompt.shape
    DB, S, _ = x_sample.shape
    depth = norm_g.shape[0]
    hp = x_prompt.reshape(B * T, D)
    hs = x_sample.reshape(DB * S, D)
    ekv_p, ekv_s, elf_p, elf_s = [], [], [], []
    okv_p, okv_s, win_p, win_s = [], [], [], []
    for layer in range(depth):
        g = norm_g[layer]
        li = layer // 2
        hp3, hs3 = hp.reshape(B, T, D), hs.reshape(DB, S, D)
        if layer % 2 == 0:
            a_p, a_s, w_o, kvp, kvs, lfp, lfs = even_layer(
                hp3, hs3, cache_even_kv, cache_even_logf, li, page_table, g[0], w_even_in[li], b_even_f[li],
                diff_lambda[li].astype(F32), diff_subln_g[li].astype(F32), w_even_out[li], rel_table, layer)
            ekv_p.append(kvp); ekv_s.append(kvs); elf_p.append(lfp); elf_s.append(lfs)
        else:
            a_p, a_s, w_o, kvp, kvs, wp_, ws_ = odd_layer(
                hp3, hs3, cache_odd_kv, state_odd_win, li, page_table, g[0], w_odd_in[li], b_odd_gate[li],
                w_cmp_pos[li], b_cmp[li], w_cmp_out[li], w_odd_out[li], rel_table)
            okv_p.append(kvp); okv_s.append(kvs); win_p.append(wp_); win_s.append(ws_)
        hp = out_proj_residual(a_p, w_o, g[1], hp)
        hs = out_proj_residual(a_s, w_o, g[1], hs)
        w1 = w_ffn_in[layer].astype(BF16)
        w2 = w_ffn_out[layer].astype(BF16)
        hp = ffn_residual(hp, g[2], w1, w2, g[3], 256)
        hs = ffn_residual(hs, g[2], w1, w2, g[3], 256)
    return (hp.reshape(B, T, D), hs.reshape(DB, S, D), jnp.stack(ekv_p), jnp.stack(ekv_s), jnp.stack(elf_p), jnp.stack(elf_s),
            jnp.stack(okv_p), jnp.stack(okv_s), jnp.stack(win_p), jnp.stack(win_s))
```

```python
import functools
import math

import numpy as np
import jax
import jax.numpy as jnp
from jax import lax
from jax.experimental import pallas as pl
from jax.experimental.pallas import tpu as pltpu

F32 = jnp.float32
BF16 = jnp.bfloat16
I32 = jnp.int32

HEAD_DIM = 64
LANES = 128
PAGE_SIZE = 128
CMP_BLOCK = 32
CMP_STRIDE = 16
SEL_BLOCK = 64
N_SEL = 16
WINDOW = 512
N_BUCKETS = 32
MAX_DISTANCE = 1024
SCALE = HEAD_DIM ** -0.5
EPS = 1e-6
NEG = -1e30
FORCE = 1e9
PAD_SCORE = -3.0e38
VMEM_LIMIT = 56 * 1024 * 1024


def _cparams(n_axes):
    return pltpu.CompilerParams(dimension_semantics=("arbitrary",) * n_axes,
                                vmem_limit_bytes=VMEM_LIMIT)


def _dot(a, b):
    return jnp.dot(a, b, preferred_element_type=F32)


def _dot_nt(a, b):
    return lax.dot_general(a, b, (((1,), (1,)), ((), ())), preferred_element_type=F32)


def _split3(x):
    hi = x.astype(BF16)
    r1 = x - hi.astype(F32)
    mid = r1.astype(BF16)
    lo = (r1 - mid.astype(F32)).astype(BF16)
    return hi, mid, lo


def _dot_exact_rhs01(x, m01):
    hi, mid, lo = _split3(x)
    return _dot(hi, m01) + _dot(mid, m01) + _dot(lo, m01)


def _dot_exact_lhs01(m01, x):
    hi, mid, lo = _split3(x)
    return _dot(m01, hi) + _dot(m01, mid) + _dot(m01, lo)


def _lane_tile(x, n):
    return x if n == 1 else jnp.concatenate([x] * n, axis=1)


def _aligned(x, m):
    return x if isinstance(x, int) else pl.multiple_of(x, m)


def _shift_left_one_lane(x):
    n = x.shape[1] // LANES
    if n == 0:
        return pltpu.roll(x, x.shape[1] - 1, 1)
    keep = lax.broadcasted_iota(I32, (1, LANES), 1) < LANES - 1
    rolled = [pltpu.roll(x[:, c * LANES:(c + 1) * LANES], LANES - 1, 1) for c in range(n)]
    out = [jnp.where(keep, rolled[c], rolled[min(c + 1, n - 1)]) for c in range(n)]
    return out[0] if n == 1 else jnp.concatenate(out, axis=1)


def _silu(x):
    return x * (1.0 / (1.0 + jnp.exp(-x)))


def rel_bucket(dist):
    n = jnp.maximum(dist, 0)
    exact = N_BUCKETS // 2
    nf = jnp.maximum(n, 1).astype(F32)
    large = exact + (jnp.log(nf / exact) / math.log(MAX_DISTANCE / exact) * (N_BUCKETS - exact)).astype(I32)
    return jnp.where(n < exact, n, jnp.minimum(large, N_BUCKETS - 1))


def bias_table(table, dist):
    onehot = (rel_bucket(dist)[..., None] == jnp.arange(N_BUCKETS)).astype(F32)
    return jnp.einsum('...k,kh->h...', onehot, table.astype(F32), precision=lax.Precision.HIGHEST)


def _norm_rows(x, g):
    ms = jnp.mean(x * x, axis=-1, keepdims=True)
    return x * lax.rsqrt(ms + EPS) * g


def _proj_kernel(x_ref, g_ref, w_ref, o_ref, h_ref):
    @pl.when(pl.program_id(1) == 0)
    def _():
        h_ref[...] = _norm_rows(x_ref[...], g_ref[...]).astype(BF16)

    o_ref[...] = _dot(h_ref[...], w_ref[...])


def norm_proj(x, g, w, tn):
    M, D = x.shape
    N = w.shape[1]
    tm = min(M, 512)
    return pl.pallas_call(
        _proj_kernel,
        grid=(M // tm, N // tn),
        in_specs=[pl.BlockSpec((tm, D), lambda i, j: (i, 0)),
                  pl.BlockSpec((1, D), lambda i, j: (0, 0)),
                  pl.BlockSpec((D, tn), lambda i, j: (0, j))],
        out_specs=pl.BlockSpec((tm, tn), lambda i, j: (i, j)),
        out_shape=jax.ShapeDtypeStruct((M, N), F32),
        scratch_shapes=[pltpu.VMEM((tm, D), BF16)],
        compiler_params=_cparams(2),
        name="norm_proj",
    )(x, g.reshape(1, D), w)


def _proj_t_kernel(x_ref, g_ref, w_ref, o_ref, h_ref):
    @pl.when(pl.program_id(2) == 0)
    def _():
        h_ref[...] = _norm_rows(x_ref[...], g_ref[...]).astype(BF16)

    o_ref[...] = _dot_nt(w_ref[...], h_ref[...])


def norm_proj_t(x, g, w_t, tn):
    B, T, D = x.shape
    N = w_t.shape[0]
    tm = min(T, 512)
    return pl.pallas_call(
        _proj_t_kernel,
        grid=(B, T // tm, N // tn),
        in_specs=[pl.BlockSpec((None, tm, D), lambda b, i, j: (b, i, 0)),
                  pl.BlockSpec((1, D), lambda b, i, j: (0, 0)),
                  pl.BlockSpec((tn, D), lambda b, i, j: (j, 0))],
        out_specs=pl.BlockSpec((None, tn, tm), lambda b, i, j: (b, j, i)),
        out_shape=jax.ShapeDtypeStruct((B, N, T), F32),
        scratch_shapes=[pltpu.VMEM((tm, D), BF16)],
        compiler_params=_cparams(3),
        name="norm_proj_t",
    )(x, g.reshape(1, D), w_t)


def _out_res_kernel(a_ref, w_ref, g_ref, r_ref, o_ref):
    y = _dot(a_ref[...].astype(BF16), w_ref[...])
    o_ref[...] = r_ref[...] + _norm_rows(y, g_ref[...])


def out_proj_residual(a, w, g, res):
    M, K = a.shape
    D = w.shape[1]
    tm = min(M, 512)
    return pl.pallas_call(
        _out_res_kernel,
        grid=(M // tm,),
        in_specs=[pl.BlockSpec((tm, K), lambda i: (i, 0)),
                  pl.BlockSpec((K, D), lambda i: (0, 0)),
                  pl.BlockSpec((1, D), lambda i: (0, 0)),
                  pl.BlockSpec((tm, D), lambda i: (i, 0))],
        out_specs=pl.BlockSpec((tm, D), lambda i: (i, 0)),
        out_shape=jax.ShapeDtypeStruct((M, D), F32),
        compiler_params=_cparams(1),
        name="out_proj_residual",
    )(a, w, g.reshape(1, D), res)


def _ffn_kernel(x_ref, g2_ref, wg_ref, wu_ref, wo_ref, g3_ref, o_ref, h_ref, acc_ref):
    f = pl.program_id(1)

    @pl.when(f == 0)
    def _():
        h_ref[...] = _norm_rows(x_ref[...], g2_ref[...]).astype(BF16)
        acc_ref[...] = jnp.zeros_like(acc_ref)

    h = h_ref[...]
    gate = _dot(h, wg_ref[...])
    up = _dot(h, wu_ref[...])
    acc_ref[...] += _dot((_silu(gate) * up).astype(BF16), wo_ref[...])

    @pl.when(f == pl.num_programs(1) - 1)
    def _():
        o_ref[...] = x_ref[...] + _norm_rows(acc_ref[...], g3_ref[...])


def ffn_residual(x, g2, w_in, w_out, g3, tf):
    M, D = x.shape
    Fdim = w_out.shape[0]
    nf = Fdim // tf
    tm = min(M, 512)
    return pl.pallas_call(
        _ffn_kernel,
        grid=(M // tm, nf),
        in_specs=[pl.BlockSpec((tm, D), lambda i, f: (i, 0)),
                  pl.BlockSpec((1, D), lambda i, f: (0, 0)),
                  pl.BlockSpec((D, tf), lambda i, f: (0, f)),
                  pl.BlockSpec((D, tf), lambda i, f: (0, nf + f)),
                  pl.BlockSpec((tf, D), lambda i, f: (f, 0)),
                  pl.BlockSpec((1, D), lambda i, f: (0, 0))],
        out_specs=pl.BlockSpec((tm, D), lambda i, f: (i, 0)),
        out_shape=jax.ShapeDtypeStruct((M, D), F32),
        scratch_shapes=[pltpu.VMEM((tm, D), BF16), pltpu.VMEM((tm, D), F32)],
        compiler_params=_cparams(2),
        name="ffn_residual",
    )(x, g2.reshape(1, D), w_in, w_in, w_out, g3.reshape(1, D))


def _logf_kernel(f_ref, b_ref, tri_ref, lf_ref, cum_ref, carry_ref):
    @pl.when(pl.program_id(1) == 0)
    def _():
        carry_ref[...] = jnp.zeros_like(carry_ref)

    x = f_ref[...] + b_ref[...]
    lf = jnp.minimum(x, 0.0) - jnp.log(1.0 + jnp.exp(-jnp.abs(x)))
    lf_ref[...] = lf
    cum = _dot_exact_lhs01(tri_ref[...], lf) + carry_ref[...]
    cum_ref[...] = cum
    carry_ref[...] = cum[cum.shape[0] - 1:, :]


def logf_cumsum(fq, col_block, b_pad, tq):
    B, T, _ = fq.shape
    tri = jnp.asarray(np.tril(np.ones((tq, tq), np.float32)), BF16)
    return pl.pallas_call(
        _logf_kernel,
        grid=(B, T // tq),
        in_specs=[pl.BlockSpec((None, tq, LANES), lambda b, i: (b, i, col_block)),
                  pl.BlockSpec((1, LANES), lambda b, i: (0, 0)),
                  pl.BlockSpec((tq, tq), lambda b, i: (0, 0))],
        out_specs=[pl.BlockSpec((None, tq, LANES), lambda b, i: (b, i, 0)),
                   pl.BlockSpec((None, tq, LANES), lambda b, i: (b, i, 0))],
        out_shape=[jax.ShapeDtypeStruct((B, T, LANES), F32)] * 2,
        scratch_shapes=[pltpu.VMEM((1, LANES), F32)],
        compiler_params=_cparams(2),
        name="logf_cumsum",
    )(fq, b_pad, tri)


def _lambda_value(lp, lam_init):
    a = jnp.sum(lp[0:1, :] * lp[1:2, :], axis=-1, keepdims=True)
    b = jnp.sum(lp[2:3, :] * lp[3:4, :], axis=-1, keepdims=True)
    return jnp.exp(a) - jnp.exp(b) + lam_init


def _even_attn_kernel(lp_ref, q_ref, k_ref, v_ref, c_ref, tb_ref, sg_ref, o_ref, kb, vb, m_ref, l_ref, acc_ref,
                      *, tq, tk, n_fox, lam_init):
    c = pl.program_id(1)
    qi = pl.program_id(2)

    @pl.when(qi == 0)
    def _():
        kb[...] = k_ref[...].astype(BF16)
        vb[...] = v_ref[...].astype(BF16)

    lo = lax.broadcasted_iota(I32, (1, LANES), 1) < HEAD_DIM
    q = q_ref[...] * SCALE
    qh = (jnp.where(lo, q, 0.0).astype(BF16), jnp.where(lo, 0.0, q).astype(BF16))
    t0 = qi * tq
    n_full = t0 // tk
    sub = tk // tq
    nd = tb_ref.shape[0]

    def tile(kt, bias_fn, first, diag):
        start = _aligned(kt * tk, tk)
        k = kb[:, pl.ds(start, tk)]
        v = vb[:, pl.ds(start, tk)]
        if diag:
            row = t0 + lax.broadcasted_iota(I32, (tq, tk), 0)
            col = start + lax.broadcasted_iota(I32, (tq, tk), 1)
            causal = col <= row
        for h in range(2):
            s = _dot(qh[h], k) + bias_fn(h, kt, start)
            if diag:
                s = jnp.where(causal, s, NEG)
            rmax = jnp.broadcast_to(jnp.max(s, axis=-1, keepdims=True), (tq, LANES))
            if first:
                m_new = rmax
            else:
                m_prev = m_ref[h]
                m_new = jnp.maximum(m_prev, rmax)
                alpha = jnp.exp(m_prev - m_new)
            p = jnp.exp(s - _lane_tile(m_new, tk // LANES))
            rsum = jnp.broadcast_to(jnp.sum(p, axis=-1, keepdims=True), (tq, LANES))
            pv = _dot_nt(p.astype(BF16), v)
            if first:
                l_ref[h] = rsum
                acc_ref[h] = pv
            else:
                l_ref[h] = alpha * l_ref[h] + rsum
                acc_ref[h] = alpha * acc_ref[h] + pv
            m_ref[h] = m_new

    def run(bias_fn):
        @pl.when(n_full == 0)
        def _():
            tile(0, bias_fn, True, True)

        @pl.when(n_full > 0)
        def _():
            tile(0, bias_fn, True, False)

            def body(kt, carry):
                tile(kt, bias_fn, False, False)
                return carry

            lax.fori_loop(1, n_full, body, 0)
            tile(n_full, bias_fn, False, True)

    @pl.when(c < n_fox)
    def _():
        def bias_fn(h, kt, start):
            return -c_ref[h:h + 1, pl.ds(start, tk)]

        run(bias_fn)
        o_ref[...] = jnp.where(lo, acc_ref[0] / l_ref[0], acc_ref[1] / l_ref[1])

    @pl.when(c >= n_fox)
    def _():
        def bias_fn(h, kt, start):
            d0 = qi - kt * sub
            return jnp.concatenate([tb_ref[jnp.clip(d0 - cc, 0, nd - 1)] for cc in range(sub)], axis=1)

        run(bias_fn)
        lam = _lambda_value(lp_ref[...], lam_init)
        o = acc_ref[0] / l_ref[0] - lam * (acc_ref[1] / l_ref[1])
        ms = jnp.mean(o * o, axis=-1, keepdims=True)
        o_ref[...] = o * lax.rsqrt(ms + EPS) * sg_ref[...] * (1.0 - lam_init)


def even_prompt_attention(q_arr, kv_t, cum_pairs, toe, lam_p, subln_g, lam_init, tq, tk):
    B, T, _ = q_arr.shape
    n_fox = 4
    n_tiles = 8
    nd = toe.shape[1]
    kern = functools.partial(_even_attn_kernel, tq=tq, tk=tk, n_fox=n_fox, lam_init=lam_init)
    return pl.pallas_call(
        kern,
        grid=(B, n_tiles, T // tq),
        in_specs=[pl.BlockSpec((4, HEAD_DIM), lambda b, c, i: (0, 0)),
                  pl.BlockSpec((None, tq, LANES), lambda b, c, i: (b, i, c)),
                  pl.BlockSpec((None, LANES, T), lambda b, c, i: (b, c, 0)),
                  pl.BlockSpec((None, LANES, T), lambda b, c, i: (b, n_tiles + c, 0)),
                  pl.BlockSpec((None, None, 2, T), lambda b, c, i: (b, jnp.minimum(c, n_fox - 1), 0, 0)),
                  pl.BlockSpec((None, nd, tq, tq), lambda b, c, i: (jnp.maximum(c - n_fox, 0), 0, 0, 0)),
                  pl.BlockSpec((1, LANES), lambda b, c, i: (0, 0))],
        out_specs=pl.BlockSpec((None, tq, LANES), lambda b, c, i: (b, i, c)),
        out_shape=jax.ShapeDtypeStruct((B, T, n_tiles * LANES), F32),
        scratch_shapes=[pltpu.VMEM((LANES, T), BF16), pltpu.VMEM((LANES, T), BF16),
                        pltpu.VMEM((2, tq, LANES), F32), pltpu.VMEM((2, tq, LANES), F32), pltpu.VMEM((2, tq, LANES), F32)],
        compiler_params=_cparams(3),
        name="even_prompt_attention",
    )(lam_p, q_arr, kv_t, kv_t, cum_pairs, toe, subln_g.reshape(1, LANES))


def toeplitz_dist(tq, nd):
    i = np.arange(tq)[:, None]
    j = np.arange(tq)[None, :]
    return jnp.asarray(np.maximum(np.arange(nd)[:, None, None] * tq + (i - j)[None], 0), I32)


def _pool_weights_t(w_pos_rows):
    reps = LANES // CMP_STRIDE
    w0 = jnp.tile(w_pos_rows[:CMP_STRIDE].T, (1, reps))
    w1 = jnp.tile(w_pos_rows[CMP_STRIDE:].T, (1, reps))
    return w0.astype(F32), w1.astype(F32)


def _group_sum_stack(n_tiles):
    spt = LANES // CMP_STRIDE
    assert n_tiles * spt <= LANES
    m = np.zeros((n_tiles, 2 * LANES, 2 * LANES), np.float32)
    pos = np.arange(LANES)
    for i in range(n_tiles):
        m[i, pos, i * spt + pos // CMP_STRIDE] = 1.0
        m[i, LANES + pos, LANES + i * spt + pos // CMP_STRIDE] = 1.0
    return m


def _pool2(x, w0, w1, gmat2, two_pass):
    y = jnp.concatenate([x * w0, x * w1], axis=1)
    hi = y.astype(BF16)
    out = _dot(hi, gmat2)
    if two_pass:
        out = out + _dot((y - hi.astype(F32)).astype(BF16), gmat2)
    return out


def _cmp_finish_t(p0, p1, b_col, w0t, w1t):
    half = w0t.shape[0]
    z = _silu(p0 + _shift_left_one_lane(p1) + b_col)
    ck = _dot(w0t, z[:half].astype(BF16))
    cv = _dot(w1t, z[half:].astype(BF16))
    return ck, cv


def _cmp_build_kernel(c_ref, w0_ref, w1_ref, g_ref, b_ref, w0t_ref, w1t_ref, ck_ref, cv_ref):
    T = c_ref.shape[1]
    nb = ck_ref.shape[1]
    w0 = w0_ref[...]
    w1 = w1_ref[...]
    acc = None
    for lt in range(T // LANES):
        a = _pool2(c_ref[:, lt * LANES:(lt + 1) * LANES], w0, w1, g_ref[lt], True)
        acc = a if acc is None else acc + a
    ck, cv = _cmp_finish_t(acc[:, 0:nb], acc[:, LANES:LANES + nb], b_ref[...], w0t_ref[...], w1t_ref[...])
    ck_ref[...] = ck
    cv_ref[...] = cv


def cmp_build(cs_t, w0, w1, b_col, w0t, w1t):
    B, _, T = cs_t.shape
    nb = T // CMP_STRIDE
    width = w0.shape[0]
    half = width // 2
    n_tiles = T // LANES
    gmat = jnp.asarray(_group_sum_stack(n_tiles), BF16)
    const = lambda shape: pl.BlockSpec(shape, lambda b: (0,) * len(shape))
    return pl.pallas_call(
        _cmp_build_kernel,
        grid=(B,),
        in_specs=[pl.BlockSpec((None, width, T), lambda b: (b, 0, 0)),
                  const((width, LANES)), const((width, LANES)), const((n_tiles, 2 * LANES, 2 * LANES)), const((width, 1)),
                  const((half, half)), const((half, half))],
        out_specs=[pl.BlockSpec((None, half, nb), lambda b: (b, 0, 0))] * 2,
        out_shape=[jax.ShapeDtypeStruct((B, half, nb), F32)] * 2,
        compiler_params=_cparams(1),
        name="cmp_build",
    )(cs_t, w0, w1, gmat, b_col, w0t, w1t)


def _nsa_kernel(q_ref, ck_ref, cv_ref, sk_ref, sv_ref, wk_ref, wv_ref, gt_ref, bg_ref, cb_ref, tb_ref, cov_ref, ex_ref,
                o_ref, sk2, sv2, wk2, wv2, ck2, cv2, madd, p_ref, m_ref, l_ref, a_ref, acc_ref,
                *, tq, tk, n_blk, n_selb, n_top, win_chunks):
    qi = pl.program_id(2)

    @pl.when(qi == 0)
    def _():
        for src, dst in ((sk_ref, sk2), (sv_ref, sv2), (wk_ref, wk2), (wv_ref, wv2), (ck_ref, ck2), (cv_ref, cv2)):
            x = src[...].astype(BF16)
            dst[0:HEAD_DIM, :] = x
            dst[HEAD_DIM:, :] = x

    T = sk_ref.shape[1]
    nb = ck_ref.shape[1]
    nd = tb_ref.shape[1]
    sub = tk // tq
    t0 = qi * tq
    lane = lax.broadcasted_iota(I32, (1, LANES), 1)
    lo = lane < HEAD_DIM
    tcol = t0 + lax.broadcasted_iota(I32, (tq, 1), 0)
    q = q_ref[...] * SCALE
    parts = []
    for j in range(4):
        tile = q[:, (j // 2) * LANES:(j // 2 + 1) * LANES]
        parts.append(jnp.where(lo if j % 2 == 0 else jnp.logical_not(lo), tile, 0.0))
    qs = jnp.concatenate(parts, axis=0).astype(BF16)
    gates = 1.0 / (1.0 + jnp.exp(-(gt_ref[...] + bg_ref[...])))

    n_idx = lax.broadcasted_iota(I32, (1, nb), 1)
    validc = jnp.logical_and(n_idx * CMP_STRIDE + (CMP_BLOCK - 1) <= tcol, n_idx < n_blk)
    sc = _dot(qs, ck2[...])
    o_c = []
    psum = jnp.zeros((tq, nb), F32)
    for j in range(4):
        s = jnp.where(validc, sc[j * tq:(j + 1) * tq] + cb_ref[j], NEG)
        m = jnp.max(s, axis=-1, keepdims=True)
        e = jnp.where(validc, jnp.exp(s - m), 0.0)
        l = jnp.sum(e, axis=-1, keepdims=True)
        p = e / jnp.where(l > 0.0, l, 1.0)
        psum = psum + p
        o_c.append(_dot_nt(p.astype(BF16), cv2[...]))

    score = _dot_exact_rhs01(psum, cov_ref[...])
    cur = tcol // SEL_BLOCK
    forced = jnp.logical_or(lane == 0, jnp.logical_or(lane == cur, lane == cur - 1))
    score = jnp.where(forced, FORCE, score)
    score = jnp.where(lane * SEL_BLOCK <= tcol, score, NEG)
    score = jnp.where(lane < n_selb, score, PAD_SCORE)
    n_selp = -(-n_selb // 8) * 8
    sc_t = score.T[:n_selp]
    jrow = lax.broadcasted_iota(I32, (n_selp, 1), 0)
    rank = jnp.zeros((n_selp, tq), F32)
    for i in range(n_selb):
        ri = sc_t[i:i + 1, :]
        tie = jnp.where(i < jrow, 1.0, 0.0)
        rank = rank + jnp.where(ri > sc_t, 1.0, jnp.where(ri == sc_t, tie, 0.0))
    sel_t = jnp.where(rank < n_top, 1.0, 0.0)
    if n_selp < LANES:
        sel_t = jnp.concatenate([sel_t, jnp.zeros((LANES - n_selp, tq), F32)], axis=0)
    chosen = _dot(sel_t.T.astype(BF16), ex_ref[...])
    kpos = lax.broadcasted_iota(I32, (1, T), 1)
    madd[...] = jnp.where(jnp.logical_and(chosen > 0.5, kpos <= tcol), 0.0, NEG)

    def softmax_rows(j, s, first):
        rows = slice(j * tq, (j + 1) * tq)
        w = s.shape[1]
        rmax = jnp.broadcast_to(jnp.max(s, axis=-1, keepdims=True), (tq, LANES))
        if first:
            m_new = rmax
        else:
            m_prev = m_ref[rows]
            m_new = jnp.maximum(m_prev, rmax)
            alpha = jnp.exp(m_prev - m_new)
            a_ref[rows] = alpha
        p = jnp.exp(s - _lane_tile(m_new, w // LANES))
        rsum = jnp.broadcast_to(jnp.sum(p, axis=-1, keepdims=True), (tq, LANES))
        l_ref[rows] = rsum if first else alpha * l_ref[rows] + rsum
        m_ref[rows] = m_new
        p_ref[rows, 0:w] = p.astype(BF16)

    def toe_bias(j, d0, n_chunks):
        return jnp.concatenate([tb_ref[j, jnp.clip(d0 - cc, 0, nd - 1)] for cc in range(n_chunks)], axis=1)

    kt_d = qi // sub

    def sel_tile(kt, first):
        start = _aligned(kt * tk, tk)
        mask = madd[:, pl.ds(start, tk)]
        s4 = _dot(qs, sk2[:, pl.ds(start, tk)])
        for j in range(4):
            softmax_rows(j, s4[j * tq:(j + 1) * tq] + toe_bias(j, qi - kt * sub, sub) + mask, first)
        pv = _dot_nt(p_ref[:, 0:tk], sv2[:, pl.ds(start, tk)])
        acc_ref[...] = pv if first else a_ref[...] * acc_ref[...] + pv

    sel_tile(kt_d, True)

    def sel_body(i, carry):
        sel_tile(kt_d - 1 - i, False)
        return carry

    lax.fori_loop(0, kt_d, sel_body, 0)
    o_s = acc_ref[...] / l_ref[...]

    ww = win_chunks * LANES
    w0 = jnp.maximum(qi - (win_chunks - 1), 0)
    wstart = pl.multiple_of(w0 * tq, tq)
    dist = tcol - (wstart + lax.broadcasted_iota(I32, (1, ww), 1))
    wmask = jnp.where(jnp.logical_and(dist >= 0, dist <= WINDOW), 0.0, NEG)
    s4 = _dot(qs, wk2[:, pl.ds(wstart, ww)])
    for j in range(4):
        softmax_rows(j, s4[j * tq:(j + 1) * tq] + toe_bias(j, qi - w0, win_chunks) + wmask, True)
    o_w = _dot_nt(p_ref[:, 0:ww], wv2[:, pl.ds(wstart, ww)]) / l_ref[...]

    outs = []
    for j in range(4):
        rows = slice(j * tq, (j + 1) * tq)
        outs.append(gates[:, 3 * j:3 * j + 1] * o_c[j] + gates[:, 3 * j + 1:3 * j + 2] * o_s[rows]
                    + gates[:, 3 * j + 2:3 * j + 3] * o_w[rows])
    o_ref[:, 0:LANES] = jnp.where(lo, outs[0], outs[1])
    o_ref[:, LANES:2 * LANES] = jnp.where(lo, outs[2], outs[3])


def nsa_prompt_attention(qg_arr, ck_t, cv_t, cs_t, w_t, bg_pad, cbias, toe, cover, expand, tq, tk):
    B, T, _ = qg_arr.shape
    nb = ck_t.shape[2]
    nd = toe.shape[1]
    n_selb = T // SEL_BLOCK
    win_chunks = WINDOW // tq + 1
    ww = win_chunks * LANES
    assert tq == LANES and T >= ww and tk % tq == 0 and T % tk == 0
    kern = functools.partial(_nsa_kernel, tq=tq, tk=tk, n_blk=nb - 1, n_selb=n_selb, n_top=min(N_SEL, n_selb),
                             win_chunks=win_chunks)
    rows64 = lambda blk0: pl.BlockSpec((None, HEAD_DIM, T), lambda g, b, i: (b, blk0 + g, 0))
    wide = max(tk, ww)
    return pl.pallas_call(
        kern,
        grid=(4, B, T // tq),
        in_specs=[pl.BlockSpec((None, tq, 2 * LANES), lambda g, b, i: (b, i, g)),
                  pl.BlockSpec((None, HEAD_DIM, nb), lambda g, b, i: (b, g, 0)),
                  pl.BlockSpec((None, HEAD_DIM, nb), lambda g, b, i: (b, g, 0)),
                  rows64(8), rows64(12), rows64(0), rows64(4),
                  pl.BlockSpec((None, tq, LANES), lambda g, b, i: (b, i, 8 + g)),
                  pl.BlockSpec((None, 1, LANES), lambda g, b, i: (g, 0, 0)),
                  pl.BlockSpec((4, tq, nb), lambda g, b, i: (g, i, 0)),
                  pl.BlockSpec((4, nd, tq, tq), lambda g, b, i: (g, 0, 0, 0)),
                  pl.BlockSpec((nb, LANES), lambda g, b, i: (0, 0)),
                  pl.BlockSpec((LANES, T), lambda g, b, i: (0, 0))],
        out_specs=pl.BlockSpec((None, tq, 2 * LANES), lambda g, b, i: (b, i, g)),
        out_shape=jax.ShapeDtypeStruct((B, T, 8 * LANES), F32),
        scratch_shapes=[pltpu.VMEM((LANES, T), BF16)] * 4 + [pltpu.VMEM((LANES, nb), BF16)] * 2
                       + [pltpu.VMEM((tq, T), F32), pltpu.VMEM((4 * tq, wide), BF16)]
                       + [pltpu.VMEM((4 * tq, LANES), F32)] * 4,
        compiler_params=_cparams(3),
        name="nsa_prompt_attention",
    )(qg_arr, ck_t, cv_t, cs_t, cs_t, w_t, w_t, qg_arr, bg_pad, cbias, toe, cover, expand)


def cover_matrix_np(n_blk, n_selb, rows, cols):
    i = np.arange(rows)[:, None]
    j = np.arange(cols)[None, :]
    m = (i * CMP_STRIDE < (j + 1) * SEL_BLOCK) & (i * CMP_STRIDE + CMP_BLOCK > j * SEL_BLOCK) & (i < n_blk) & (j < n_selb)
    return m.astype(np.float32)


def _even_dec_kernel(pt_ref, *refs, pps, n_steps, lam_init):
    k_refs = refs[0:pps]
    v_refs = refs[pps:2 * pps]
    lf_refs = refs[2 * pps:3 * pps]
    (qt_ref, cn_ref, tbt_ref, ut_ref, kn_ref, vn_ref, bo_ref, lp_ref, sg_ref,
     o_ref, m_ref, l_ref, acc_ref, carry_ref) = refs[3 * pps:]
    s = pl.program_id(1)
    n_maps, width = qt_ref.shape
    n_fox = lf_refs[0].shape[0]

    @pl.when(s == 0)
    def _():
        m_ref[...] = jnp.full_like(m_ref, NEG)
        l_ref[...] = jnp.zeros_like(l_ref)
        acc_ref[...] = jnp.zeros_like(acc_ref)
        carry_ref[...] = jnp.zeros_like(carry_ref)

    qt = qt_ref[...]
    is_fox = lax.broadcasted_iota(I32, (n_maps, 1), 0) < n_fox
    pad = jnp.zeros((n_maps - n_fox, PAGE_SIZE), F32)
    lfts = jnp.concatenate([jnp.concatenate([lf_refs[i][...], pad], axis=0) for i in range(pps)], axis=0)
    within = _dot_exact_rhs01(lfts, ut_ref[...])
    totals = jnp.sum(lfts, axis=-1, keepdims=True)
    run = carry_ref[...]
    later = [None] * pps
    for i in reversed(range(pps)):
        later[i] = run
        run = run + totals[i * n_maps:(i + 1) * n_maps]
    carry_ref[...] = run
    cn = cn_ref[...]
    sts = []
    for i in range(pps):
        kpg = k_refs[i][...].reshape(width, PAGE_SIZE).astype(BF16)
        suffix = within[i * n_maps:(i + 1) * n_maps] + later[i] + cn
        sts.append(_dot(qt, kpg) + jnp.where(is_fox, suffix, tbt_ref[i]))
    st = jnp.concatenate(sts, axis=1)
    m = m_ref[...]
    m_new = jnp.maximum(m, jnp.max(st, axis=-1, keepdims=True))
    alpha = jnp.exp(m - m_new)
    p = jnp.exp(st - m_new)
    l_ref[...] = alpha * l_ref[...] + jnp.sum(p, axis=-1, keepdims=True)
    pb = p.astype(BF16)
    pv = None
    for i in range(pps):
        vpg = v_refs[i][...].reshape(width, PAGE_SIZE).astype(BF16)
        d = _dot_nt(pb[:, i * PAGE_SIZE:(i + 1) * PAGE_SIZE], vpg)
        pv = d if pv is None else pv + d
    acc_ref[...] = alpha * acc_ref[...] + pv
    m_ref[...] = m_new

    @pl.when(s == n_steps - 1)
    def _():
        kn = kn_ref[...].astype(BF16).astype(F32)
        vn = vn_ref[...].astype(BF16).astype(F32)
        s_own = jnp.sum(qt.astype(F32) * kn, axis=-1, keepdims=True) + bo_ref[...]
        m = m_ref[...]
        m_all = jnp.maximum(m, s_own)
        a = jnp.exp(m - m_all)
        e_own = jnp.exp(s_own - m_all)
        l_all = a * l_ref[...] + e_own
        o = (a * acc_ref[...] + e_own * vn) / l_all
        r = lax.broadcasted_iota(I32, (n_maps, width), 0)
        cidx = lax.broadcasted_iota(I32, (n_maps, width), 1)
        d_fox = n_fox * HEAD_DIM
        fox_sel = jnp.logical_and(cidx < d_fox, r == cidx // HEAD_DIM)
        dh = (cidx - d_fox) // LANES
        d1_sel = jnp.logical_and(cidx >= d_fox, r == n_fox + 2 * dh)
        d2_sel = jnp.logical_and(cidx >= d_fox, r == n_fox + 2 * dh + 1)
        o_f = jnp.sum(jnp.where(fox_sel, o, 0.0), axis=0, keepdims=True)
        a1 = jnp.sum(jnp.where(d1_sel, o, 0.0), axis=0, keepdims=True)
        a2 = jnp.sum(jnp.where(d2_sel, o, 0.0), axis=0, keepdims=True)
        lam = _lambda_value(lp_ref[...], lam_init)
        o_d = a1 - lam * a2
        for tl in range(width // LANES):
            sl = slice(tl * LANES, (tl + 1) * LANES)
            if tl * LANES < d_fox:
                o_ref[:, sl] = o_f[:, sl]
            else:
                x = o_d[:, sl]
                ms = jnp.mean(x * x, axis=-1, keepdims=True)
                o_ref[:, sl] = x * lax.rsqrt(ms + EPS) * sg_ref[...] * (1.0 - lam_init)


def even_decode(pool_t, lf_t, li, pt_flat, n_pages, qt, cn, tbt, kn, vn, bo, lam_p, subln_g, lam_init, pps):
    DB, n_maps, width = qt.shape
    n_fox = lf_t.shape[2]
    n_steps = n_pages // pps
    ut = jnp.asarray(np.tril(np.ones((PAGE_SIZE, PAGE_SIZE), np.float32), -1), BF16)

    def page(i):
        return lambda b, s, pt: pt[b * n_pages + (n_steps - 1 - s) * pps + i]

    kv_block = (None, None, None, n_maps, HEAD_DIM, PAGE_SIZE)
    k_specs = [pl.BlockSpec(kv_block, lambda b, s, pt, f=page(i): (li, f(b, s, pt), 0, 0, 0, 0)) for i in range(pps)]
    v_specs = [pl.BlockSpec(kv_block, lambda b, s, pt, f=page(i): (li, f(b, s, pt), 1, 0, 0, 0)) for i in range(pps)]
    lf_specs = [pl.BlockSpec((None, None, n_fox, PAGE_SIZE), lambda b, s, pt, f=page(i): (li, f(b, s, pt), 0, 0)) for i in range(pps)]
    per_b = lambda shape: pl.BlockSpec((None,) + shape, lambda b, s, pt: (b,) + (0,) * len(shape))
    const = lambda shape: pl.BlockSpec(shape, lambda b, s, pt: (0,) * len(shape))
    grid_spec = pltpu.PrefetchScalarGridSpec(
        num_scalar_prefetch=1,
        grid=(DB, n_steps),
        in_specs=k_specs + v_specs + lf_specs + [
            per_b((n_maps, width)), per_b((n_maps, 1)),
            pl.BlockSpec((pps, n_maps, PAGE_SIZE), lambda b, s, pt: (n_steps - 1 - s, 0, 0)),
            const((PAGE_SIZE, PAGE_SIZE)),
            per_b((1, width)), per_b((1, width)),
            const((n_maps, 1)), const((4, HEAD_DIM)), const((1, LANES))],
        out_specs=per_b((1, width)),
        scratch_shapes=[pltpu.VMEM((n_maps, 1), F32), pltpu.VMEM((n_maps, 1), F32), pltpu.VMEM((n_maps, width), F32),
                        pltpu.VMEM((n_maps, 1), F32)])
    kern = functools.partial(_even_dec_kernel, pps=pps, n_steps=n_steps, lam_init=lam_init)
    return pl.pallas_call(
        kern, grid_spec=grid_spec,
        out_shape=jax.ShapeDtypeStruct((DB, 1, width), F32),
        compiler_params=_cparams(2),
        name="even_decode",
    )(pt_flat, *([pool_t] * (2 * pps)), *([lf_t] * pps), qt, cn, tbt, ut, kn, vn, bo, lam_p, subln_g.reshape(1, LANES))


def _odd_cmp_kernel(pt_ref, *refs, pps, n_steps, n_selb, n_top, t_pos):
    r_refs = refs[0:pps]
    (w0_ref, w1_ref, g_ref, b_ref, w0t_ref, w1t_ref, qz_ref, cb_ref, cov_ref, oc_ref, idx_ref, p0, p1) = refs[pps:]
    s = pl.program_id(1)
    width = w0_ref.shape[0]
    w0 = w0_ref[...]
    w1 = w1_ref[...]
    acc = None
    for i in range(pps):
        x = r_refs[i][...].reshape(width, PAGE_SIZE)
        a = _pool2(x, w0, w1, g_ref[i], False)
        acc = a if acc is None else acc + a
    col = pl.multiple_of(s * LANES, LANES)
    p0[:, pl.ds(col, LANES)] = acc[:, 0:LANES]
    p1[:, pl.ds(col, LANES)] = acc[:, LANES:]

    @pl.when(s == n_steps - 1)
    def _():
        nb = p0.shape[1]
        ck, cv = _cmp_finish_t(p0[...], p1[...], b_ref[...], w0t_ref[...], w1t_ref[...])
        st = _dot(qz_ref[...], ck.astype(BF16)) + cb_ref[...]
        n_idx = lax.broadcasted_iota(I32, (1, nb), 1)
        valid = n_idx * CMP_STRIDE + (CMP_BLOCK - 1) <= t_pos
        sm = jnp.where(valid, st, NEG)
        m = jnp.max(sm, axis=-1, keepdims=True)
        e = jnp.where(valid, jnp.exp(sm - m), 0.0)
        l = jnp.sum(e, axis=-1, keepdims=True)
        p = e / jnp.where(l > 0.0, l, 1.0)
        oc_ref[...] = _dot_nt(p.astype(BF16), cv.astype(BF16))
        gqa = p.shape[0] // 4
        rows = [jnp.sum(p[g * gqa:(g + 1) * gqa], axis=0, keepdims=True) for g in range(4)]
        psum = jnp.concatenate(rows + rows, axis=0)
        score = _dot_exact_rhs01(psum, cov_ref[...])
        nsp = score.shape[1]
        jb = lax.broadcasted_iota(I32, (1, nsp), 1)
        cur = t_pos // SEL_BLOCK
        forced = jnp.logical_or(jb == 0, jnp.logical_or(jb == cur, jb == cur - 1))
        score = jnp.where(forced, FORCE, score)
        score = jnp.where(jb * SEL_BLOCK <= t_pos, score, NEG)
        score = jnp.where(jb < n_selb, score, PAD_SCORE)
        jf = jb.astype(F32)
        slot = lax.broadcasted_iota(I32, (1, LANES), 1)
        picks = jnp.zeros((8, LANES), F32)
        for it in range(n_top):
            mx = jnp.max(score, axis=-1, keepdims=True)
            ix = jnp.min(jnp.where(score == mx, jf, 1e9), axis=-1, keepdims=True)
            picks = jnp.where(slot == it, ix, picks)
            score = jnp.where(jf == ix, PAD_SCORE, score)
        idx_ref[...] = picks.astype(I32)


def odd_decode_cmp(pool_t, li, pt_flat, n_pages, w0, w1, b_col, w0t, w1t, qz16, cbias, cover, n_selb, t_pos):
    DB = qz16.shape[0]
    width = w0.shape[0]
    half = width // 2
    pps = LANES // (PAGE_SIZE // CMP_STRIDE)
    assert n_pages % pps == 0
    n_steps = n_pages // pps
    nb = n_pages * PAGE_SIZE // CMP_STRIDE
    nsp = cover.shape[1]
    spp = PAGE_SIZE // CMP_STRIDE
    gstack = jnp.asarray(_group_sum_stack(pps), BF16)
    r_specs = [pl.BlockSpec((None, None, 2, 4, HEAD_DIM, PAGE_SIZE),
                            lambda b, s, pt, i=i: (li, pt[b * n_pages + s * pps + i], 0, 0, 0, 0)) for i in range(pps)]
    per_b = lambda shape: pl.BlockSpec((None,) + shape, lambda b, s, pt: (b,) + (0,) * len(shape))
    const = lambda shape: pl.BlockSpec(shape, lambda b, s, pt: (0,) * len(shape))
    grid_spec = pltpu.PrefetchScalarGridSpec(
        num_scalar_prefetch=1,
        grid=(DB, n_steps),
        in_specs=r_specs + [const((width, LANES)), const((width, LANES)), const((pps, 2 * LANES, 2 * LANES)), const((width, 1)),
                            const((half, half)), const((half, half)),
                            per_b((16, half)), const((16, nb)), const((nb, nsp))],
        out_specs=[per_b((16, half)), per_b((8, LANES))],
        scratch_shapes=[pltpu.VMEM((width, nb), F32), pltpu.VMEM((width, nb), F32)])
    kern = functools.partial(_odd_cmp_kernel, pps=pps, n_steps=n_steps, n_selb=n_selb, n_top=min(N_SEL, n_selb), t_pos=t_pos)
    return pl.pallas_call(
        kern, grid_spec=grid_spec,
        out_shape=[jax.ShapeDtypeStruct((DB, 16, half), F32), jax.ShapeDtypeStruct((DB, 8, LANES), I32)],
        compiler_params=_cparams(2),
        name="odd_decode_cmp",
    )(pt_flat, *([pool_t] * pps), w0, w1, gstack, b_col, w0t, w1t, qz16, cbias, cover)


def _odd_sel_kernel(idx_ref, pt_ref, *refs, n_top, n_past_blk, bpp, t_pos):
    k_refs = refs[0:n_top]
    v_refs = refs[n_top:2 * n_top]
    (q_ref, sn_ref, tb_ref, wk_ref, wv_ref, wn_ref, tw_ref, tw0_ref, oc_ref, gt_ref, o_ref) = refs[2 * n_top:]
    b = pl.program_id(0)
    g = pl.program_id(1)
    q = q_ref[...]
    qf = q.astype(F32)
    lane = lax.broadcasted_iota(I32, (1, PAGE_SIZE), 1)
    bias0 = tw0_ref[:, 0:1]

    logits, vals = [], []
    n_new = jnp.zeros((), I32)
    for k in range(n_top):
        blk = idx_ref[(b * 4 + g) * n_top + k]
        is_new = blk >= n_past_blk
        n_new = n_new + is_new.astype(I32)
        page = jnp.minimum(blk // bpp, n_past_blk // bpp - 1)
        bias = tb_ref[pl.ds(pl.multiple_of((page * 4 + g) * 8, 8), 8), :]
        s = _dot(q, k_refs[k][...].astype(BF16)) + bias
        pos = page * PAGE_SIZE + lane
        valid = jnp.logical_and(jnp.logical_and(pos // SEL_BLOCK == blk, pos <= t_pos), jnp.logical_not(is_new))
        logits.append(jnp.where(valid, s, NEG))
        vals.append(v_refs[k][...].astype(BF16))
    has_new = n_new > 0
    kn = sn_ref[0:1, :].astype(BF16).astype(F32)
    vn = sn_ref[1:2, :].astype(BF16).astype(F32)
    s_new = jnp.where(has_new, jnp.sum(qf * kn, axis=-1, keepdims=True) + bias0, NEG)
    m = s_new
    for s in logits:
        m = jnp.maximum(m, s.max(axis=-1, keepdims=True))
    p_new = jnp.where(has_new, jnp.exp(s_new - m), 0.0)
    l = p_new
    acc = p_new * vn
    for s, vt in zip(logits, vals):
        p = jnp.where(s > 0.5 * NEG, jnp.exp(s - m), 0.0)
        l = l + jnp.sum(p, axis=-1, keepdims=True)
        acc = acc + _dot_nt(p.astype(BF16), vt)
    o_s = acc / jnp.where(l > 0.0, l, 1.0)

    sw = _dot(q, wk_ref[...].astype(BF16)) + tw_ref[...]
    kwn = wn_ref[0:1, :].astype(BF16).astype(F32)
    vwn = wn_ref[1:2, :].astype(BF16).astype(F32)
    sw_new = jnp.sum(qf * kwn, axis=-1, keepdims=True) + bias0
    mw = jnp.maximum(jnp.max(sw, axis=-1, keepdims=True), sw_new)
    pw = jnp.exp(sw - mw)
    pw_new = jnp.exp(sw_new - mw)
    lw = jnp.sum(pw, axis=-1, keepdims=True) + pw_new
    o_w = (_dot_nt(pw.astype(BF16), wv_ref[...].astype(BF16)) + pw_new * vwn) / lw

    gates = 1.0 / (1.0 + jnp.exp(-gt_ref[...]))
    o_ref[...] = gates[:, 0:1] * oc_ref[...] + gates[:, 1:2] * o_s + gates[:, 2:3] * o_w


def odd_decode_sel(pool_t, li, idx_flat, pt_flat, n_pages, win_t, qg, s_new, tbp, w_new, tw, tw0, oc_g, graw, n_top, t_pos):
    DB = qg.shape[0]
    bpp = PAGE_SIZE // SEL_BLOCK
    wb = win_t.shape[-1]
    n_past_blk = n_pages * bpp

    def blk_spec(k, which):
        def imap(b, g, idx, pt):
            blk = idx[(b * 4 + g) * n_top + k]
            page = pt[b * n_pages + jnp.minimum(blk // bpp, n_pages - 1)]
            return (li, page, which, g, 0, 0)
        return pl.BlockSpec((None, None, None, None, HEAD_DIM, PAGE_SIZE), imap)

    per_bg = lambda shape: pl.BlockSpec((None, None) + shape, lambda b, g, idx, pt: (b, g) + (0,) * len(shape))
    grid_spec = pltpu.PrefetchScalarGridSpec(
        num_scalar_prefetch=2,
        grid=(DB, 4),
        in_specs=[blk_spec(k, 2) for k in range(n_top)] + [blk_spec(k, 3) for k in range(n_top)] + [
            per_bg((8, HEAD_DIM)), per_bg((2, HEAD_DIM)),
            pl.BlockSpec(tbp.shape, lambda b, g, idx, pt: (0, 0)),
            pl.BlockSpec((None, None, None, None, HEAD_DIM, wb), lambda b, g, idx, pt: (li, b, 0, g, 0, 0)),
            pl.BlockSpec((None, None, None, None, HEAD_DIM, wb), lambda b, g, idx, pt: (li, b, 1, g, 0, 0)),
            per_bg((2, HEAD_DIM)),
            pl.BlockSpec((8, wb), lambda b, g, idx, pt: (g, 0)),
            pl.BlockSpec((8, LANES), lambda b, g, idx, pt: (g, 0)),
            per_bg((8, HEAD_DIM)), per_bg((8, LANES))],
        out_specs=per_bg((8, HEAD_DIM)))
    kern = functools.partial(_odd_sel_kernel, n_top=n_top, n_past_blk=n_past_blk, bpp=bpp, t_pos=t_pos)
    return pl.pallas_call(
        kern, grid_spec=grid_spec,
        out_shape=jax.ShapeDtypeStruct((DB, 4, 8, HEAD_DIM), F32),
        compiler_params=_cparams(2),
        name="odd_decode_sel",
    )(idx_flat, pt_flat, *([pool_t] * (2 * n_top)), qg, s_new, tbp, win_t, win_t, w_new, tw, tw0, oc_g, graw)


def _pad_cols(w, n):
    return jnp.pad(w, ((0, 0), (0, n - w.shape[1])))


def _pad_rows8(x4):
    pad = [(0, 0)] * x4.ndim
    pad[-2] = (0, 4)
    return jnp.pad(x4, pad)


def even_layer(hp, hs, pool_kv, pool_lf, li, page_table, g0, w_in, b_f, lam_p, subln_g, rel_table, layer):
    B, T, D = hp.shape
    DB, S, _ = hs.shape
    assert S == 1
    n_fox, d_fox = 8, 512
    lam_init = 0.8 - 0.6 * math.exp(-0.3 * layer)
    cuts = np.cumsum((d_fox,) * 6)
    q_f, k_f, v_f, q_d, k_d, v_d, f_w = jnp.split(w_in, [int(c) for c in cuts], axis=1)
    w_q = jnp.concatenate([q_f, q_d, _pad_cols(f_w, LANES)], axis=1).astype(BF16)
    w_kv = jnp.concatenate([k_f, k_d, v_f, v_d], axis=1).astype(BF16)
    bf_pad = jnp.pad(b_f.astype(F32), (0, LANES - n_fox)).reshape(1, LANES)
    table4 = rel_table[:, :4]

    q_arr = norm_proj(hp.reshape(B * T, D), g0, w_q, 384).reshape(B, T, -1)
    kv_t = norm_proj_t(hp, g0, w_kv.T, 512)
    tq = min(T, 256)
    tk = min(T, 512)
    lf_pad, cum_pad = logf_cumsum(q_arr, 8, bf_pad, tq)
    cum_pairs = cum_pad[:, :, :n_fox].transpose(0, 2, 1).reshape(B, 4, 2, T)
    toe = bias_table(table4, toeplitz_dist(tq, T // tq))
    attn_p = even_prompt_attention(q_arr, kv_t, cum_pairs, toe, lam_p, subln_g, lam_init, tq, tk)
    ekv_p = kv_t.reshape(B, 2, 16, HEAD_DIM, T).transpose(0, 4, 1, 2, 3)
    elf_p = lf_pad[:, :, :n_fox]

    n_pages = page_table.shape[1]
    past = n_pages * PAGE_SIZE
    xs = hs.reshape(DB, D)
    qs_arr = norm_proj(xs, g0, w_q, 384)
    kvs_arr = norm_proj(xs, g0, w_kv, 512)
    lfs_pad, _ = logf_cumsum(qs_arr.reshape(1, DB, -1), 8, bf_pad, DB)
    lf_new = lfs_pad[0, :, :n_fox]
    seg = np.zeros((16, 1024), np.float32)
    for m_ in range(16):
        seg[m_, m_ * HEAD_DIM:(m_ + 1) * HEAD_DIM] = 1.0
    qt = (qs_arr[:, None, :1024] * SCALE * seg[None]).astype(BF16)
    cn = jnp.pad(lf_new, ((0, 0), (0, 8)))[:, :, None]
    pos = jnp.arange(past).reshape(n_pages, PAGE_SIZE)
    rb = jnp.repeat(bias_table(table4, past - pos), 2, axis=0)
    tbt = jnp.concatenate([jnp.zeros((n_pages, 8, PAGE_SIZE), F32), rb.transpose(1, 0, 2)], axis=1)
    bo = jnp.concatenate([jnp.zeros((8,), F32), jnp.repeat(rel_table[0, :4], 2)]).reshape(16, 1)
    pool_t = pool_kv.transpose(0, 1, 3, 4, 5, 2)
    lf_t = pool_lf.transpose(0, 1, 3, 2)
    attn_s = even_decode(pool_t, lf_t, li, page_table.reshape(-1), n_pages, qt, cn, tbt,
                         kvs_arr[:, None, :1024], kvs_arr[:, None, 1024:], bo, lam_p, subln_g, lam_init,
                         pps=math.gcd(n_pages, 8))
    ekv_s = kvs_arr.reshape(DB, 1, 2, 16, HEAD_DIM)
    elf_s = lf_new.reshape(DB, 1, n_fox)
    return attn_p.reshape(B * T, -1), attn_s.reshape(DB, -1), ekv_p, ekv_s, elf_p, elf_s


def odd_layer(hp, hs, pool, win_state, li, page_table, g0, w_in, b_gate, w_pos, b_cmp, w_cout, rel_table):
    B, T, D = hp.shape
    DB, S, _ = hs.shape
    assert S == 1
    d_q, d_kv = 1024, 512
    w_q, w_c, w_s, w_w, w_g = jnp.split(w_in, [d_q, d_q + d_kv, d_q + 2 * d_kv, d_q + 3 * d_kv], axis=1)
    n_gate = 12
    w_g4 = jnp.concatenate([_pad_cols(w_g[:, g * n_gate:(g + 1) * n_gate], LANES) for g in range(4)], axis=1)
    w_qg = jnp.concatenate([w_q, w_g4], axis=1).astype(BF16)
    w_cs = jnp.concatenate([w_c, w_s], axis=1).astype(BF16)
    w_w16 = w_w.astype(BF16)
    bg = b_gate.astype(F32)
    bg_pad = jnp.pad(bg.reshape(4, 1, n_gate), ((0, 0), (0, 0), (0, LANES - n_gate)))
    wp = w_pos.reshape(CMP_BLOCK, d_kv)
    w0, w1 = _pool_weights_t(wp)
    b_col = b_cmp.reshape(d_kv, 1).astype(F32)
    eye4 = jnp.eye(4, dtype=F32)
    w0t = jnp.kron(eye4, w_cout[0].T).astype(BF16)
    w1t = jnp.kron(eye4, w_cout[1].T).astype(BF16)
    table = rel_table

    qg_arr = norm_proj(hp.reshape(B * T, D), g0, w_qg, 512).reshape(B, T, -1)
    cs_t = norm_proj_t(hp, g0, w_cs.T, 512)
    w_t = norm_proj_t(hp, g0, w_w16.T, 512)
    ck_t, cv_t = cmp_build(cs_t, w0, w1, b_col, w0t, w1t)
    nb = T // CMP_STRIDE
    n_blk = nb - 1
    n_selb = T // SEL_BLOCK
    tq = LANES
    tk = min(T, 512)
    t_all = jnp.arange(T)
    e_pos = jnp.arange(nb) * CMP_STRIDE + CMP_BLOCK - 1
    cbias = bias_table(table, t_all[:, None] - e_pos[None, :])
    toe = bias_table(table, toeplitz_dist(tq, T // tq))
    cover = jnp.asarray(cover_matrix_np(n_blk, n_selb, nb, LANES), BF16)
    expand = np.zeros((LANES, T), np.float32)
    expand[np.arange(T) // SEL_BLOCK, np.arange(T)] = 1.0
    attn_p = nsa_prompt_attention(qg_arr, ck_t, cv_t, cs_t, w_t, bg_pad, cbias, toe, cover, jnp.asarray(expand, BF16), tq, tk)
    okv_p = cs_t.reshape(B, 4, 4, HEAD_DIM, T).transpose(0, 4, 1, 2, 3)
    wn = min(WINDOW, T)
    win_p = w_t[:, :, T - wn:].reshape(B, 2, 4, HEAD_DIM, wn).transpose(0, 4, 1, 2, 3)

    n_pages = page_table.shape[1]
    past = n_pages * PAGE_SIZE
    xs = hs.reshape(DB, D)
    qgs = norm_proj(xs, g0, w_qg, 512)
    css = norm_proj(xs, g0, w_cs, 512)
    wns = norm_proj(xs, g0, w_w16, 512)
    pt_flat = page_table.reshape(-1)
    nbs = past // CMP_STRIDE
    n_past_blk = past // SEL_BLOCK
    n_selb_s = n_past_blk + 1
    nsp = -(-n_selb_s // LANES) * LANES
    qs = qgs[:, :d_q]
    segz = np.zeros((16, 256), np.float32)
    for h_ in range(16):
        segz[h_, (h_ // 4) * HEAD_DIM:(h_ // 4 + 1) * HEAD_DIM] = 1.0
    qz16 = (jnp.tile(qs.reshape(DB, 16, 1, HEAD_DIM), (1, 1, 4, 1)).reshape(DB, 16, 256) * SCALE * segz[None]).astype(BF16)
    e_pos_s = jnp.arange(nbs) * CMP_STRIDE + CMP_BLOCK - 1
    cbias_s = bias_table(table, past - e_pos_s)
    cover_s = jnp.asarray(cover_matrix_np(nbs, n_selb_s, nbs, nsp), BF16)
    pool_t = pool.transpose(0, 1, 3, 4, 5, 2)
    n_top = min(N_SEL, n_selb_s)
    oc16, idx8 = odd_decode_cmp(pool_t, li, pt_flat, n_pages, w0, w1, b_col, w0t, w1t, qz16, cbias_s, cover_s,
                                n_selb_s, past)
    idx_flat = idx8[:, :4, :n_top].reshape(-1)
    oc4 = oc16.reshape(DB, 4, 4, 4, HEAD_DIM)[:, np.arange(4), :, np.arange(4)].transpose(1, 0, 2, 3)
    qg = _pad_rows8(qs.reshape(DB, 4, 4, HEAD_DIM) * SCALE).astype(BF16)
    pos_p = jnp.arange(past).reshape(n_pages, PAGE_SIZE)
    tbp = bias_table(table, past - pos_p).reshape(4, 4, n_pages, PAGE_SIZE).transpose(2, 0, 1, 3)
    tbp = _pad_rows8(tbp).reshape(n_pages * 32, PAGE_SIZE)
    wb = win_state.shape[2]
    tw = _pad_rows8(bias_table(table, wb - jnp.arange(wb)).reshape(4, 4, wb)).reshape(32, wb)
    tw0 = _pad_rows8(jnp.broadcast_to(table[0].reshape(4, 4, 1), (4, 4, LANES))).reshape(32, LANES)
    g48 = jnp.concatenate([qgs[:, d_q + g * LANES:d_q + g * LANES + n_gate] for g in range(4)], axis=1) + bg[None]
    graw = jnp.pad(_pad_rows8(g48.reshape(DB, 4, 4, 3)), ((0, 0), (0, 0), (0, 0), (0, LANES - 3)))
    win_t = win_state.transpose(0, 1, 3, 4, 5, 2)
    s_new = css[:, d_kv:].reshape(DB, 2, 4, HEAD_DIM).transpose(0, 2, 1, 3)
    w_new = wns.reshape(DB, 2, 4, HEAD_DIM).transpose(0, 2, 1, 3)
    o4 = odd_decode_sel(pool_t, li, idx_flat, pt_flat, n_pages, win_t, qg, s_new, tbp, w_new, tw, tw0,
                        _pad_rows8(oc4), graw, n_top, past)
    attn_s = o4[:, :, :4, :].reshape(DB, d_q)
    okv_s = css.reshape(DB, 1, 4, 4, HEAD_DIM)
    keys_t = jnp.concatenate([win_t[li], wns.reshape(DB, 2, 4, HEAD_DIM, 1)], axis=-1)
    keep = min(WINDOW, wb + 1)
    win_s = keys_t[..., wb + 1 - keep:].transpose(0, 4, 1, 2, 3)
    return attn_p.reshape(B * T, -1), attn_s, okv_p, okv_s, win_p, win_s


def kernel(x_prompt, x_sample, cache_even_kv, cache_even_logf, cache_odd_kv, state_odd_win, page_table, rel_table, norm_g, w_even_in, b_even_f, diff_lambda, diff_subln_g, w_even_out, w_odd_in, b_odd_gate, w_cmp_pos, b_cmp, w_cmp_out, w_odd_out, w_ffn_in, w_ffn_out):
    B, T, D = x_prompt.shape
    DB, S, _ = x_sample.shape
    depth = norm_g.shape[0]
    hp = x_prompt.reshape(B * T, D)
    hs = x_sample.reshape(DB * S, D)
    ekv_p, ekv_s, elf_p, elf_s = [], [], [], []
    okv_p, okv_s, win_p, win_s = [], [], [], []
    for layer in range(depth):
        g = norm_g[layer]
        li = layer // 2
        hp3, hs3 = hp.reshape(B, T, D), hs.reshape(DB, S, D)
        if layer % 2 == 0:
            a_p, a_s, kvp, kvs, lfp, lfs = even_layer(
                hp3, hs3, cache_even_kv, cache_even_logf, li, page_table, g[0], w_even_in[li], b_even_f[li],
                diff_lambda[li].astype(F32), diff_subln_g[li].astype(F32), rel_table, layer)
            w_o = w_even_out[li].astype(BF16)
            ekv_p.append(kvp); ekv_s.append(kvs); elf_p.append(lfp); elf_s.append(lfs)
        else:
            a_p, a_s, kvp, kvs, wp_, ws_ = odd_layer(
                hp3, hs3, cache_odd_kv, state_odd_win, li, page_table, g[0], w_odd_in[li], b_odd_gate[li],
                w_cmp_pos[li], b_cmp[li], w_cmp_out[li], rel_table)
            w_o = w_odd_out[li].astype(BF16)
            okv_p.append(kvp); okv_s.append(kvs); win_p.append(wp_); win_s.append(ws_)
        hp = out_proj_residual(a_p, w_o, g[1], hp)
        hs = out_proj_residual(a_s, w_o, g[1], hs)
        w1 = w_ffn_in[layer].astype(BF16)
        w2 = w_ffn_out[layer].astype(BF16)
        hp = ffn_residual(hp, g[2], w1, w2, g[3], 256)
        hs = ffn_residual(hs, g[2], w1, w2, g[3], 256)
    return (hp.reshape(B, T, D), hs.reshape(DB, S, D), jnp.stack(ekv_p), jnp.stack(ekv_s), jnp.stack(elf_p), jnp.stack(elf_s),
            jnp.stack(okv_p), jnp.stack(okv_s), jnp.stack(win_p), jnp.stack(win_s))
```

```python
import functools
import math

import numpy as np
import jax
import jax.numpy as jnp
from jax import lax
from jax.experimental import pallas as pl
from jax.experimental.pallas import tpu as pltpu

F32 = jnp.float32
BF16 = jnp.bfloat16
I32 = jnp.int32

HEAD_DIM = 64
LANES = 128
PAGE_SIZE = 128
CMP_BLOCK = 32
CMP_STRIDE = 16
SEL_BLOCK = 64
N_SEL = 16
WINDOW = 512
N_BUCKETS = 32
MAX_DISTANCE = 1024
SCALE = HEAD_DIM ** -0.5
LOG2E = math.log2(math.e)
EPS = 1e-6
NEG = -1e30
FORCE = 1e9
PAD_SCORE = -3.0e38
VMEM_LIMIT = 56 * 1024 * 1024


def _cparams(n_axes):
    return pltpu.CompilerParams(dimension_semantics=("arbitrary",) * n_axes,
                                vmem_limit_bytes=VMEM_LIMIT)


def _dot(a, b):
    return jnp.dot(a, b, preferred_element_type=F32)


def _dot_nt(a, b):
    return lax.dot_general(a, b, (((1,), (1,)), ((), ())), preferred_element_type=F32)


def _split3(x):
    hi = x.astype(BF16)
    r1 = x - hi.astype(F32)
    mid = r1.astype(BF16)
    lo = (r1 - mid.astype(F32)).astype(BF16)
    return hi, mid, lo


def _dot_exact_rhs01(x, m01):
    hi, mid, lo = _split3(x)
    return _dot(hi, m01) + _dot(mid, m01) + _dot(lo, m01)


def _dot_exact_lhs01(m01, x):
    hi, mid, lo = _split3(x)
    return _dot(m01, hi) + _dot(m01, mid) + _dot(m01, lo)


def _lane_tile(x, n):
    return x if n == 1 else jnp.concatenate([x] * n, axis=1)


def _aligned(x, m):
    return x if isinstance(x, int) else pl.multiple_of(x, m)


def _shift_left_one_lane(x):
    n = x.shape[1] // LANES
    if n == 0:
        return pltpu.roll(x, x.shape[1] - 1, 1)
    keep = lax.broadcasted_iota(I32, (1, LANES), 1) < LANES - 1
    rolled = [pltpu.roll(x[:, c * LANES:(c + 1) * LANES], LANES - 1, 1) for c in range(n)]
    out = [jnp.where(keep, rolled[c], rolled[min(c + 1, n - 1)]) for c in range(n)]
    return out[0] if n == 1 else jnp.concatenate(out, axis=1)


def _silu(x):
    return x * (1.0 / (1.0 + jnp.exp(-x)))


def rel_bucket(dist):
    n = jnp.maximum(dist, 0)
    exact = N_BUCKETS // 2
    nf = jnp.maximum(n, 1).astype(F32)
    large = exact + (jnp.log(nf / exact) / math.log(MAX_DISTANCE / exact) * (N_BUCKETS - exact)).astype(I32)
    return jnp.where(n < exact, n, jnp.minimum(large, N_BUCKETS - 1))


def bias_table(table, dist):
    onehot = (rel_bucket(dist)[..., None] == jnp.arange(N_BUCKETS)).astype(F32)
    return jnp.einsum('...k,kh->h...', onehot, table.astype(F32), precision=lax.Precision.HIGHEST)


def _norm_rows(x, g):
    ms = jnp.mean(x * x, axis=-1, keepdims=True)
    return x * lax.rsqrt(ms + EPS) * g


def _proj_kernel(x_ref, g_ref, w_ref, o_ref, h_ref):
    @pl.when(pl.program_id(1) == 0)
    def _():
        h_ref[...] = _norm_rows(x_ref[...], g_ref[...]).astype(BF16)

    o_ref[...] = _dot(h_ref[...], w_ref[...]).astype(o_ref.dtype)


def norm_proj(x, g, w, tn, out_dtype=F32):
    M, D = x.shape
    N = w.shape[1]
    tm = min(M, 1024)
    return pl.pallas_call(
        _proj_kernel,
        grid=(M // tm, N // tn),
        in_specs=[pl.BlockSpec((tm, D), lambda i, j: (i, 0)),
                  pl.BlockSpec((1, D), lambda i, j: (0, 0)),
                  pl.BlockSpec((D, tn), lambda i, j: (0, j))],
        out_specs=pl.BlockSpec((tm, tn), lambda i, j: (i, j)),
        out_shape=jax.ShapeDtypeStruct((M, N), out_dtype),
        scratch_shapes=[pltpu.VMEM((tm, D), BF16)],
        compiler_params=_cparams(2),
        name="norm_proj",
    )(x, g.reshape(1, D), w)


def _proj_t_kernel(x_ref, g_ref, w_ref, o_ref, h_ref):
    @pl.when(pl.program_id(2) == 0)
    def _():
        h_ref[...] = _norm_rows(x_ref[...], g_ref[...]).astype(BF16)

    o_ref[...] = _dot_nt(w_ref[...], h_ref[...])


def norm_proj_t(x, g, w_t, tn):
    B, T, D = x.shape
    N = w_t.shape[0]
    tm = min(T, 1024)
    return pl.pallas_call(
        _proj_t_kernel,
        grid=(B, T // tm, N // tn),
        in_specs=[pl.BlockSpec((None, tm, D), lambda b, i, j: (b, i, 0)),
                  pl.BlockSpec((1, D), lambda b, i, j: (0, 0)),
                  pl.BlockSpec((tn, D), lambda b, i, j: (j, 0))],
        out_specs=pl.BlockSpec((None, tn, tm), lambda b, i, j: (b, j, i)),
        out_shape=jax.ShapeDtypeStruct((B, N, T), F32),
        scratch_shapes=[pltpu.VMEM((tm, D), BF16)],
        compiler_params=_cparams(3),
        name="norm_proj_t",
    )(x, g.reshape(1, D), w_t)


def _out_res_kernel(a_ref, w_ref, g_ref, r_ref, o_ref):
    y = _dot(a_ref[...].astype(BF16), w_ref[...])
    o_ref[...] = r_ref[...] + _norm_rows(y, g_ref[...])


def out_proj_residual(a, w, g, res):
    M, K = a.shape
    D = w.shape[1]
    tm = min(M, 512)
    return pl.pallas_call(
        _out_res_kernel,
        grid=(M // tm,),
        in_specs=[pl.BlockSpec((tm, K), lambda i: (i, 0)),
                  pl.BlockSpec((K, D), lambda i: (0, 0)),
                  pl.BlockSpec((1, D), lambda i: (0, 0)),
                  pl.BlockSpec((tm, D), lambda i: (i, 0))],
        out_specs=pl.BlockSpec((tm, D), lambda i: (i, 0)),
        out_shape=jax.ShapeDtypeStruct((M, D), F32),
        compiler_params=_cparams(1),
        name="out_proj_residual",
    )(a, w, g.reshape(1, D), res)


def _ffn_kernel(x_ref, g2_ref, wgu_ref, wo_ref, g3_ref, o_ref, h_ref, acc_ref):
    f = pl.program_id(1)
    tf = wo_ref.shape[0]

    @pl.when(f == 0)
    def _():
        h_ref[...] = _norm_rows(x_ref[...], g2_ref[...]).astype(BF16)
        acc_ref[...] = jnp.zeros_like(acc_ref)

    gu = _dot(h_ref[...], wgu_ref[...])
    acc_ref[...] += _dot((_silu(gu[:, :tf]) * gu[:, tf:]).astype(BF16), wo_ref[...])

    @pl.when(f == pl.num_programs(1) - 1)
    def _():
        o_ref[...] = x_ref[...] + _norm_rows(acc_ref[...], g3_ref[...])


def ffn_weights(w_in, w_out, tf):
    D = w_in.shape[0]
    Fdim = w_out.shape[0]
    nf = Fdim // tf
    gate = w_in[:, :Fdim].reshape(D, nf, tf)
    up = w_in[:, Fdim:].reshape(D, nf, tf)
    return jnp.concatenate([gate, up], axis=2).transpose(1, 0, 2).astype(BF16), w_out.astype(BF16)


def ffn_residual(x, g2, w_gu, w_out, g3):
    M, D = x.shape
    nf, _, tf2 = w_gu.shape
    tf = tf2 // 2
    tm = min(M, 1024)
    return pl.pallas_call(
        _ffn_kernel,
        grid=(M // tm, nf),
        in_specs=[pl.BlockSpec((tm, D), lambda i, f: (i, 0)),
                  pl.BlockSpec((1, D), lambda i, f: (0, 0)),
                  pl.BlockSpec((None, D, tf2), lambda i, f: (f, 0, 0)),
                  pl.BlockSpec((tf, D), lambda i, f: (f, 0)),
                  pl.BlockSpec((1, D), lambda i, f: (0, 0))],
        out_specs=pl.BlockSpec((tm, D), lambda i, f: (i, 0)),
        out_shape=jax.ShapeDtypeStruct((M, D), F32),
        scratch_shapes=[pltpu.VMEM((tm, D), BF16), pltpu.VMEM((tm, D), F32)],
        compiler_params=_cparams(2),
        name="ffn_residual",
    )(x, g2.reshape(1, D), w_gu, w_out, g3.reshape(1, D))


def _logf_kernel(f_ref, b_ref, tri_ref, lf_ref, cum_ref, carry_ref):
    @pl.when(pl.program_id(1) == 0)
    def _():
        carry_ref[...] = jnp.zeros_like(carry_ref)

    x = f_ref[...] + b_ref[...]
    lf = jnp.minimum(x, 0.0) - jnp.log(1.0 + jnp.exp(-jnp.abs(x)))
    lf_ref[...] = lf
    cum = _dot_exact_lhs01(tri_ref[...], lf) + carry_ref[...]
    cum_ref[...] = cum
    carry_ref[...] = cum[cum.shape[0] - 1:, :]


def logf_cumsum(fq, col_block, b_pad, tq):
    B, T, _ = fq.shape
    tri = jnp.asarray(np.tril(np.ones((tq, tq), np.float32)), BF16)
    return pl.pallas_call(
        _logf_kernel,
        grid=(B, T // tq),
        in_specs=[pl.BlockSpec((None, tq, LANES), lambda b, i: (b, i, col_block)),
                  pl.BlockSpec((1, LANES), lambda b, i: (0, 0)),
                  pl.BlockSpec((tq, tq), lambda b, i: (0, 0))],
        out_specs=[pl.BlockSpec((None, tq, LANES), lambda b, i: (b, i, 0)),
                   pl.BlockSpec((None, tq, LANES), lambda b, i: (b, i, 0))],
        out_shape=[jax.ShapeDtypeStruct((B, T, LANES), F32)] * 2,
        scratch_shapes=[pltpu.VMEM((1, LANES), F32)],
        compiler_params=_cparams(2),
        name="logf_cumsum",
    )(fq, b_pad, tri)


def _lambda_value(lp, lam_init):
    a = jnp.sum(lp[0:1, :] * lp[1:2, :], axis=-1, keepdims=True)
    b = jnp.sum(lp[2:3, :] * lp[3:4, :], axis=-1, keepdims=True)
    return jnp.exp(a) - jnp.exp(b) + lam_init


def _even_attn_kernel(lp_ref, q_ref, k_ref, v_ref, c_ref, tb_ref, sg_ref, o_ref, kb, vb, m_ref, l_ref, acc_ref,
                      *, tq, tk, n_fox, lam_init):
    c = pl.program_id(1)
    qi = pl.program_id(2)

    @pl.when(qi == 0)
    def _():
        kb[...] = k_ref[...].astype(BF16)
        vb[...] = v_ref[...].astype(BF16)

    lo = lax.broadcasted_iota(I32, (1, LANES), 1) < HEAD_DIM
    q = q_ref[...] * SCALE
    qh = (jnp.where(lo, q, 0.0).astype(BF16), jnp.where(lo, 0.0, q).astype(BF16))
    t0 = qi * tq
    n_full = t0 // tk
    sub = tk // tq
    nd = tb_ref.shape[0]

    def tile(kt, bias_fn, first, diag):
        start = _aligned(kt * tk, tk)
        k = kb[:, pl.ds(start, tk)]
        v = vb[:, pl.ds(start, tk)]
        if diag:
            row = t0 + lax.broadcasted_iota(I32, (tq, tk), 0)
            col = start + lax.broadcasted_iota(I32, (tq, tk), 1)
            causal = col <= row
        for h in range(2):
            s = _dot(qh[h], k) + bias_fn(h, kt, start)
            if diag:
                s = jnp.where(causal, s, NEG)
            rmax = jnp.broadcast_to(jnp.max(s, axis=-1, keepdims=True), (tq, LANES))
            if first:
                m_new = rmax
            else:
                m_prev = m_ref[h]
                m_new = jnp.maximum(m_prev, rmax)
                alpha = jnp.exp(m_prev - m_new)
            p = jnp.exp(s - _lane_tile(m_new, tk // LANES))
            rsum = jnp.broadcast_to(jnp.sum(p, axis=-1, keepdims=True), (tq, LANES))
            pv = _dot_nt(p.astype(BF16), v)
            if first:
                l_ref[h] = rsum
                acc_ref[h] = pv
            else:
                l_ref[h] = alpha * l_ref[h] + rsum
                acc_ref[h] = alpha * acc_ref[h] + pv
            m_ref[h] = m_new

    def run(bias_fn):
        @pl.when(n_full == 0)
        def _():
            tile(0, bias_fn, True, True)

        @pl.when(n_full > 0)
        def _():
            tile(0, bias_fn, True, False)

            def body(kt, carry):
                tile(kt, bias_fn, False, False)
                return carry

            lax.fori_loop(1, n_full, body, 0)
            tile(n_full, bias_fn, False, True)

    @pl.when(c < n_fox)
    def _():
        def bias_fn(h, kt, start):
            return -c_ref[h:h + 1, pl.ds(start, tk)]

        run(bias_fn)
        o_ref[...] = jnp.where(lo, acc_ref[0] / l_ref[0], acc_ref[1] / l_ref[1])

    @pl.when(c >= n_fox)
    def _():
        def bias_fn(h, kt, start):
            d0 = qi - kt * sub
            return jnp.concatenate([tb_ref[jnp.clip(d0 - cc, 0, nd - 1)] for cc in range(sub)], axis=1)

        run(bias_fn)
        lam = _lambda_value(lp_ref[...], lam_init)
        o = acc_ref[0] / l_ref[0] - lam * (acc_ref[1] / l_ref[1])
        ms = jnp.mean(o * o, axis=-1, keepdims=True)
        o_ref[...] = o * lax.rsqrt(ms + EPS) * sg_ref[...] * (1.0 - lam_init)


def even_prompt_attention(q_arr, kv_t, cum_pairs, toe, lam_p, subln_g, lam_init, tq, tk):
    B, T, _ = q_arr.shape
    n_fox = 4
    n_tiles = 8
    nd = toe.shape[1]
    kern = functools.partial(_even_attn_kernel, tq=tq, tk=tk, n_fox=n_fox, lam_init=lam_init)
    return pl.pallas_call(
        kern,
        grid=(B, n_tiles, T // tq),
        in_specs=[pl.BlockSpec((4, HEAD_DIM), lambda b, c, i: (0, 0)),
                  pl.BlockSpec((None, tq, LANES), lambda b, c, i: (b, i, c)),
                  pl.BlockSpec((None, LANES, T), lambda b, c, i: (b, c, 0)),
                  pl.BlockSpec((None, LANES, T), lambda b, c, i: (b, n_tiles + c, 0)),
                  pl.BlockSpec((None, None, 2, T), lambda b, c, i: (b, jnp.minimum(c, n_fox - 1), 0, 0)),
                  pl.BlockSpec((None, nd, tq, tq), lambda b, c, i: (jnp.maximum(c - n_fox, 0), 0, 0, 0)),
                  pl.BlockSpec((1, LANES), lambda b, c, i: (0, 0))],
        out_specs=pl.BlockSpec((None, tq, LANES), lambda b, c, i: (b, i, c)),
        out_shape=jax.ShapeDtypeStruct((B, T, n_tiles * LANES), F32),
        scratch_shapes=[pltpu.VMEM((LANES, T), BF16), pltpu.VMEM((LANES, T), BF16),
                        pltpu.VMEM((2, tq, LANES), F32), pltpu.VMEM((2, tq, LANES), F32), pltpu.VMEM((2, tq, LANES), F32)],
        compiler_params=_cparams(3),
        name="even_prompt_attention",
    )(lam_p, q_arr, kv_t, kv_t, cum_pairs, toe, subln_g.reshape(1, LANES))


def toeplitz_dist(tq, nd):
    i = np.arange(tq)[:, None]
    j = np.arange(tq)[None, :]
    return jnp.asarray(np.maximum(np.arange(nd)[:, None, None] * tq + (i - j)[None], 0), I32)


def _pool_weights_t(w_pos_rows):
    reps = LANES // CMP_STRIDE
    w0 = jnp.tile(w_pos_rows[:CMP_STRIDE].T, (1, reps))
    w1 = jnp.tile(w_pos_rows[CMP_STRIDE:].T, (1, reps))
    return w0.astype(F32), w1.astype(F32)


def _group_sum_stack(n_tiles):
    spt = LANES // CMP_STRIDE
    assert n_tiles * spt <= LANES
    m = np.zeros((n_tiles, 2 * LANES, 2 * LANES), np.float32)
    pos = np.arange(LANES)
    for i in range(n_tiles):
        m[i, pos, i * spt + pos // CMP_STRIDE] = 1.0
        m[i, LANES + pos, LANES + i * spt + pos // CMP_STRIDE] = 1.0
    return m


def _pool2(x, w0, w1, gmat2, two_pass):
    y = jnp.concatenate([x * w0, x * w1], axis=1)
    hi = y.astype(BF16)
    out = _dot(hi, gmat2)
    if two_pass:
        out = out + _dot((y - hi.astype(F32)).astype(BF16), gmat2)
    return out


def _cmp_finish_t(p0, p1, b_col, w0t, w1t):
    half = w0t.shape[0]
    z = _silu(p0 + _shift_left_one_lane(p1) + b_col)
    ck = _dot(w0t, z[:half].astype(BF16))
    cv = _dot(w1t, z[half:].astype(BF16))
    return ck, cv


def _cmp_build_kernel(c_ref, w0_ref, w1_ref, g_ref, b_ref, w0t_ref, w1t_ref, ck_ref, cv_ref):
    T = c_ref.shape[1]
    nb = ck_ref.shape[1]
    w0 = w0_ref[...]
    w1 = w1_ref[...]
    acc = None
    for lt in range(T // LANES):
        a = _pool2(c_ref[:, lt * LANES:(lt + 1) * LANES], w0, w1, g_ref[lt], True)
        acc = a if acc is None else acc + a
    ck, cv = _cmp_finish_t(acc[:, 0:nb], acc[:, LANES:LANES + nb], b_ref[...], w0t_ref[...], w1t_ref[...])
    ck_ref[...] = ck
    cv_ref[...] = cv


def cmp_build(cs_t, w0, w1, b_col, w0t, w1t):
    B, _, T = cs_t.shape
    nb = T // CMP_STRIDE
    width = w0.shape[0]
    half = width // 2
    n_tiles = T // LANES
    gmat = jnp.asarray(_group_sum_stack(n_tiles), BF16)
    const = lambda shape: pl.BlockSpec(shape, lambda b: (0,) * len(shape))
    return pl.pallas_call(
        _cmp_build_kernel,
        grid=(B,),
        in_specs=[pl.BlockSpec((None, width, T), lambda b: (b, 0, 0)),
                  const((width, LANES)), const((width, LANES)), const((n_tiles, 2 * LANES, 2 * LANES)), const((width, 1)),
                  const((half, half)), const((half, half))],
        out_specs=[pl.BlockSpec((None, half, nb), lambda b: (b, 0, 0))] * 2,
        out_shape=[jax.ShapeDtypeStruct((B, half, nb), F32)] * 2,
        compiler_params=_cparams(1),
        name="cmp_build",
    )(cs_t, w0, w1, gmat, b_col, w0t, w1t)


def _nsa_kernel(q_ref, ck_ref, cv_ref, sk_ref, sv_ref, wk_ref, wv_ref, gt_ref, bg_ref, cb_ref, tb_ref, cov_ref, ex_ref,
                o_ref, sk2, sv2, wk2, wv2, ck2, cv2, madd, p_ref, m_ref, l_ref, a_ref, acc_ref,
                *, tq, tk, n_blk, n_selb, n_top, win_chunks):
    qi = pl.program_id(2)

    @pl.when(qi == 0)
    def _():
        for src, dst in ((sk_ref, sk2), (sv_ref, sv2), (wk_ref, wk2), (wv_ref, wv2), (ck_ref, ck2), (cv_ref, cv2)):
            x = src[...].astype(BF16)
            dst[0:HEAD_DIM, :] = x
            dst[HEAD_DIM:, :] = x

    T = sk_ref.shape[1]
    nb = ck_ref.shape[1]
    nd = tb_ref.shape[1]
    sub = tk // tq
    t0 = qi * tq
    lane = lax.broadcasted_iota(I32, (1, LANES), 1)
    lo = lane < HEAD_DIM
    tcol = t0 + lax.broadcasted_iota(I32, (tq, 1), 0)
    q = q_ref[...] * SCALE
    parts = []
    for j in range(4):
        tile = q[:, (j // 2) * LANES:(j // 2 + 1) * LANES]
        parts.append(jnp.where(lo if j % 2 == 0 else jnp.logical_not(lo), tile, 0.0))
    qs = jnp.concatenate(parts, axis=0).astype(BF16)
    gates = 1.0 / (1.0 + jnp.exp(-(gt_ref[...] + bg_ref[...])))

    n_idx = lax.broadcasted_iota(I32, (1, nb), 1)
    validc = jnp.logical_and(n_idx * CMP_STRIDE + (CMP_BLOCK - 1) <= tcol, n_idx < n_blk)
    sc = _dot(qs, ck2[...])
    o_c = []
    psum = jnp.zeros((tq, nb), F32)
    for j in range(4):
        s = jnp.where(validc, sc[j * tq:(j + 1) * tq] + cb_ref[j], NEG)
        m = jnp.max(s, axis=-1, keepdims=True)
        e = jnp.where(validc, jnp.exp(s - m), 0.0)
        l = jnp.sum(e, axis=-1, keepdims=True)
        p = e / jnp.where(l > 0.0, l, 1.0)
        psum = psum + p
        o_c.append(_dot_nt(p.astype(BF16), cv2[...]))

    score = _dot_exact_rhs01(psum, cov_ref[...])
    cur = tcol // SEL_BLOCK
    forced = jnp.logical_or(lane == 0, jnp.logical_or(lane == cur, lane == cur - 1))
    score = jnp.where(forced, FORCE, score)
    score = jnp.where(lane * SEL_BLOCK <= tcol, score, NEG)
    score = jnp.where(lane < n_selb, score, PAD_SCORE)
    n_selp = -(-n_selb // 8) * 8
    sc_t = score.T[:n_selp]
    jrow = lax.broadcasted_iota(I32, (n_selp, 1), 0)
    rank = jnp.zeros((n_selp, tq), F32)
    for i in range(n_selb):
        ri = sc_t[i:i + 1, :]
        tie = jnp.where(i < jrow, 1.0, 0.0)
        rank = rank + jnp.where(ri > sc_t, 1.0, jnp.where(ri == sc_t, tie, 0.0))
    sel_t = jnp.where(rank < n_top, 1.0, 0.0)
    if n_selp < LANES:
        sel_t = jnp.concatenate([sel_t, jnp.zeros((LANES - n_selp, tq), F32)], axis=0)
    chosen = _dot(sel_t.T.astype(BF16), ex_ref[...])
    kpos = lax.broadcasted_iota(I32, (1, T), 1)
    madd[...] = jnp.where(jnp.logical_and(chosen > 0.5, kpos <= tcol), 0.0, NEG)

    def softmax_rows(j, s, first):
        rows = slice(j * tq, (j + 1) * tq)
        w = s.shape[1]
        rmax = jnp.broadcast_to(jnp.max(s, axis=-1, keepdims=True), (tq, LANES))
        if first:
            m_new = rmax
        else:
            m_prev = m_ref[rows]
            m_new = jnp.maximum(m_prev, rmax)
            alpha = jnp.exp(m_prev - m_new)
            a_ref[rows] = alpha
        p = jnp.exp(s - _lane_tile(m_new, w // LANES))
        rsum = jnp.broadcast_to(jnp.sum(p, axis=-1, keepdims=True), (tq, LANES))
        l_ref[rows] = rsum if first else alpha * l_ref[rows] + rsum
        m_ref[rows] = m_new
        p_ref[rows, 0:w] = p.astype(BF16)

    def toe_bias(j, d0, n_chunks):
        return jnp.concatenate([tb_ref[j, jnp.clip(d0 - cc, 0, nd - 1)] for cc in range(n_chunks)], axis=1)

    kt_d = qi // sub

    def sel_tile(kt, first):
        start = _aligned(kt * tk, tk)
        mask = madd[:, pl.ds(start, tk)]
        s4 = _dot(qs, sk2[:, pl.ds(start, tk)])
        for j in range(4):
            softmax_rows(j, s4[j * tq:(j + 1) * tq] + toe_bias(j, qi - kt * sub, sub) + mask, first)
        pv = _dot_nt(p_ref[:, 0:tk], sv2[:, pl.ds(start, tk)])
        acc_ref[...] = pv if first else a_ref[...] * acc_ref[...] + pv

    sel_tile(kt_d, True)

    def sel_body(i, carry):
        sel_tile(kt_d - 1 - i, False)
        return carry

    lax.fori_loop(0, kt_d, sel_body, 0)
    o_s = acc_ref[...] / l_ref[...]

    ww = win_chunks * LANES
    w0 = jnp.maximum(qi - (win_chunks - 1), 0)
    wstart = pl.multiple_of(w0 * tq, tq)
    dist = tcol - (wstart + lax.broadcasted_iota(I32, (1, ww), 1))
    wmask = jnp.where(jnp.logical_and(dist >= 0, dist <= WINDOW), 0.0, NEG)
    s4 = _dot(qs, wk2[:, pl.ds(wstart, ww)])
    for j in range(4):
        softmax_rows(j, s4[j * tq:(j + 1) * tq] + toe_bias(j, qi - w0, win_chunks) + wmask, True)
    o_w = _dot_nt(p_ref[:, 0:ww], wv2[:, pl.ds(wstart, ww)]) / l_ref[...]

    outs = []
    for j in range(4):
        rows = slice(j * tq, (j + 1) * tq)
        outs.append(gates[:, 3 * j:3 * j + 1] * o_c[j] + gates[:, 3 * j + 1:3 * j + 2] * o_s[rows]
                    + gates[:, 3 * j + 2:3 * j + 3] * o_w[rows])
    o_ref[:, 0:LANES] = jnp.where(lo, outs[0], outs[1])
    o_ref[:, LANES:2 * LANES] = jnp.where(lo, outs[2], outs[3])


def nsa_prompt_attention(qg_arr, ck_t, cv_t, cs_t, w_t, bg_pad, cbias, toe, cover, expand, tq, tk):
    B, T, _ = qg_arr.shape
    nb = ck_t.shape[2]
    nd = toe.shape[1]
    n_selb = T // SEL_BLOCK
    win_chunks = WINDOW // tq + 1
    ww = win_chunks * LANES
    assert tq == LANES and T >= ww and tk % tq == 0 and T % tk == 0
    kern = functools.partial(_nsa_kernel, tq=tq, tk=tk, n_blk=nb - 1, n_selb=n_selb, n_top=min(N_SEL, n_selb),
                             win_chunks=win_chunks)
    rows64 = lambda blk0: pl.BlockSpec((None, HEAD_DIM, T), lambda g, b, i: (b, blk0 + g, 0))
    wide = max(tk, ww)
    return pl.pallas_call(
        kern,
        grid=(4, B, T // tq),
        in_specs=[pl.BlockSpec((None, tq, 2 * LANES), lambda g, b, i: (b, i, g)),
                  pl.BlockSpec((None, HEAD_DIM, nb), lambda g, b, i: (b, g, 0)),
                  pl.BlockSpec((None, HEAD_DIM, nb), lambda g, b, i: (b, g, 0)),
                  rows64(8), rows64(12), rows64(0), rows64(4),
                  pl.BlockSpec((None, tq, LANES), lambda g, b, i: (b, i, 8 + g)),
                  pl.BlockSpec((None, 1, LANES), lambda g, b, i: (g, 0, 0)),
                  pl.BlockSpec((4, tq, nb), lambda g, b, i: (g, i, 0)),
                  pl.BlockSpec((4, nd, tq, tq), lambda g, b, i: (g, 0, 0, 0)),
                  pl.BlockSpec((nb, LANES), lambda g, b, i: (0, 0)),
                  pl.BlockSpec((LANES, T), lambda g, b, i: (0, 0))],
        out_specs=pl.BlockSpec((None, tq, 2 * LANES), lambda g, b, i: (b, i, g)),
        out_shape=jax.ShapeDtypeStruct((B, T, 8 * LANES), F32),
        scratch_shapes=[pltpu.VMEM((LANES, T), BF16)] * 4 + [pltpu.VMEM((LANES, nb), BF16)] * 2
                       + [pltpu.VMEM((tq, T), F32), pltpu.VMEM((4 * tq, wide), BF16)]
                       + [pltpu.VMEM((4 * tq, LANES), F32)] * 4,
        compiler_params=_cparams(3),
        name="nsa_prompt_attention",
    )(qg_arr, ck_t, cv_t, cs_t, cs_t, w_t, w_t, qg_arr, bg_pad, cbias, toe, cover, expand)


def _nsa_t_kernel(q_ref, ckt_ref, cvt_ref, sk_ref, svt_ref, wk_ref, wvt_ref, gt_ref, bg_ref, cbt_ref, tbt_ref, covt_ref, ext_ref,
                  o_ref, svb, wvb, ck2, cvb, maddt, *, tq, tk, n_blk, n_selb, n_top, win_chunks, n_chains):
    qi = pl.program_id(2)

    @pl.when(qi == 0)
    def _():
        ones = jnp.ones((16, svt_ref.shape[1]), BF16)
        svb[0:HEAD_DIM, :] = svt_ref[...].astype(BF16)
        svb[HEAD_DIM:, :] = ones
        wvb[0:HEAD_DIM, :] = wvt_ref[...].astype(BF16)
        wvb[HEAD_DIM:, :] = ones
        cvb[...] = cvt_ref[...].astype(BF16)
        ckt = ckt_ref[...]
        ck2[...] = jnp.concatenate([ckt, ckt], axis=0).T.astype(BF16)

    T = svt_ref.shape[1]
    nb = ckt_ref.shape[1]
    nd = tbt_ref.shape[1]
    sub = tk // tq
    t0 = qi * tq
    lane = lax.broadcasted_iota(I32, (1, LANES), 1)
    lo = lane < HEAD_DIM
    tpos = t0 + lax.broadcasted_iota(I32, (1, tq), 1)
    q = q_ref[...]
    zero = jnp.zeros((), BF16)
    parts = []
    for j in range(4):
        tile = q[:, (j // 2) * LANES:(j // 2 + 1) * LANES]
        parts.append(jnp.where(lo if j % 2 == 0 else jnp.logical_not(lo), tile, zero))
    qs = jnp.concatenate(parts, axis=0)
    gt = 1.0 / (1.0 + jnp.exp(-(gt_ref[...] + bg_ref[...])))

    def gate(r):
        return jnp.concatenate([gt[3 * j + r:3 * j + r + 1, :] for j in range(4)], axis=1)

    n_col = lax.broadcasted_iota(I32, (nb, 1), 0)
    validc = jnp.logical_and(n_col * CMP_STRIDE + (CMP_BLOCK - 1) <= tpos, n_col < n_blk)
    validc4 = _lane_tile(validc, 4)
    sc = _dot_nt(ck2[...], qs) + jnp.concatenate([cbt_ref[j] for j in range(4)], axis=1)
    sc = jnp.where(validc4, sc, NEG)
    m = jnp.max(sc, axis=0, keepdims=True)
    e = jnp.where(validc4, jnp.exp2(sc - m), 0.0)
    l = jnp.sum(e, axis=0, keepdims=True)
    p = e / jnp.where(l > 0.0, l, 1.0)
    o_c = _dot(cvb[...], p.astype(BF16))
    psum = p[:, 0:tq] + p[:, tq:2 * tq] + p[:, 2 * tq:3 * tq] + p[:, 3 * tq:4 * tq]

    score = _dot_exact_lhs01(covt_ref[...], psum)
    jrow = lax.broadcasted_iota(I32, (LANES, 1), 0)
    cur = tpos // SEL_BLOCK
    forced = jnp.logical_or(jrow == 0, jnp.logical_or(jrow == cur, jrow == cur - 1))
    score = jnp.where(forced, FORCE, score)
    score = jnp.where(jrow * SEL_BLOCK <= tpos, score, NEG)
    n_selp = -(-n_selb // 8) * 8
    sc_t = score[:n_selp]
    jr = jrow[:n_selp]
    rank = jnp.zeros((n_selp, tq), F32)
    for i in range(n_selb):
        ri = sc_t[i:i + 1, :]
        tie = jnp.where(i < jr, 1.0, 0.0)
        rank = rank + jnp.where(ri > sc_t, 1.0, jnp.where(ri == sc_t, tie, 0.0))
    sel_t = jnp.where(jnp.logical_and(rank < n_top, jr < n_selb), 1.0, 0.0)
    if n_selp < LANES:
        sel_t = jnp.concatenate([sel_t, jnp.zeros((LANES - n_selp, tq), F32)], axis=0)
    chosen = _dot(ext_ref[...], sel_t.astype(BF16))
    kpos = lax.broadcasted_iota(I32, (T, 1), 0)
    maddt[...] = jnp.where(jnp.logical_and(chosen > 0.5, kpos <= tpos), 0.0, NEG)

    hpc = 4 // n_chains
    qs_c = [qs[c * hpc * tq:(c + 1) * hpc * tq] for c in range(n_chains)]

    def toe_bias(c, d0, n_chunks):
        cols = [jnp.concatenate([tbt_ref[j, jnp.clip(d0 - cc, 0, nd - 1)] for cc in range(n_chunks)], axis=0)
                for j in range(c * hpc, (c + 1) * hpc)]
        return cols[0] if hpc == 1 else jnp.concatenate(cols, axis=1)

    def update(s, vt, state):
        s_max = jnp.max(s, axis=0, keepdims=True)
        if state is None:
            m_new = s_max
        else:
            m_prev, acc_prev = state
            m_new = jnp.maximum(m_prev, s_max)
        p = jnp.exp2((s - m_new).astype(BF16))
        pv = _dot(vt, p)
        if state is None:
            return m_new, pv
        return m_new, jnp.exp2(m_prev - m_new) * acc_prev + pv

    def attend(k_ref, v_ref, start, width, d0, mask, states):
        k = k_ref[pl.ds(start, width), :]
        v = v_ref[:, pl.ds(start, width)]
        mask_c = _lane_tile(mask, hpc)
        return tuple(update(_dot_nt(k, qs_c[c]) + toe_bias(c, d0, width // tq) + mask_c, v,
                            None if states is None else states[c]) for c in range(n_chains))

    kt_d = qi // sub

    def sel_tile(kt, states):
        start = _aligned(kt * tk, tk)
        return attend(sk_ref, svb, start, tk, qi - kt * sub, maddt[pl.ds(start, tk), :], states)

    states = lax.fori_loop(0, kt_d, lambda i, st: sel_tile(kt_d - 1 - i, st), sel_tile(kt_d, None))
    o_s = jnp.concatenate([acc[0:HEAD_DIM] / acc[HEAD_DIM:HEAD_DIM + 1] for _, acc in states], axis=1)

    ww = win_chunks * tq
    w0 = jnp.maximum(qi - (win_chunks - 1), 0)
    wstart = pl.multiple_of(w0 * tq, tq)
    dist = tpos - (wstart + lax.broadcasted_iota(I32, (ww, 1), 0))
    wmask = jnp.where(jnp.logical_and(dist >= 0, dist <= WINDOW), 0.0, NEG)
    states = attend(wk_ref, wvb, wstart, ww, qi - w0, wmask, None)
    o_w = jnp.concatenate([acc[0:HEAD_DIM] / acc[HEAD_DIM:HEAD_DIM + 1] for _, acc in states], axis=1)

    o = gate(0) * o_c + gate(1) * o_s + gate(2) * o_w
    for j in range(4):
        o_ref[j * HEAD_DIM:(j + 1) * HEAD_DIM, :] = o[:, j * tq:(j + 1) * tq]


def nsa_prompt_attention_t(qk_arr, ck_t, cv_t, cs_t, wg_t, bg_col, cbias_t, toe_t, cover_t, expand_t, tq, tk):
    B, T, _ = qk_arr.shape
    nb = ck_t.shape[2]
    nd = toe_t.shape[1]
    n_selb = T // SEL_BLOCK
    win_chunks = WINDOW // tq + 1
    assert tq == LANES and T >= win_chunks * tq and tk % tq == 0 and T % tk == 0 and nb <= LANES
    kern = functools.partial(_nsa_t_kernel, tq=tq, tk=tk, n_blk=nb - 1, n_selb=n_selb, n_top=min(N_SEL, n_selb),
                             win_chunks=win_chunks, n_chains=1)
    return pl.pallas_call(
        kern,
        grid=(4, B, T // tq),
        in_specs=[pl.BlockSpec((None, tq, 2 * LANES), lambda g, b, i: (b, i, g)),
                  pl.BlockSpec((None, HEAD_DIM, nb), lambda g, b, i: (b, g, 0)),
                  pl.BlockSpec((None, HEAD_DIM, nb), lambda g, b, i: (b, g, 0)),
                  pl.BlockSpec((None, T, LANES), lambda g, b, i: (b, 0, 8 + g)),
                  pl.BlockSpec((None, HEAD_DIM, T), lambda g, b, i: (b, 12 + g, 0)),
                  pl.BlockSpec((None, T, LANES), lambda g, b, i: (b, 0, 12 + g)),
                  pl.BlockSpec((None, HEAD_DIM, T), lambda g, b, i: (b, 4 + g, 0)),
                  pl.BlockSpec((None, 16, tq), lambda g, b, i: (b, 32 + g, i)),
                  pl.BlockSpec((None, 16, 1), lambda g, b, i: (g, 0, 0)),
                  pl.BlockSpec((4, nb, tq), lambda g, b, i: (g, 0, i)),
                  pl.BlockSpec((4, nd, tq, tq), lambda g, b, i: (g, 0, 0, 0)),
                  pl.BlockSpec((LANES, nb), lambda g, b, i: (0, 0)),
                  pl.BlockSpec((T, LANES), lambda g, b, i: (0, 0))],
        out_specs=pl.BlockSpec((None, 4 * HEAD_DIM, tq), lambda g, b, i: (b, g, i)),
        out_shape=jax.ShapeDtypeStruct((B, 16 * HEAD_DIM, T), F32),
        scratch_shapes=[pltpu.VMEM((HEAD_DIM + 16, T), BF16), pltpu.VMEM((HEAD_DIM + 16, T), BF16),
                        pltpu.VMEM((nb, LANES), BF16), pltpu.VMEM((HEAD_DIM, nb), BF16), pltpu.VMEM((T, tq), F32)],
        compiler_params=_cparams(3),
        name="nsa_prompt_attention_t",
    )(qk_arr, ck_t, cv_t, qk_arr, cs_t, qk_arr, wg_t, wg_t, bg_col, cbias_t, toe_t, cover_t, expand_t)


def _out_res_t_kernel(a_ref, w_ref, g_ref, r_ref, o_ref):
    y = _dot(a_ref[...].T.astype(BF16), w_ref[...])
    o_ref[...] = r_ref[...] + _norm_rows(y, g_ref[...])


def out_proj_residual_t(a_t, w, g, res):
    B, K, T = a_t.shape
    D = w.shape[1]
    tm = min(T, 512)
    nt = T // tm
    return pl.pallas_call(
        _out_res_t_kernel,
        grid=(B, nt),
        in_specs=[pl.BlockSpec((None, K, tm), lambda b, i: (b, 0, i)),
                  pl.BlockSpec((K, D), lambda b, i: (0, 0)),
                  pl.BlockSpec((1, D), lambda b, i: (0, 0)),
                  pl.BlockSpec((tm, D), lambda b, i: (b * nt + i, 0))],
        out_specs=pl.BlockSpec((tm, D), lambda b, i: (b * nt + i, 0)),
        out_shape=jax.ShapeDtypeStruct((B * T, D), F32),
        compiler_params=_cparams(2),
        name="out_proj_residual_t",
    )(a_t, w, g.reshape(1, D), res)


def cover_matrix_np(n_blk, n_selb, rows, cols):
    i = np.arange(rows)[:, None]
    j = np.arange(cols)[None, :]
    m = (i * CMP_STRIDE < (j + 1) * SEL_BLOCK) & (i * CMP_STRIDE + CMP_BLOCK > j * SEL_BLOCK) & (i < n_blk) & (j < n_selb)
    return m.astype(np.float32)


def _even_dec_kernel(pt_ref, *refs, pps, n_steps, lam_init):
    k_refs = refs[0:pps]
    v_refs = refs[pps:2 * pps]
    lf_refs = refs[2 * pps:3 * pps]
    (qt_ref, cn_ref, tbt_ref, ut_ref, kn_ref, vn_ref, bo_ref, lp_ref, sg_ref,
     o_ref, m_ref, l_ref, acc_ref, carry_ref) = refs[3 * pps:]
    s = pl.program_id(1)
    n_maps, width = qt_ref.shape
    n_fox = lf_refs[0].shape[0]

    @pl.when(s == 0)
    def _():
        m_ref[...] = jnp.full_like(m_ref, NEG)
        l_ref[...] = jnp.zeros_like(l_ref)
        acc_ref[...] = jnp.zeros_like(acc_ref)
        carry_ref[...] = jnp.zeros_like(carry_ref)

    qt = qt_ref[...]
    is_fox = lax.broadcasted_iota(I32, (n_maps, 1), 0) < n_fox
    pad = jnp.zeros((n_maps - n_fox, PAGE_SIZE), F32)
    lfts = jnp.concatenate([jnp.concatenate([lf_refs[i][...], pad], axis=0) for i in range(pps)], axis=0)
    within = _dot_exact_rhs01(lfts, ut_ref[...])
    totals = jnp.sum(lfts, axis=-1, keepdims=True)
    run = carry_ref[...]
    later = [None] * pps
    for i in reversed(range(pps)):
        later[i] = run
        run = run + totals[i * n_maps:(i + 1) * n_maps]
    carry_ref[...] = run
    cn = cn_ref[...]
    sts = []
    for i in range(pps):
        kpg = k_refs[i][...].reshape(width, PAGE_SIZE).astype(BF16)
        suffix = within[i * n_maps:(i + 1) * n_maps] + later[i] + cn
        sts.append(_dot(qt, kpg) + jnp.where(is_fox, suffix, tbt_ref[i]))
    st = jnp.concatenate(sts, axis=1)
    m = m_ref[...]
    m_new = jnp.maximum(m, jnp.max(st, axis=-1, keepdims=True))
    alpha = jnp.exp(m - m_new)
    p = jnp.exp(st - m_new)
    l_ref[...] = alpha * l_ref[...] + jnp.sum(p, axis=-1, keepdims=True)
    pb = p.astype(BF16)
    pv = None
    for i in range(pps):
        vpg = v_refs[i][...].reshape(width, PAGE_SIZE).astype(BF16)
        d = _dot_nt(pb[:, i * PAGE_SIZE:(i + 1) * PAGE_SIZE], vpg)
        pv = d if pv is None else pv + d
    acc_ref[...] = alpha * acc_ref[...] + pv
    m_ref[...] = m_new

    @pl.when(s == n_steps - 1)
    def _():
        kn = kn_ref[...].astype(BF16).astype(F32)
        vn = vn_ref[...].astype(BF16).astype(F32)
        s_own = jnp.sum(qt.astype(F32) * kn, axis=-1, keepdims=True) + bo_ref[...]
        m = m_ref[...]
        m_all = jnp.maximum(m, s_own)
        a = jnp.exp(m - m_all)
        e_own = jnp.exp(s_own - m_all)
        l_all = a * l_ref[...] + e_own
        o = (a * acc_ref[...] + e_own * vn) / l_all
        r = lax.broadcasted_iota(I32, (n_maps, width), 0)
        cidx = lax.broadcasted_iota(I32, (n_maps, width), 1)
        d_fox = n_fox * HEAD_DIM
        fox_sel = jnp.logical_and(cidx < d_fox, r == cidx // HEAD_DIM)
        dh = (cidx - d_fox) // LANES
        d1_sel = jnp.logical_and(cidx >= d_fox, r == n_fox + 2 * dh)
        d2_sel = jnp.logical_and(cidx >= d_fox, r == n_fox + 2 * dh + 1)
        o_f = jnp.sum(jnp.where(fox_sel, o, 0.0), axis=0, keepdims=True)
        a1 = jnp.sum(jnp.where(d1_sel, o, 0.0), axis=0, keepdims=True)
        a2 = jnp.sum(jnp.where(d2_sel, o, 0.0), axis=0, keepdims=True)
        lam = _lambda_value(lp_ref[...], lam_init)
        o_d = a1 - lam * a2
        for tl in range(width // LANES):
            sl = slice(tl * LANES, (tl + 1) * LANES)
            if tl * LANES < d_fox:
                o_ref[:, sl] = o_f[:, sl]
            else:
                x = o_d[:, sl]
                ms = jnp.mean(x * x, axis=-1, keepdims=True)
                o_ref[:, sl] = x * lax.rsqrt(ms + EPS) * sg_ref[...] * (1.0 - lam_init)


def even_decode(pool_t, lf_t, li, pt_flat, n_pages, qt, cn, tbt, kn, vn, bo, lam_p, subln_g, lam_init, pps):
    DB, n_maps, width = qt.shape
    n_fox = lf_t.shape[2]
    n_steps = n_pages // pps
    ut = jnp.asarray(np.tril(np.ones((PAGE_SIZE, PAGE_SIZE), np.float32), -1), BF16)

    def page(i):
        return lambda b, s, pt: pt[b * n_pages + (n_steps - 1 - s) * pps + i]

    kv_block = (None, None, None, n_maps, HEAD_DIM, PAGE_SIZE)
    k_specs = [pl.BlockSpec(kv_block, lambda b, s, pt, f=page(i): (li, f(b, s, pt), 0, 0, 0, 0)) for i in range(pps)]
    v_specs = [pl.BlockSpec(kv_block, lambda b, s, pt, f=page(i): (li, f(b, s, pt), 1, 0, 0, 0)) for i in range(pps)]
    lf_specs = [pl.BlockSpec((None, None, n_fox, PAGE_SIZE), lambda b, s, pt, f=page(i): (li, f(b, s, pt), 0, 0)) for i in range(pps)]
    per_b = lambda shape: pl.BlockSpec((None,) + shape, lambda b, s, pt: (b,) + (0,) * len(shape))
    const = lambda shape: pl.BlockSpec(shape, lambda b, s, pt: (0,) * len(shape))
    grid_spec = pltpu.PrefetchScalarGridSpec(
        num_scalar_prefetch=1,
        grid=(DB, n_steps),
        in_specs=k_specs + v_specs + lf_specs + [
            per_b((n_maps, width)), per_b((n_maps, 1)),
            pl.BlockSpec((pps, n_maps, PAGE_SIZE), lambda b, s, pt: (n_steps - 1 - s, 0, 0)),
            const((PAGE_SIZE, PAGE_SIZE)),
            per_b((1, width)), per_b((1, width)),
            const((n_maps, 1)), const((4, HEAD_DIM)), const((1, LANES))],
        out_specs=per_b((1, width)),
        scratch_shapes=[pltpu.VMEM((n_maps, 1), F32), pltpu.VMEM((n_maps, 1), F32), pltpu.VMEM((n_maps, width), F32),
                        pltpu.VMEM((n_maps, 1), F32)])
    kern = functools.partial(_even_dec_kernel, pps=pps, n_steps=n_steps, lam_init=lam_init)
    return pl.pallas_call(
        kern, grid_spec=grid_spec,
        out_shape=jax.ShapeDtypeStruct((DB, 1, width), F32),
        compiler_params=_cparams(2),
        name="even_decode",
    )(pt_flat, *([pool_t] * (2 * pps)), *([lf_t] * pps), qt, cn, tbt, ut, kn, vn, bo, lam_p, subln_g.reshape(1, LANES))


def _odd_cmp_kernel(pt_ref, *refs, pps, n_steps, n_selb, n_top, t_pos):
    r_refs = refs[0:pps]
    (w0_ref, w1_ref, g_ref, b_ref, w0t_ref, w1t_ref, qz_ref, cb_ref, cov_ref, oc_ref, idx_ref, p0, p1) = refs[pps:]
    s = pl.program_id(1)
    width = w0_ref.shape[0]
    w0 = w0_ref[...]
    w1 = w1_ref[...]
    acc = None
    for i in range(pps):
        x = r_refs[i][...].reshape(width, PAGE_SIZE)
        a = _pool2(x, w0, w1, g_ref[i], False)
        acc = a if acc is None else acc + a
    col = pl.multiple_of(s * LANES, LANES)
    p0[:, pl.ds(col, LANES)] = acc[:, 0:LANES]
    p1[:, pl.ds(col, LANES)] = acc[:, LANES:]

    @pl.when(s == n_steps - 1)
    def _():
        nb = p0.shape[1]
        ck, cv = _cmp_finish_t(p0[...], p1[...], b_ref[...], w0t_ref[...], w1t_ref[...])
        st = _dot(qz_ref[...], ck.astype(BF16)) + cb_ref[...]
        n_idx = lax.broadcasted_iota(I32, (1, nb), 1)
        valid = n_idx * CMP_STRIDE + (CMP_BLOCK - 1) <= t_pos
        sm = jnp.where(valid, st, NEG)
        m = jnp.max(sm, axis=-1, keepdims=True)
        e = jnp.where(valid, jnp.exp(sm - m), 0.0)
        l = jnp.sum(e, axis=-1, keepdims=True)
        p = e / jnp.where(l > 0.0, l, 1.0)
        oc_ref[...] = _dot_nt(p.astype(BF16), cv.astype(BF16))
        gqa = p.shape[0] // 4
        rows = [jnp.sum(p[g * gqa:(g + 1) * gqa], axis=0, keepdims=True) for g in range(4)]
        psum = jnp.concatenate(rows + rows, axis=0)
        score = _dot_exact_rhs01(psum, cov_ref[...])
        nsp = score.shape[1]
        jb = lax.broadcasted_iota(I32, (1, nsp), 1)
        cur = t_pos // SEL_BLOCK
        forced = jnp.logical_or(jb == 0, jnp.logical_or(jb == cur, jb == cur - 1))
        score = jnp.where(forced, FORCE, score)
        score = jnp.where(jb * SEL_BLOCK <= t_pos, score, NEG)
        score = jnp.where(jb < n_selb, score, PAD_SCORE)
        jf = jb.astype(F32)
        slot = lax.broadcasted_iota(I32, (1, LANES), 1)
        picks = jnp.zeros((8, LANES), F32)
        for it in range(n_top):
            mx = jnp.max(score, axis=-1, keepdims=True)
            ix = jnp.min(jnp.where(score == mx, jf, 1e9), axis=-1, keepdims=True)
            picks = jnp.where(slot == it, ix, picks)
            score = jnp.where(jf == ix, PAD_SCORE, score)
        idx_ref[...] = picks.astype(I32)


def odd_decode_cmp(pool_t, li, pt_flat, n_pages, w0, w1, b_col, w0t, w1t, qz16, cbias, cover, n_selb, t_pos):
    DB = qz16.shape[0]
    width = w0.shape[0]
    half = width // 2
    pps = LANES // (PAGE_SIZE // CMP_STRIDE)
    assert n_pages % pps == 0
    n_steps = n_pages // pps
    nb = n_pages * PAGE_SIZE // CMP_STRIDE
    nsp = cover.shape[1]
    spp = PAGE_SIZE // CMP_STRIDE
    gstack = jnp.asarray(_group_sum_stack(pps), BF16)
    r_specs = [pl.BlockSpec((None, None, 2, 4, HEAD_DIM, PAGE_SIZE),
                            lambda b, s, pt, i=i: (li, pt[b * n_pages + s * pps + i], 0, 0, 0, 0)) for i in range(pps)]
    per_b = lambda shape: pl.BlockSpec((None,) + shape, lambda b, s, pt: (b,) + (0,) * len(shape))
    const = lambda shape: pl.BlockSpec(shape, lambda b, s, pt: (0,) * len(shape))
    grid_spec = pltpu.PrefetchScalarGridSpec(
        num_scalar_prefetch=1,
        grid=(DB, n_steps),
        in_specs=r_specs + [const((width, LANES)), const((width, LANES)), const((pps, 2 * LANES, 2 * LANES)), const((width, 1)),
                            const((half, half)), const((half, half)),
                            per_b((16, half)), const((16, nb)), const((nb, nsp))],
        out_specs=[per_b((16, half)), per_b((8, LANES))],
        scratch_shapes=[pltpu.VMEM((width, nb), F32), pltpu.VMEM((width, nb), F32)])
    kern = functools.partial(_odd_cmp_kernel, pps=pps, n_steps=n_steps, n_selb=n_selb, n_top=min(N_SEL, n_selb), t_pos=t_pos)
    return pl.pallas_call(
        kern, grid_spec=grid_spec,
        out_shape=[jax.ShapeDtypeStruct((DB, 16, half), F32), jax.ShapeDtypeStruct((DB, 8, LANES), I32)],
        compiler_params=_cparams(2),
        name="odd_decode_cmp",
    )(pt_flat, *([pool_t] * pps), w0, w1, gstack, b_col, w0t, w1t, qz16, cbias, cover)


def _odd_sel_kernel(idx_ref, pt_ref, *refs, n_top, n_past_blk, bpp, t_pos):
    k_refs = refs[0:n_top]
    v_refs = refs[n_top:2 * n_top]
    (q_ref, sn_ref, tb_ref, wk_ref, wv_ref, wn_ref, tw_ref, tw0_ref, oc_ref, gt_ref, o_ref) = refs[2 * n_top:]
    b = pl.program_id(0)
    g = pl.program_id(1)
    q = q_ref[...]
    qf = q.astype(F32)
    lane = lax.broadcasted_iota(I32, (1, PAGE_SIZE), 1)
    bias0 = tw0_ref[:, 0:1]

    logits, vals = [], []
    n_new = jnp.zeros((), I32)
    for k in range(n_top):
        blk = idx_ref[(b * 4 + g) * n_top + k]
        is_new = blk >= n_past_blk
        n_new = n_new + is_new.astype(I32)
        page = jnp.minimum(blk // bpp, n_past_blk // bpp - 1)
        bias = tb_ref[pl.ds(pl.multiple_of((page * 4 + g) * 8, 8), 8), :]
        s = _dot(q, k_refs[k][...].astype(BF16)) + bias
        pos = page * PAGE_SIZE + lane
        valid = jnp.logical_and(jnp.logical_and(pos // SEL_BLOCK == blk, pos <= t_pos), jnp.logical_not(is_new))
        logits.append(jnp.where(valid, s, NEG))
        vals.append(v_refs[k][...].astype(BF16))
    has_new = n_new > 0
    kn = sn_ref[0:1, :].astype(BF16).astype(F32)
    vn = sn_ref[1:2, :].astype(BF16).astype(F32)
    s_new = jnp.where(has_new, jnp.sum(qf * kn, axis=-1, keepdims=True) + bias0, NEG)
    m = s_new
    for s in logits:
        m = jnp.maximum(m, s.max(axis=-1, keepdims=True))
    p_new = jnp.where(has_new, jnp.exp(s_new - m), 0.0)
    l = p_new
    acc = p_new * vn
    for s, vt in zip(logits, vals):
        p = jnp.where(s > 0.5 * NEG, jnp.exp(s - m), 0.0)
        l = l + jnp.sum(p, axis=-1, keepdims=True)
        acc = acc + _dot_nt(p.astype(BF16), vt)
    o_s = acc / jnp.where(l > 0.0, l, 1.0)

    sw = _dot(q, wk_ref[...].astype(BF16)) + tw_ref[...]
    kwn = wn_ref[0:1, :].astype(BF16).astype(F32)
    vwn = wn_ref[1:2, :].astype(BF16).astype(F32)
    sw_new = jnp.sum(qf * kwn, axis=-1, keepdims=True) + bias0
    mw = jnp.maximum(jnp.max(sw, axis=-1, keepdims=True), sw_new)
    pw = jnp.exp(sw - mw)
    pw_new = jnp.exp(sw_new - mw)
    lw = jnp.sum(pw, axis=-1, keepdims=True) + pw_new
    o_w = (_dot_nt(pw.astype(BF16), wv_ref[...].astype(BF16)) + pw_new * vwn) / lw

    gates = 1.0 / (1.0 + jnp.exp(-gt_ref[...]))
    o_ref[...] = gates[:, 0:1] * oc_ref[...] + gates[:, 1:2] * o_s + gates[:, 2:3] * o_w


def odd_decode_sel(pool_t, li, idx_flat, pt_flat, n_pages, win_t, qg, s_new, tbp, w_new, tw, tw0, oc_g, graw, n_top, t_pos):
    DB = qg.shape[0]
    bpp = PAGE_SIZE // SEL_BLOCK
    wb = win_t.shape[-1]
    n_past_blk = n_pages * bpp

    def blk_spec(k, which):
        def imap(b, g, idx, pt):
            blk = idx[(b * 4 + g) * n_top + k]
            page = pt[b * n_pages + jnp.minimum(blk // bpp, n_pages - 1)]
            return (li, page, which, g, 0, 0)
        return pl.BlockSpec((None, None, None, None, HEAD_DIM, PAGE_SIZE), imap)

    per_bg = lambda shape: pl.BlockSpec((None, None) + shape, lambda b, g, idx, pt: (b, g) + (0,) * len(shape))
    grid_spec = pltpu.PrefetchScalarGridSpec(
        num_scalar_prefetch=2,
        grid=(DB, 4),
        in_specs=[blk_spec(k, 2) for k in range(n_top)] + [blk_spec(k, 3) for k in range(n_top)] + [
            per_bg((8, HEAD_DIM)), per_bg((2, HEAD_DIM)),
            pl.BlockSpec(tbp.shape, lambda b, g, idx, pt: (0, 0)),
            pl.BlockSpec((None, None, None, None, HEAD_DIM, wb), lambda b, g, idx, pt: (li, b, 0, g, 0, 0)),
            pl.BlockSpec((None, None, None, None, HEAD_DIM, wb), lambda b, g, idx, pt: (li, b, 1, g, 0, 0)),
            per_bg((2, HEAD_DIM)),
            pl.BlockSpec((8, wb), lambda b, g, idx, pt: (g, 0)),
            pl.BlockSpec((8, LANES), lambda b, g, idx, pt: (g, 0)),
            per_bg((8, HEAD_DIM)), per_bg((8, LANES))],
        out_specs=per_bg((8, HEAD_DIM)))
    kern = functools.partial(_odd_sel_kernel, n_top=n_top, n_past_blk=n_past_blk, bpp=bpp, t_pos=t_pos)
    return pl.pallas_call(
        kern, grid_spec=grid_spec,
        out_shape=jax.ShapeDtypeStruct((DB, 4, 8, HEAD_DIM), F32),
        compiler_params=_cparams(2),
        name="odd_decode_sel",
    )(idx_flat, pt_flat, *([pool_t] * (2 * n_top)), qg, s_new, tbp, win_t, win_t, w_new, tw, tw0, oc_g, graw)


def _pad_cols(w, n):
    return jnp.pad(w, ((0, 0), (0, n - w.shape[1])))


def _pad_rows8(x4):
    pad = [(0, 0)] * x4.ndim
    pad[-2] = (0, 4)
    return jnp.pad(x4, pad)


def even_layer(hp, hs, pool_kv, pool_lf, li, page_table, g0, w_in, b_f, lam_p, subln_g, rel_table, layer):
    B, T, D = hp.shape
    DB, S, _ = hs.shape
    assert S == 1
    n_fox, d_fox = 8, 512
    lam_init = 0.8 - 0.6 * math.exp(-0.3 * layer)
    cuts = np.cumsum((d_fox,) * 6)
    q_f, k_f, v_f, q_d, k_d, v_d, f_w = jnp.split(w_in, [int(c) for c in cuts], axis=1)
    w_q = jnp.concatenate([q_f, q_d, _pad_cols(f_w, LANES)], axis=1).astype(BF16)
    w_kv = jnp.concatenate([k_f, k_d, v_f, v_d], axis=1).astype(BF16)
    bf_pad = jnp.pad(b_f.astype(F32), (0, LANES - n_fox)).reshape(1, LANES)
    table4 = rel_table[:, :4]

    q_arr = norm_proj(hp.reshape(B * T, D), g0, w_q, 384).reshape(B, T, -1)
    kv_t = norm_proj_t(hp, g0, w_kv.T, 512)
    tq = min(T, 256)
    tk = min(T, 512)
    lf_pad, cum_pad = logf_cumsum(q_arr, 8, bf_pad, tq)
    cum_pairs = cum_pad[:, :, :n_fox].transpose(0, 2, 1).reshape(B, 4, 2, T)
    toe = bias_table(table4, toeplitz_dist(tq, T // tq))
    attn_p = even_prompt_attention(q_arr, kv_t, cum_pairs, toe, lam_p, subln_g, lam_init, tq, tk)
    ekv_p = kv_t.reshape(B, 2, 16, HEAD_DIM, T).transpose(0, 4, 1, 2, 3)
    elf_p = lf_pad[:, :, :n_fox]

    n_pages = page_table.shape[1]
    past = n_pages * PAGE_SIZE
    xs = hs.reshape(DB, D)
    qs_arr = norm_proj(xs, g0, w_q, 384)
    kvs_arr = norm_proj(xs, g0, w_kv, 512)
    lfs_pad, _ = logf_cumsum(qs_arr.reshape(1, DB, -1), 8, bf_pad, DB)
    lf_new = lfs_pad[0, :, :n_fox]
    seg = np.zeros((16, 1024), np.float32)
    for m_ in range(16):
        seg[m_, m_ * HEAD_DIM:(m_ + 1) * HEAD_DIM] = 1.0
    qt = (qs_arr[:, None, :1024] * SCALE * seg[None]).astype(BF16)
    cn = jnp.pad(lf_new, ((0, 0), (0, 8)))[:, :, None]
    pos = jnp.arange(past).reshape(n_pages, PAGE_SIZE)
    rb = jnp.repeat(bias_table(table4, past - pos), 2, axis=0)
    tbt = jnp.concatenate([jnp.zeros((n_pages, 8, PAGE_SIZE), F32), rb.transpose(1, 0, 2)], axis=1)
    bo = jnp.concatenate([jnp.zeros((8,), F32), jnp.repeat(rel_table[0, :4], 2)]).reshape(16, 1)
    pool_t = pool_kv.transpose(0, 1, 3, 4, 5, 2)
    lf_t = pool_lf.transpose(0, 1, 3, 2)
    attn_s = even_decode(pool_t, lf_t, li, page_table.reshape(-1), n_pages, qt, cn, tbt,
                         kvs_arr[:, None, :1024], kvs_arr[:, None, 1024:], bo, lam_p, subln_g, lam_init,
                         pps=math.gcd(n_pages, 8))
    ekv_s = kvs_arr.reshape(DB, 1, 2, 16, HEAD_DIM)
    elf_s = lf_new.reshape(DB, 1, n_fox)
    return attn_p.reshape(B * T, -1), attn_s.reshape(DB, -1), ekv_p, ekv_s, elf_p, elf_s


def odd_layer(hp, hs, pool, win_state, li, page_table, g0, w_in, b_gate, w_pos, b_cmp, w_cout, rel_table):
    B, T, D = hp.shape
    DB, S, _ = hs.shape
    assert S == 1
    d_q, d_kv = 1024, 512
    w_q, w_c, w_s, w_w, w_g = jnp.split(w_in, [d_q, d_q + d_kv, d_q + 2 * d_kv, d_q + 3 * d_kv], axis=1)
    n_gate = 12
    w_g4 = jnp.concatenate([_pad_cols(w_g[:, g * n_gate:(g + 1) * n_gate], LANES) for g in range(4)], axis=1)
    w_qg = jnp.concatenate([w_q, w_g4], axis=1).astype(BF16)
    w_cs = jnp.concatenate([w_c, w_s], axis=1).astype(BF16)
    w_w16 = w_w.astype(BF16)
    bg = b_gate.astype(F32)
    bg_pad = jnp.pad(bg.reshape(4, 1, n_gate), ((0, 0), (0, 0), (0, LANES - n_gate)))
    wp = w_pos.reshape(CMP_BLOCK, d_kv)
    w0, w1 = _pool_weights_t(wp)
    b_col = b_cmp.reshape(d_kv, 1).astype(F32)
    eye4 = jnp.eye(4, dtype=F32)
    w0t = jnp.kron(eye4, w_cout[0].T).astype(BF16)
    w1t = jnp.kron(eye4, w_cout[1].T).astype(BF16)
    table = rel_table

    d_g = d_kv // 4
    dup = lambda w: jnp.concatenate([w[:, g * HEAD_DIM:(g + 1) * HEAD_DIM] for g in range(4) for _ in range(2)], axis=1)
    w_qk = jnp.concatenate([w_q * (SCALE * LOG2E), dup(w_s[:, :2 * d_g]), dup(w_w[:, :2 * d_g])], axis=1).astype(BF16)
    w_g16 = jnp.concatenate([_pad_cols(w_g[:, g * n_gate:(g + 1) * n_gate], 16) for g in range(4)], axis=1)
    w_wg_t = jnp.concatenate([w_w, w_g16], axis=1).T.astype(BF16)
    bg_col = jnp.pad(bg.reshape(4, n_gate, 1), ((0, 0), (0, 16 - n_gate), (0, 0)))
    qk_arr = norm_proj(hp.reshape(B * T, D), g0, w_qk, 512, BF16).reshape(B, T, -1)
    cs_t = norm_proj_t(hp, g0, w_cs.T, 512)
    wg_t = norm_proj_t(hp, g0, w_wg_t, w_wg_t.shape[0])
    ck_t, cv_t = cmp_build(cs_t, w0, w1, b_col, w0t, w1t)
    nb = T // CMP_STRIDE
    n_blk = nb - 1
    n_selb = T // SEL_BLOCK
    tq = LANES
    tk = min(T, 512)
    t_all = jnp.arange(T)
    e_pos = jnp.arange(nb) * CMP_STRIDE + CMP_BLOCK - 1
    cbias_t = LOG2E * bias_table(table, t_all[None, :] - e_pos[:, None])
    toe_t = LOG2E * bias_table(table, toeplitz_dist(tq, T // tq).transpose(0, 2, 1))
    cover_t = jnp.asarray(cover_matrix_np(n_blk, n_selb, nb, LANES).T, BF16)
    expand_t = np.zeros((T, LANES), np.float32)
    expand_t[np.arange(T), np.arange(T) // SEL_BLOCK] = 1.0
    attn_p = nsa_prompt_attention_t(qk_arr, ck_t, cv_t, cs_t, wg_t, bg_col, cbias_t, toe_t, cover_t,
                                    jnp.asarray(expand_t, BF16), tq, tk)
    okv_p = cs_t.reshape(B, 4, 4, HEAD_DIM, T).transpose(0, 4, 1, 2, 3)
    wn = min(WINDOW, T)
    win_p = wg_t[:, :d_kv, T - wn:].reshape(B, 2, 4, HEAD_DIM, wn).transpose(0, 4, 1, 2, 3)

    n_pages = page_table.shape[1]
    past = n_pages * PAGE_SIZE
    xs = hs.reshape(DB, D)
    qgs = norm_proj(xs, g0, w_qg, 512)
    css = norm_proj(xs, g0, w_cs, 512)
    wns = norm_proj(xs, g0, w_w16, 512)
    pt_flat = page_table.reshape(-1)
    nbs = past // CMP_STRIDE
    n_past_blk = past // SEL_BLOCK
    n_selb_s = n_past_blk + 1
    nsp = -(-n_selb_s // LANES) * LANES
    qs = qgs[:, :d_q]
    segz = np.zeros((16, 256), np.float32)
    for h_ in range(16):
        segz[h_, (h_ // 4) * HEAD_DIM:(h_ // 4 + 1) * HEAD_DIM] = 1.0
    qz16 = (jnp.tile(qs.reshape(DB, 16, 1, HEAD_DIM), (1, 1, 4, 1)).reshape(DB, 16, 256) * SCALE * segz[None]).astype(BF16)
    e_pos_s = jnp.arange(nbs) * CMP_STRIDE + CMP_BLOCK - 1
    cbias_s = bias_table(table, past - e_pos_s)
    cover_s = jnp.asarray(cover_matrix_np(nbs, n_selb_s, nbs, nsp), BF16)
    pool_t = pool.transpose(0, 1, 3, 4, 5, 2)
    n_top = min(N_SEL, n_selb_s)
    oc16, idx8 = odd_decode_cmp(pool_t, li, pt_flat, n_pages, w0, w1, b_col, w0t, w1t, qz16, cbias_s, cover_s,
                                n_selb_s, past)
    idx_flat = idx8[:, :4, :n_top].reshape(-1)
    oc4 = oc16.reshape(DB, 4, 4, 4, HEAD_DIM)[:, np.arange(4), :, np.arange(4)].transpose(1, 0, 2, 3)
    qg = _pad_rows8(qs.reshape(DB, 4, 4, HEAD_DIM) * SCALE).astype(BF16)
    pos_p = jnp.arange(past).reshape(n_pages, PAGE_SIZE)
    tbp = bias_table(table, past - pos_p).reshape(4, 4, n_pages, PAGE_SIZE).transpose(2, 0, 1, 3)
    tbp = _pad_rows8(tbp).reshape(n_pages * 32, PAGE_SIZE)
    wb = win_state.shape[2]
    tw = _pad_rows8(bias_table(table, wb - jnp.arange(wb)).reshape(4, 4, wb)).reshape(32, wb)
    tw0 = _pad_rows8(jnp.broadcast_to(table[0].reshape(4, 4, 1), (4, 4, LANES))).reshape(32, LANES)
    g48 = jnp.concatenate([qgs[:, d_q + g * LANES:d_q + g * LANES + n_gate] for g in range(4)], axis=1) + bg[None]
    graw = jnp.pad(_pad_rows8(g48.reshape(DB, 4, 4, 3)), ((0, 0), (0, 0), (0, 0), (0, LANES - 3)))
    win_t = win_state.transpose(0, 1, 3, 4, 5, 2)
    s_new = css[:, d_kv:].reshape(DB, 2, 4, HEAD_DIM).transpose(0, 2, 1, 3)
    w_new = wns.reshape(DB, 2, 4, HEAD_DIM).transpose(0, 2, 1, 3)
    o4 = odd_decode_sel(pool_t, li, idx_flat, pt_flat, n_pages, win_t, qg, s_new, tbp, w_new, tw, tw0,
                        _pad_rows8(oc4), graw, n_top, past)
    attn_s = o4[:, :, :4, :].reshape(DB, d_q)
    okv_s = css.reshape(DB, 1, 4, 4, HEAD_DIM)
    keys_t = jnp.concatenate([win_t[li], wns.reshape(DB, 2, 4, HEAD_DIM, 1)], axis=-1)
    keep = min(WINDOW, wb + 1)
    win_s = keys_t[..., wb + 1 - keep:].transpose(0, 4, 1, 2, 3)
    return attn_p, attn_s, okv_p, okv_s, win_p, win_s


def kernel(x_prompt, x_sample, cache_even_kv, cache_even_logf, cache_odd_kv, state_odd_win, page_table, rel_table, norm_g, w_even_in, b_even_f, diff_lambda, diff_subln_g, w_even_out, w_odd_in, b_odd_gate, w_cmp_pos, b_cmp, w_cmp_out, w_odd_out, w_ffn_in, w_ffn_out):
    B, T, D = x_prompt.shape
    DB, S, _ = x_sample.shape
    depth = norm_g.shape[0]
    hp = x_prompt.reshape(B * T, D)
    hs = x_sample.reshape(DB * S, D)
    ekv_p, ekv_s, elf_p, elf_s = [], [], [], []
    okv_p, okv_s, win_p, win_s = [], [], [], []
    for layer in range(depth):
        g = norm_g[layer]
        li = layer // 2
        hp3, hs3 = hp.reshape(B, T, D), hs.reshape(DB, S, D)
        if layer % 2 == 0:
            a_p, a_s, kvp, kvs, lfp, lfs = even_layer(
                hp3, hs3, cache_even_kv, cache_even_logf, li, page_table, g[0], w_even_in[li], b_even_f[li],
                diff_lambda[li].astype(F32), diff_subln_g[li].astype(F32), rel_table, layer)
            w_o = w_even_out[li].astype(BF16)
            ekv_p.append(kvp); ekv_s.append(kvs); elf_p.append(lfp); elf_s.append(lfs)
        else:
            a_p, a_s, kvp, kvs, wp_, ws_ = odd_layer(
                hp3, hs3, cache_odd_kv, state_odd_win, li, page_table, g[0], w_odd_in[li], b_odd_gate[li],
                w_cmp_pos[li], b_cmp[li], w_cmp_out[li], rel_table)
            w_o = w_odd_out[li].astype(BF16)
            okv_p.append(kvp); okv_s.append(kvs); win_p.append(wp_); win_s.append(ws_)
        hp = (out_proj_residual if a_p.ndim == 2 else out_proj_residual_t)(a_p, w_o, g[1], hp)
        hs = out_proj_residual(a_s, w_o, g[1], hs)
        w1, w2 = ffn_weights(w_ffn_in[layer], w_ffn_out[layer], 256)
        hp = ffn_residual(hp, g[2], w1, w2, g[3])
        hs = ffn_residual(hs, g[2], w1, w2, g[3])
    return (hp.reshape(B, T, D), hs.reshape(DB, S, D), jnp.stack(ekv_p), jnp.stack(ekv_s), jnp.stack(elf_p), jnp.stack(elf_s),
            jnp.stack(okv_p), jnp.stack(okv_s), jnp.stack(win_p), jnp.stack(win_s))
```

```python
import functools
import math

import numpy as np
import jax
import jax.numpy as jnp
from jax import lax
from jax.experimental import pallas as pl
from jax.experimental.pallas import tpu as pltpu

F32 = jnp.float32
BF16 = jnp.bfloat16
I32 = jnp.int32

HEAD_DIM = 64
LANES = 128
PAGE_SIZE = 128
CMP_BLOCK = 32
CMP_STRIDE = 16
SEL_BLOCK = 64
N_SEL = 16
WINDOW = 512
N_BUCKETS = 32
MAX_DISTANCE = 1024
SCALE = HEAD_DIM ** -0.5
LOG2E = math.log2(math.e)
EPS = 1e-6
NEG = -1e30
FORCE = 1e9
PAD_SCORE = -3.0e38
VMEM_LIMIT = 56 * 1024 * 1024


def _cparams(n_axes):
    return pltpu.CompilerParams(dimension_semantics=("arbitrary",) * n_axes,
                                vmem_limit_bytes=VMEM_LIMIT)


def _dot(a, b):
    return jnp.dot(a, b, preferred_element_type=F32)


def _dot_nt(a, b):
    return lax.dot_general(a, b, (((1,), (1,)), ((), ())), preferred_element_type=F32)


def _split3(x):
    hi = x.astype(BF16)
    r1 = x - hi.astype(F32)
    mid = r1.astype(BF16)
    lo = (r1 - mid.astype(F32)).astype(BF16)
    return hi, mid, lo


def _dot_exact_rhs01(x, m01):
    hi, mid, lo = _split3(x)
    return _dot(hi, m01) + _dot(mid, m01) + _dot(lo, m01)


def _dot_exact_lhs01(m01, x):
    hi, mid, lo = _split3(x)
    return _dot(m01, hi) + _dot(m01, mid) + _dot(m01, lo)


def _lane_tile(x, n):
    return x if n == 1 else jnp.concatenate([x] * n, axis=1)


def _aligned(x, m):
    return x if isinstance(x, int) else pl.multiple_of(x, m)


def _shift_left_one_lane(x):
    n = x.shape[1] // LANES
    if n == 0:
        return pltpu.roll(x, x.shape[1] - 1, 1)
    keep = lax.broadcasted_iota(I32, (1, LANES), 1) < LANES - 1
    rolled = [pltpu.roll(x[:, c * LANES:(c + 1) * LANES], LANES - 1, 1) for c in range(n)]
    out = [jnp.where(keep, rolled[c], rolled[min(c + 1, n - 1)]) for c in range(n)]
    return out[0] if n == 1 else jnp.concatenate(out, axis=1)


def _silu(x):
    return x * (1.0 / (1.0 + jnp.exp(-x)))


def rel_bucket(dist):
    n = jnp.maximum(dist, 0)
    exact = N_BUCKETS // 2
    nf = jnp.maximum(n, 1).astype(F32)
    large = exact + (jnp.log(nf / exact) / math.log(MAX_DISTANCE / exact) * (N_BUCKETS - exact)).astype(I32)
    return jnp.where(n < exact, n, jnp.minimum(large, N_BUCKETS - 1))


def bias_table(table, dist):
    onehot = (rel_bucket(dist)[..., None] == jnp.arange(N_BUCKETS)).astype(F32)
    return jnp.einsum('...k,kh->h...', onehot, table.astype(F32), precision=lax.Precision.HIGHEST)


def _norm_rows(x, g):
    ms = jnp.mean(x * x, axis=-1, keepdims=True)
    return x * lax.rsqrt(ms + EPS) * g


def _proj_kernel(x_ref, g_ref, w_ref, o_ref, h_ref):
    @pl.when(pl.program_id(1) == 0)
    def _():
        h_ref[...] = _norm_rows(x_ref[...], g_ref[...]).astype(BF16)

    o_ref[...] = _dot(h_ref[...], w_ref[...]).astype(o_ref.dtype)


def norm_proj(x, g, w, tn, out_dtype=F32):
    M, D = x.shape
    N = w.shape[1]
    tm = min(M, 1024)
    return pl.pallas_call(
        _proj_kernel,
        grid=(M // tm, N // tn),
        in_specs=[pl.BlockSpec((tm, D), lambda i, j: (i, 0)),
                  pl.BlockSpec((1, D), lambda i, j: (0, 0)),
                  pl.BlockSpec((D, tn), lambda i, j: (0, j))],
        out_specs=pl.BlockSpec((tm, tn), lambda i, j: (i, j)),
        out_shape=jax.ShapeDtypeStruct((M, N), out_dtype),
        scratch_shapes=[pltpu.VMEM((tm, D), BF16)],
        compiler_params=_cparams(2),
        name="norm_proj",
    )(x, g.reshape(1, D), w)


def _proj_t_kernel(x_ref, g_ref, w_ref, o_ref, h_ref):
    @pl.when(pl.program_id(2) == 0)
    def _():
        h_ref[...] = _norm_rows(x_ref[...], g_ref[...]).astype(BF16)

    o_ref[...] = _dot_nt(w_ref[...], h_ref[...])


def norm_proj_t(x, g, w_t, tn):
    B, T, D = x.shape
    N = w_t.shape[0]
    tm = min(T, 1024)
    return pl.pallas_call(
        _proj_t_kernel,
        grid=(B, T // tm, N // tn),
        in_specs=[pl.BlockSpec((None, tm, D), lambda b, i, j: (b, i, 0)),
                  pl.BlockSpec((1, D), lambda b, i, j: (0, 0)),
                  pl.BlockSpec((tn, D), lambda b, i, j: (j, 0))],
        out_specs=pl.BlockSpec((None, tn, tm), lambda b, i, j: (b, j, i)),
        out_shape=jax.ShapeDtypeStruct((B, N, T), F32),
        scratch_shapes=[pltpu.VMEM((tm, D), BF16)],
        compiler_params=_cparams(3),
        name="norm_proj_t",
    )(x, g.reshape(1, D), w_t)


def _out_res_kernel(a_ref, w_ref, g_ref, r_ref, o_ref):
    y = _dot(a_ref[...].astype(BF16), w_ref[...])
    o_ref[...] = r_ref[...] + _norm_rows(y, g_ref[...])


def out_proj_residual(a, w, g, res):
    M, K = a.shape
    D = w.shape[1]
    tm = min(M, 512)
    return pl.pallas_call(
        _out_res_kernel,
        grid=(M // tm,),
        in_specs=[pl.BlockSpec((tm, K), lambda i: (i, 0)),
                  pl.BlockSpec((K, D), lambda i: (0, 0)),
                  pl.BlockSpec((1, D), lambda i: (0, 0)),
                  pl.BlockSpec((tm, D), lambda i: (i, 0))],
        out_specs=pl.BlockSpec((tm, D), lambda i: (i, 0)),
        out_shape=jax.ShapeDtypeStruct((M, D), F32),
        compiler_params=_cparams(1),
        name="out_proj_residual",
    )(a, w, g.reshape(1, D), res)


def _ffn_kernel(x_ref, g2_ref, wgu_ref, wo_ref, g3_ref, o_ref, h_ref, acc_ref):
    f = pl.program_id(1)
    tf = wo_ref.shape[0]

    @pl.when(f == 0)
    def _():
        h_ref[...] = _norm_rows(x_ref[...], g2_ref[...]).astype(BF16)
        acc_ref[...] = jnp.zeros_like(acc_ref)

    gu = _dot(h_ref[...], wgu_ref[...])
    acc_ref[...] += _dot((_silu(gu[:, :tf]) * gu[:, tf:]).astype(BF16), wo_ref[...])

    @pl.when(f == pl.num_programs(1) - 1)
    def _():
        o_ref[...] = x_ref[...] + _norm_rows(acc_ref[...], g3_ref[...])


def ffn_weights(w_in, w_out, tf):
    D = w_in.shape[0]
    Fdim = w_out.shape[0]
    nf = Fdim // tf
    gate = w_in[:, :Fdim].reshape(D, nf, tf)
    up = w_in[:, Fdim:].reshape(D, nf, tf)
    return jnp.concatenate([gate, up], axis=2).transpose(1, 0, 2).astype(BF16), w_out.astype(BF16)


def ffn_residual(x, g2, w_gu, w_out, g3):
    M, D = x.shape
    nf, _, tf2 = w_gu.shape
    tf = tf2 // 2
    tm = min(M, 1024)
    return pl.pallas_call(
        _ffn_kernel,
        grid=(M // tm, nf),
        in_specs=[pl.BlockSpec((tm, D), lambda i, f: (i, 0)),
                  pl.BlockSpec((1, D), lambda i, f: (0, 0)),
                  pl.BlockSpec((None, D, tf2), lambda i, f: (f, 0, 0)),
                  pl.BlockSpec((tf, D), lambda i, f: (f, 0)),
                  pl.BlockSpec((1, D), lambda i, f: (0, 0))],
        out_specs=pl.BlockSpec((tm, D), lambda i, f: (i, 0)),
        out_shape=jax.ShapeDtypeStruct((M, D), F32),
        scratch_shapes=[pltpu.VMEM((tm, D), BF16), pltpu.VMEM((tm, D), F32)],
        compiler_params=_cparams(2),
        name="ffn_residual",
    )(x, g2.reshape(1, D), w_gu, w_out, g3.reshape(1, D))


def _logf_kernel(f_ref, b_ref, tri_ref, lf_ref, cum_ref, carry_ref):
    @pl.when(pl.program_id(1) == 0)
    def _():
        carry_ref[...] = jnp.zeros_like(carry_ref)

    x = f_ref[...] + b_ref[...]
    lf = jnp.minimum(x, 0.0) - jnp.log(1.0 + jnp.exp(-jnp.abs(x)))
    lf_ref[...] = lf
    cum = _dot_exact_lhs01(tri_ref[...], lf) + carry_ref[...]
    cum_ref[...] = cum
    carry_ref[...] = cum[cum.shape[0] - 1:, :]


def logf_cumsum(fq, col_block, b_pad, tq):
    B, T, _ = fq.shape
    tri = jnp.asarray(np.tril(np.ones((tq, tq), np.float32)), BF16)
    return pl.pallas_call(
        _logf_kernel,
        grid=(B, T // tq),
        in_specs=[pl.BlockSpec((None, tq, LANES), lambda b, i: (b, i, col_block)),
                  pl.BlockSpec((1, LANES), lambda b, i: (0, 0)),
                  pl.BlockSpec((tq, tq), lambda b, i: (0, 0))],
        out_specs=[pl.BlockSpec((None, tq, LANES), lambda b, i: (b, i, 0)),
                   pl.BlockSpec((None, tq, LANES), lambda b, i: (b, i, 0))],
        out_shape=[jax.ShapeDtypeStruct((B, T, LANES), F32)] * 2,
        scratch_shapes=[pltpu.VMEM((1, LANES), F32)],
        compiler_params=_cparams(2),
        name="logf_cumsum",
    )(fq, b_pad, tri)


def _lambda_value(lp, lam_init):
    a = jnp.sum(lp[0:1, :] * lp[1:2, :], axis=-1, keepdims=True)
    b = jnp.sum(lp[2:3, :] * lp[3:4, :], axis=-1, keepdims=True)
    return jnp.exp(a) - jnp.exp(b) + lam_init


def _even_attn_kernel(lp_ref, q_ref, k_ref, v_ref, c_ref, tb_ref, sg_ref, o_ref, kb, vb, m_ref, l_ref, acc_ref,
                      *, tq, tk, n_fox, lam_init):
    c = pl.program_id(1)
    qi = pl.program_id(2)

    @pl.when(qi == 0)
    def _():
        kb[...] = k_ref[...].astype(BF16)
        vb[...] = v_ref[...].astype(BF16)

    lo = lax.broadcasted_iota(I32, (1, LANES), 1) < HEAD_DIM
    q = q_ref[...] * SCALE
    qh = (jnp.where(lo, q, 0.0).astype(BF16), jnp.where(lo, 0.0, q).astype(BF16))
    t0 = qi * tq
    n_full = t0 // tk
    sub = tk // tq
    nd = tb_ref.shape[0]

    def tile(kt, bias_fn, first, diag):
        start = _aligned(kt * tk, tk)
        k = kb[:, pl.ds(start, tk)]
        v = vb[:, pl.ds(start, tk)]
        if diag:
            row = t0 + lax.broadcasted_iota(I32, (tq, tk), 0)
            col = start + lax.broadcasted_iota(I32, (tq, tk), 1)
            causal = col <= row
        for h in range(2):
            s = _dot(qh[h], k) + bias_fn(h, kt, start)
            if diag:
                s = jnp.where(causal, s, NEG)
            rmax = jnp.broadcast_to(jnp.max(s, axis=-1, keepdims=True), (tq, LANES))
            if first:
                m_new = rmax
            else:
                m_prev = m_ref[h]
                m_new = jnp.maximum(m_prev, rmax)
                alpha = jnp.exp(m_prev - m_new)
            p = jnp.exp(s - _lane_tile(m_new, tk // LANES))
            rsum = jnp.broadcast_to(jnp.sum(p, axis=-1, keepdims=True), (tq, LANES))
            pv = _dot_nt(p.astype(BF16), v)
            if first:
                l_ref[h] = rsum
                acc_ref[h] = pv
            else:
                l_ref[h] = alpha * l_ref[h] + rsum
                acc_ref[h] = alpha * acc_ref[h] + pv
            m_ref[h] = m_new

    def run(bias_fn):
        @pl.when(n_full == 0)
        def _():
            tile(0, bias_fn, True, True)

        @pl.when(n_full > 0)
        def _():
            tile(0, bias_fn, True, False)

            def body(kt, carry):
                tile(kt, bias_fn, False, False)
                return carry

            lax.fori_loop(1, n_full, body, 0)
            tile(n_full, bias_fn, False, True)

    @pl.when(c < n_fox)
    def _():
        def bias_fn(h, kt, start):
            return -c_ref[h:h + 1, pl.ds(start, tk)]

        run(bias_fn)
        o_ref[...] = jnp.where(lo, acc_ref[0] / l_ref[0], acc_ref[1] / l_ref[1])

    @pl.when(c >= n_fox)
    def _():
        def bias_fn(h, kt, start):
            d0 = qi - kt * sub
            return jnp.concatenate([tb_ref[jnp.clip(d0 - cc, 0, nd - 1)] for cc in range(sub)], axis=1)

        run(bias_fn)
        lam = _lambda_value(lp_ref[...], lam_init)
        o = acc_ref[0] / l_ref[0] - lam * (acc_ref[1] / l_ref[1])
        ms = jnp.mean(o * o, axis=-1, keepdims=True)
        o_ref[...] = o * lax.rsqrt(ms + EPS) * sg_ref[...] * (1.0 - lam_init)


def even_prompt_attention(q_arr, kv_t, cum_pairs, toe, lam_p, subln_g, lam_init, tq, tk):
    B, T, _ = q_arr.shape
    n_fox = 4
    n_tiles = 8
    nd = toe.shape[1]
    kern = functools.partial(_even_attn_kernel, tq=tq, tk=tk, n_fox=n_fox, lam_init=lam_init)
    return pl.pallas_call(
        kern,
        grid=(B, n_tiles, T // tq),
        in_specs=[pl.BlockSpec((4, HEAD_DIM), lambda b, c, i: (0, 0)),
                  pl.BlockSpec((None, tq, LANES), lambda b, c, i: (b, i, c)),
                  pl.BlockSpec((None, LANES, T), lambda b, c, i: (b, c, 0)),
                  pl.BlockSpec((None, LANES, T), lambda b, c, i: (b, n_tiles + c, 0)),
                  pl.BlockSpec((None, None, 2, T), lambda b, c, i: (b, jnp.minimum(c, n_fox - 1), 0, 0)),
                  pl.BlockSpec((None, nd, tq, tq), lambda b, c, i: (jnp.maximum(c - n_fox, 0), 0, 0, 0)),
                  pl.BlockSpec((1, LANES), lambda b, c, i: (0, 0))],
        out_specs=pl.BlockSpec((None, tq, LANES), lambda b, c, i: (b, i, c)),
        out_shape=jax.ShapeDtypeStruct((B, T, n_tiles * LANES), F32),
        scratch_shapes=[pltpu.VMEM((LANES, T), BF16), pltpu.VMEM((LANES, T), BF16),
                        pltpu.VMEM((2, tq, LANES), F32), pltpu.VMEM((2, tq, LANES), F32), pltpu.VMEM((2, tq, LANES), F32)],
        compiler_params=_cparams(3),
        name="even_prompt_attention",
    )(lam_p, q_arr, kv_t, kv_t, cum_pairs, toe, subln_g.reshape(1, LANES))


def toeplitz_dist(tq, nd):
    i = np.arange(tq)[:, None]
    j = np.arange(tq)[None, :]
    return jnp.asarray(np.maximum(np.arange(nd)[:, None, None] * tq + (i - j)[None], 0), I32)


def _pool_weights_t(w_pos_rows):
    reps = LANES // CMP_STRIDE
    w0 = jnp.tile(w_pos_rows[:CMP_STRIDE].T, (1, reps))
    w1 = jnp.tile(w_pos_rows[CMP_STRIDE:].T, (1, reps))
    return w0.astype(F32), w1.astype(F32)


def _group_sum_stack(n_tiles):
    spt = LANES // CMP_STRIDE
    assert n_tiles * spt <= LANES
    m = np.zeros((n_tiles, 2 * LANES, 2 * LANES), np.float32)
    pos = np.arange(LANES)
    for i in range(n_tiles):
        m[i, pos, i * spt + pos // CMP_STRIDE] = 1.0
        m[i, LANES + pos, LANES + i * spt + pos // CMP_STRIDE] = 1.0
    return m


def _pool2(x, w0, w1, gmat2, two_pass):
    y = jnp.concatenate([x * w0, x * w1], axis=1)
    hi = y.astype(BF16)
    out = _dot(hi, gmat2)
    if two_pass:
        out = out + _dot((y - hi.astype(F32)).astype(BF16), gmat2)
    return out


def _cmp_finish_t(p0, p1, b_col, w0t, w1t):
    half = w0t.shape[0]
    z = _silu(p0 + _shift_left_one_lane(p1) + b_col)
    ck = _dot(w0t, z[:half].astype(BF16))
    cv = _dot(w1t, z[half:].astype(BF16))
    return ck, cv


def _cmp_build_kernel(c_ref, w0_ref, w1_ref, g_ref, b_ref, w0t_ref, w1t_ref, ck_ref, cv_ref):
    T = c_ref.shape[1]
    nb = ck_ref.shape[1]
    w0 = w0_ref[...]
    w1 = w1_ref[...]
    acc = None
    for lt in range(T // LANES):
        a = _pool2(c_ref[:, lt * LANES:(lt + 1) * LANES], w0, w1, g_ref[lt], True)
        acc = a if acc is None else acc + a
    ck, cv = _cmp_finish_t(acc[:, 0:nb], acc[:, LANES:LANES + nb], b_ref[...], w0t_ref[...], w1t_ref[...])
    ck_ref[...] = ck
    cv_ref[...] = cv


def cmp_build(cs_t, w0, w1, b_col, w0t, w1t):
    B, _, T = cs_t.shape
    nb = T // CMP_STRIDE
    width = w0.shape[0]
    half = width // 2
    n_tiles = T // LANES
    gmat = jnp.asarray(_group_sum_stack(n_tiles), BF16)
    const = lambda shape: pl.BlockSpec(shape, lambda b: (0,) * len(shape))
    return pl.pallas_call(
        _cmp_build_kernel,
        grid=(B,),
        in_specs=[pl.BlockSpec((None, width, T), lambda b: (b, 0, 0)),
                  const((width, LANES)), const((width, LANES)), const((n_tiles, 2 * LANES, 2 * LANES)), const((width, 1)),
                  const((half, half)), const((half, half))],
        out_specs=[pl.BlockSpec((None, half, nb), lambda b: (b, 0, 0))] * 2,
        out_shape=[jax.ShapeDtypeStruct((B, half, nb), F32)] * 2,
        compiler_params=_cparams(1),
        name="cmp_build",
    )(cs_t, w0, w1, gmat, b_col, w0t, w1t)


def _nsa_kernel(q_ref, ck_ref, cv_ref, sk_ref, sv_ref, wk_ref, wv_ref, gt_ref, bg_ref, cb_ref, tb_ref, cov_ref, ex_ref,
                o_ref, sk2, sv2, wk2, wv2, ck2, cv2, madd, p_ref, m_ref, l_ref, a_ref, acc_ref,
                *, tq, tk, n_blk, n_selb, n_top, win_chunks):
    qi = pl.program_id(2)

    @pl.when(qi == 0)
    def _():
        for src, dst in ((sk_ref, sk2), (sv_ref, sv2), (wk_ref, wk2), (wv_ref, wv2), (ck_ref, ck2), (cv_ref, cv2)):
            x = src[...].astype(BF16)
            dst[0:HEAD_DIM, :] = x
            dst[HEAD_DIM:, :] = x

    T = sk_ref.shape[1]
    nb = ck_ref.shape[1]
    nd = tb_ref.shape[1]
    sub = tk // tq
    t0 = qi * tq
    lane = lax.broadcasted_iota(I32, (1, LANES), 1)
    lo = lane < HEAD_DIM
    tcol = t0 + lax.broadcasted_iota(I32, (tq, 1), 0)
    q = q_ref[...] * SCALE
    parts = []
    for j in range(4):
        tile = q[:, (j // 2) * LANES:(j // 2 + 1) * LANES]
        parts.append(jnp.where(lo if j % 2 == 0 else jnp.logical_not(lo), tile, 0.0))
    qs = jnp.concatenate(parts, axis=0).astype(BF16)
    gates = 1.0 / (1.0 + jnp.exp(-(gt_ref[...] + bg_ref[...])))

    n_idx = lax.broadcasted_iota(I32, (1, nb), 1)
    validc = jnp.logical_and(n_idx * CMP_STRIDE + (CMP_BLOCK - 1) <= tcol, n_idx < n_blk)
    sc = _dot(qs, ck2[...])
    o_c = []
    psum = jnp.zeros((tq, nb), F32)
    for j in range(4):
        s = jnp.where(validc, sc[j * tq:(j + 1) * tq] + cb_ref[j], NEG)
        m = jnp.max(s, axis=-1, keepdims=True)
        e = jnp.where(validc, jnp.exp(s - m), 0.0)
        l = jnp.sum(e, axis=-1, keepdims=True)
        p = e / jnp.where(l > 0.0, l, 1.0)
        psum = psum + p
        o_c.append(_dot_nt(p.astype(BF16), cv2[...]))

    score = _dot_exact_rhs01(psum, cov_ref[...])
    cur = tcol // SEL_BLOCK
    forced = jnp.logical_or(lane == 0, jnp.logical_or(lane == cur, lane == cur - 1))
    score = jnp.where(forced, FORCE, score)
    score = jnp.where(lane * SEL_BLOCK <= tcol, score, NEG)
    score = jnp.where(lane < n_selb, score, PAD_SCORE)
    n_selp = -(-n_selb // 8) * 8
    sc_t = score.T[:n_selp]
    jrow = lax.broadcasted_iota(I32, (n_selp, 1), 0)
    rank = jnp.zeros((n_selp, tq), F32)
    for i in range(n_selb):
        ri = sc_t[i:i + 1, :]
        tie = jnp.where(i < jrow, 1.0, 0.0)
        rank = rank + jnp.where(ri > sc_t, 1.0, jnp.where(ri == sc_t, tie, 0.0))
    sel_t = jnp.where(rank < n_top, 1.0, 0.0)
    if n_selp < LANES:
        sel_t = jnp.concatenate([sel_t, jnp.zeros((LANES - n_selp, tq), F32)], axis=0)
    chosen = _dot(sel_t.T.astype(BF16), ex_ref[...])
    kpos = lax.broadcasted_iota(I32, (1, T), 1)
    madd[...] = jnp.where(jnp.logical_and(chosen > 0.5, kpos <= tcol), 0.0, NEG)

    def softmax_rows(j, s, first):
        rows = slice(j * tq, (j + 1) * tq)
        w = s.shape[1]
        rmax = jnp.broadcast_to(jnp.max(s, axis=-1, keepdims=True), (tq, LANES))
        if first:
            m_new = rmax
        else:
            m_prev = m_ref[rows]
            m_new = jnp.maximum(m_prev, rmax)
            alpha = jnp.exp(m_prev - m_new)
            a_ref[rows] = alpha
        p = jnp.exp(s - _lane_tile(m_new, w // LANES))
        rsum = jnp.broadcast_to(jnp.sum(p, axis=-1, keepdims=True), (tq, LANES))
        l_ref[rows] = rsum if first else alpha * l_ref[rows] + rsum
        m_ref[rows] = m_new
        p_ref[rows, 0:w] = p.astype(BF16)

    def toe_bias(j, d0, n_chunks):
        return jnp.concatenate([tb_ref[j, jnp.clip(d0 - cc, 0, nd - 1)] for cc in range(n_chunks)], axis=1)

    kt_d = qi // sub

    def sel_tile(kt, first):
        start = _aligned(kt * tk, tk)
        mask = madd[:, pl.ds(start, tk)]
        s4 = _dot(qs, sk2[:, pl.ds(start, tk)])
        for j in range(4):
            softmax_rows(j, s4[j * tq:(j + 1) * tq] + toe_bias(j, qi - kt * sub, sub) + mask, first)
        pv = _dot_nt(p_ref[:, 0:tk], sv2[:, pl.ds(start, tk)])
        acc_ref[...] = pv if first else a_ref[...] * acc_ref[...] + pv

    sel_tile(kt_d, True)

    def sel_body(i, carry):
        sel_tile(kt_d - 1 - i, False)
        return carry

    lax.fori_loop(0, kt_d, sel_body, 0)
    o_s = acc_ref[...] / l_ref[...]

    ww = win_chunks * LANES
    w0 = jnp.maximum(qi - (win_chunks - 1), 0)
    wstart = pl.multiple_of(w0 * tq, tq)
    dist = tcol - (wstart + lax.broadcasted_iota(I32, (1, ww), 1))
    wmask = jnp.where(jnp.logical_and(dist >= 0, dist <= WINDOW), 0.0, NEG)
    s4 = _dot(qs, wk2[:, pl.ds(wstart, ww)])
    for j in range(4):
        softmax_rows(j, s4[j * tq:(j + 1) * tq] + toe_bias(j, qi - w0, win_chunks) + wmask, True)
    o_w = _dot_nt(p_ref[:, 0:ww], wv2[:, pl.ds(wstart, ww)]) / l_ref[...]

    outs = []
    for j in range(4):
        rows = slice(j * tq, (j + 1) * tq)
        outs.append(gates[:, 3 * j:3 * j + 1] * o_c[j] + gates[:, 3 * j + 1:3 * j + 2] * o_s[rows]
                    + gates[:, 3 * j + 2:3 * j + 3] * o_w[rows])
    o_ref[:, 0:LANES] = jnp.where(lo, outs[0], outs[1])
    o_ref[:, LANES:2 * LANES] = jnp.where(lo, outs[2], outs[3])


def nsa_prompt_attention(qg_arr, ck_t, cv_t, cs_t, w_t, bg_pad, cbias, toe, cover, expand, tq, tk):
    B, T, _ = qg_arr.shape
    nb = ck_t.shape[2]
    nd = toe.shape[1]
    n_selb = T // SEL_BLOCK
    win_chunks = WINDOW // tq + 1
    ww = win_chunks * LANES
    assert tq == LANES and T >= ww and tk % tq == 0 and T % tk == 0
    kern = functools.partial(_nsa_kernel, tq=tq, tk=tk, n_blk=nb - 1, n_selb=n_selb, n_top=min(N_SEL, n_selb),
                             win_chunks=win_chunks)
    rows64 = lambda blk0: pl.BlockSpec((None, HEAD_DIM, T), lambda g, b, i: (b, blk0 + g, 0))
    wide = max(tk, ww)
    return pl.pallas_call(
        kern,
        grid=(4, B, T // tq),
        in_specs=[pl.BlockSpec((None, tq, 2 * LANES), lambda g, b, i: (b, i, g)),
                  pl.BlockSpec((None, HEAD_DIM, nb), lambda g, b, i: (b, g, 0)),
                  pl.BlockSpec((None, HEAD_DIM, nb), lambda g, b, i: (b, g, 0)),
                  rows64(8), rows64(12), rows64(0), rows64(4),
                  pl.BlockSpec((None, tq, LANES), lambda g, b, i: (b, i, 8 + g)),
                  pl.BlockSpec((None, 1, LANES), lambda g, b, i: (g, 0, 0)),
                  pl.BlockSpec((4, tq, nb), lambda g, b, i: (g, i, 0)),
                  pl.BlockSpec((4, nd, tq, tq), lambda g, b, i: (g, 0, 0, 0)),
                  pl.BlockSpec((nb, LANES), lambda g, b, i: (0, 0)),
                  pl.BlockSpec((LANES, T), lambda g, b, i: (0, 0))],
        out_specs=pl.BlockSpec((None, tq, 2 * LANES), lambda g, b, i: (b, i, g)),
        out_shape=jax.ShapeDtypeStruct((B, T, 8 * LANES), F32),
        scratch_shapes=[pltpu.VMEM((LANES, T), BF16)] * 4 + [pltpu.VMEM((LANES, nb), BF16)] * 2
                       + [pltpu.VMEM((tq, T), F32), pltpu.VMEM((4 * tq, wide), BF16)]
                       + [pltpu.VMEM((4 * tq, LANES), F32)] * 4,
        compiler_params=_cparams(3),
        name="nsa_prompt_attention",
    )(qg_arr, ck_t, cv_t, cs_t, cs_t, w_t, w_t, qg_arr, bg_pad, cbias, toe, cover, expand)


def _nsa_t_kernel(q_ref, ckt_ref, cvt_ref, sk_ref, svt_ref, wk_ref, wvt_ref, gt_ref, bg_ref, cbt_ref, tbt_ref, covt_ref, ext_ref,
                  o_ref, svb, wvb, ck2, cvb, *, tq, tk, n_blk, n_selb, n_top, win_chunks, n_chains):
    qi = pl.program_id(2)

    @pl.when(qi == 0)
    def _():
        ones = jnp.ones((16, svt_ref.shape[1]), BF16)
        svb[0:HEAD_DIM, :] = svt_ref[...].astype(BF16)
        svb[HEAD_DIM:, :] = ones
        wvb[0:HEAD_DIM, :] = wvt_ref[...].astype(BF16)
        wvb[HEAD_DIM:, :] = ones
        cvb[...] = cvt_ref[...].astype(BF16)
        ckt = ckt_ref[...]
        ck2[...] = jnp.concatenate([ckt, ckt], axis=0).T.astype(BF16)

    T = svt_ref.shape[1]
    nb = ckt_ref.shape[1]
    nd = tbt_ref.shape[1]
    sub = tk // tq
    t0 = qi * tq
    lane = lax.broadcasted_iota(I32, (1, LANES), 1)
    lo = lane < HEAD_DIM
    tpos = t0 + lax.broadcasted_iota(I32, (1, tq), 1)
    q = q_ref[...]
    zero = jnp.zeros((), BF16)
    parts = []
    for j in range(4):
        tile = q[:, (j // 2) * LANES:(j // 2 + 1) * LANES]
        parts.append(jnp.where(lo if j % 2 == 0 else jnp.logical_not(lo), tile, zero))
    qs = jnp.concatenate(parts, axis=0)
    gt = 1.0 / (1.0 + jnp.exp(-(gt_ref[...] + bg_ref[...])))

    def gate(r):
        return jnp.concatenate([gt[3 * j + r:3 * j + r + 1, :] for j in range(4)], axis=1)

    n_col = lax.broadcasted_iota(I32, (nb, 1), 0)
    validc = jnp.logical_and(n_col * CMP_STRIDE + (CMP_BLOCK - 1) <= tpos, n_col < n_blk)
    validc4 = _lane_tile(validc, 4)
    sc = _dot_nt(ck2[...], qs) + jnp.concatenate([cbt_ref[j] for j in range(4)], axis=1)
    sc = jnp.where(validc4, sc, NEG)
    m = jnp.max(sc, axis=0, keepdims=True)
    e = jnp.where(validc4, jnp.exp2(sc - m), 0.0)
    l = jnp.sum(e, axis=0, keepdims=True)
    p = e / jnp.where(l > 0.0, l, 1.0)
    o_c = _dot(cvb[...], p.astype(BF16))
    psum = p[:, 0:tq] + p[:, tq:2 * tq] + p[:, 2 * tq:3 * tq] + p[:, 3 * tq:4 * tq]

    score = _dot_exact_lhs01(covt_ref[...], psum)
    jrow = lax.broadcasted_iota(I32, (LANES, 1), 0)
    cur = tpos // SEL_BLOCK
    forced = jnp.logical_or(jrow == 0, jnp.logical_or(jrow == cur, jrow == cur - 1))
    score = jnp.where(forced, FORCE, score)
    score = jnp.where(jrow * SEL_BLOCK <= tpos, score, NEG)
    n_selp = -(-n_selb // 8) * 8
    sc_t = score[:n_selp]
    jr = jrow[:n_selp]
    rank = jnp.zeros((n_selp, tq), F32)
    for i in range(n_selb):
        ri = sc_t[i:i + 1, :]
        tie = jnp.where(i < jr, 1.0, 0.0)
        rank = rank + jnp.where(ri > sc_t, 1.0, jnp.where(ri == sc_t, tie, 0.0))
    sel_t = jnp.where(jnp.logical_and(rank < n_top, jr < n_selb), 1.0, 0.0)
    if n_selp < LANES:
        sel_t = jnp.concatenate([sel_t, jnp.zeros((LANES - n_selp, tq), F32)], axis=0)
    sel_b = sel_t.astype(BF16)

    hpc = 4 // n_chains
    qs_c = [qs[c * hpc * tq:(c + 1) * hpc * tq] for c in range(n_chains)]

    def toe_bias(c, d0, n_chunks):
        cols = [jnp.concatenate([tbt_ref[j, jnp.clip(d0 - cc, 0, nd - 1)] for cc in range(n_chunks)], axis=0)
                for j in range(c * hpc, (c + 1) * hpc)]
        return cols[0] if hpc == 1 else jnp.concatenate(cols, axis=1)

    def update(s, vt, state):
        s_max = jnp.max(s, axis=0, keepdims=True)
        if state is None:
            m_new = s_max
        else:
            m_prev, acc_prev = state
            m_new = jnp.maximum(m_prev, s_max)
        p = jnp.exp2((s - m_new).astype(BF16))
        pv = _dot(vt, p)
        if state is None:
            return m_new, pv
        return m_new, jnp.exp2(m_prev - m_new) * acc_prev + pv

    def attend(k_ref, v_ref, start, width, d0, mask, states):
        k = k_ref[pl.ds(start, width), :]
        v = v_ref[:, pl.ds(start, width)]
        mask_c = _lane_tile(mask, hpc)
        cs = range(n_chains)
        raws = [_dot_nt(k, qs_c[c]) for c in cs]
        ss = [raws[c] + toe_bias(c, d0, width // tq) + mask_c for c in cs]
        ms = [jnp.max(ss[c], axis=0, keepdims=True) for c in cs]
        if states is not None:
            ms = [jnp.maximum(states[c][0], ms[c]) for c in cs]
        ps = [jnp.exp2((ss[c] - ms[c]).astype(BF16)) for c in cs]
        pvs = [_dot(v, ps[c]) for c in cs]
        if states is None:
            return tuple((ms[c], pvs[c]) for c in cs)
        return tuple((ms[c], jnp.exp2(states[c][0] - ms[c]) * states[c][1] + pvs[c]) for c in cs)

    kt_d = qi // sub

    def sel_tile(kt, states):
        start = _aligned(kt * tk, tk)
        chosen = _dot(ext_ref[pl.ds(start, tk), :], sel_b) > 0.5
        if states is None:
            chosen = jnp.logical_and(chosen, start + lax.broadcasted_iota(I32, (tk, 1), 0) <= tpos)
        return attend(sk_ref, svb, start, tk, qi - kt * sub, jnp.where(chosen, 0.0, NEG), states)

    states = lax.fori_loop(0, kt_d, lambda i, st: sel_tile(kt_d - 1 - i, st), sel_tile(kt_d, None))
    o_s = jnp.concatenate([acc[0:HEAD_DIM] / acc[HEAD_DIM:HEAD_DIM + 1] for _, acc in states], axis=1)

    ww = win_chunks * tq
    w0 = jnp.maximum(qi - (win_chunks - 1), 0)
    wstart = pl.multiple_of(w0 * tq, tq)
    dist = tpos - (wstart + lax.broadcasted_iota(I32, (ww, 1), 0))
    wmask = jnp.where(jnp.logical_and(dist >= 0, dist <= WINDOW), 0.0, NEG)
    states = attend(wk_ref, wvb, wstart, ww, qi - w0, wmask, None)
    o_w = jnp.concatenate([acc[0:HEAD_DIM] / acc[HEAD_DIM:HEAD_DIM + 1] for _, acc in states], axis=1)

    o = gate(0) * o_c + gate(1) * o_s + gate(2) * o_w
    for j in range(4):
        o_ref[j * HEAD_DIM:(j + 1) * HEAD_DIM, :] = o[:, j * tq:(j + 1) * tq]


def nsa_prompt_attention_t(qk_arr, ck_t, cv_t, cs_t, wg_t, bg_col, cbias_t, toe_t, cover_t, expand_t, tq, tk):
    B, T, _ = qk_arr.shape
    nb = ck_t.shape[2]
    nd = toe_t.shape[1]
    n_selb = T // SEL_BLOCK
    win_chunks = WINDOW // tq + 1
    assert tq == LANES and T >= win_chunks * tq and tk % tq == 0 and T % tk == 0 and nb <= LANES
    kern = functools.partial(_nsa_t_kernel, tq=tq, tk=tk, n_blk=nb - 1, n_selb=n_selb, n_top=min(N_SEL, n_selb),
                             win_chunks=win_chunks, n_chains=1)
    return pl.pallas_call(
        kern,
        grid=(4, B, T // tq),
        in_specs=[pl.BlockSpec((None, tq, 2 * LANES), lambda g, b, i: (b, i, g)),
                  pl.BlockSpec((None, HEAD_DIM, nb), lambda g, b, i: (b, g, 0)),
                  pl.BlockSpec((None, HEAD_DIM, nb), lambda g, b, i: (b, g, 0)),
                  pl.BlockSpec((None, T, LANES), lambda g, b, i: (b, 0, 8 + g)),
                  pl.BlockSpec((None, HEAD_DIM, T), lambda g, b, i: (b, 12 + g, 0)),
                  pl.BlockSpec((None, T, LANES), lambda g, b, i: (b, 0, 12 + g)),
                  pl.BlockSpec((None, HEAD_DIM, T), lambda g, b, i: (b, 4 + g, 0)),
                  pl.BlockSpec((None, 16, tq), lambda g, b, i: (b, 32 + g, i)),
                  pl.BlockSpec((None, 16, 1), lambda g, b, i: (g, 0, 0)),
                  pl.BlockSpec((4, nb, tq), lambda g, b, i: (g, 0, i)),
                  pl.BlockSpec((4, nd, tq, tq), lambda g, b, i: (g, 0, 0, 0)),
                  pl.BlockSpec((LANES, nb), lambda g, b, i: (0, 0)),
                  pl.BlockSpec((T, LANES), lambda g, b, i: (0, 0))],
        out_specs=pl.BlockSpec((None, 4 * HEAD_DIM, tq), lambda g, b, i: (b, g, i)),
        out_shape=jax.ShapeDtypeStruct((B, 16 * HEAD_DIM, T), F32),
        scratch_shapes=[pltpu.VMEM((HEAD_DIM + 16, T), BF16), pltpu.VMEM((HEAD_DIM + 16, T), BF16),
                        pltpu.VMEM((nb, LANES), BF16), pltpu.VMEM((HEAD_DIM, nb), BF16)],
        compiler_params=_cparams(3),
        name="nsa_prompt_attention_t",
    )(qk_arr, ck_t, cv_t, qk_arr, cs_t, qk_arr, wg_t, wg_t, bg_col, cbias_t, toe_t, cover_t, expand_t)


def _out_res_t_kernel(a_ref, w_ref, g_ref, r_ref, o_ref):
    y = _dot(a_ref[...].T.astype(BF16), w_ref[...])
    o_ref[...] = r_ref[...] + _norm_rows(y, g_ref[...])


def out_proj_residual_t(a_t, w, g, res):
    B, K, T = a_t.shape
    D = w.shape[1]
    tm = min(T, 512)
    nt = T // tm
    return pl.pallas_call(
        _out_res_t_kernel,
        grid=(B, nt),
        in_specs=[pl.BlockSpec((None, K, tm), lambda b, i: (b, 0, i)),
                  pl.BlockSpec((K, D), lambda b, i: (0, 0)),
                  pl.BlockSpec((1, D), lambda b, i: (0, 0)),
                  pl.BlockSpec((tm, D), lambda b, i: (b * nt + i, 0))],
        out_specs=pl.BlockSpec((tm, D), lambda b, i: (b * nt + i, 0)),
        out_shape=jax.ShapeDtypeStruct((B * T, D), F32),
        compiler_params=_cparams(2),
        name="out_proj_residual_t",
    )(a_t, w, g.reshape(1, D), res)


def cover_matrix_np(n_blk, n_selb, rows, cols):
    i = np.arange(rows)[:, None]
    j = np.arange(cols)[None, :]
    m = (i * CMP_STRIDE < (j + 1) * SEL_BLOCK) & (i * CMP_STRIDE + CMP_BLOCK > j * SEL_BLOCK) & (i < n_blk) & (j < n_selb)
    return m.astype(np.float32)


def _even_dec_kernel(pt_ref, *refs, pps, n_steps, lam_init):
    kv_refs = refs[0:pps]
    lf_refs = refs[pps:2 * pps]
    (qt_ref, cn_ref, tbt_ref, ut_ref, kn_ref, vn_ref, bo_ref, lp_ref, sg_ref,
     o_ref, m_ref, l_ref, acc_ref, carry_ref) = refs[2 * pps:]
    s = pl.program_id(1)
    n_maps, width = qt_ref.shape
    n_fox = lf_refs[0].shape[0]

    @pl.when(s == 0)
    def _():
        m_ref[...] = jnp.full_like(m_ref, NEG)
        l_ref[...] = jnp.zeros_like(l_ref)
        acc_ref[...] = jnp.zeros_like(acc_ref)
        carry_ref[...] = jnp.zeros_like(carry_ref)

    qt = qt_ref[...]
    is_fox = lax.broadcasted_iota(I32, (n_maps, 1), 0) < n_fox
    pad = jnp.zeros((n_maps - n_fox, PAGE_SIZE), F32)
    lfts = jnp.concatenate([jnp.concatenate([lf_refs[i][...], pad], axis=0) for i in range(pps)], axis=0)
    within = _dot_exact_rhs01(lfts, ut_ref[...])
    totals = jnp.sum(lfts, axis=-1, keepdims=True)
    run = carry_ref[...]
    later = [None] * pps
    for i in reversed(range(pps)):
        later[i] = run
        run = run + totals[i * n_maps:(i + 1) * n_maps]
    carry_ref[...] = run
    cn = cn_ref[...]
    sts = []
    for i in range(pps):
        kpg = kv_refs[i][0].reshape(width, PAGE_SIZE).astype(BF16)
        suffix = within[i * n_maps:(i + 1) * n_maps] + later[i] + cn
        sts.append(_dot(qt, kpg) + jnp.where(is_fox, suffix, tbt_ref[i]))
    st = jnp.concatenate(sts, axis=1)
    m = m_ref[...]
    m_new = jnp.maximum(m, jnp.max(st, axis=-1, keepdims=True))
    alpha = jnp.exp(m - m_new)
    p = jnp.exp(st - m_new)
    l_ref[...] = alpha * l_ref[...] + jnp.sum(p, axis=-1, keepdims=True)
    pb = p.astype(BF16)
    pv = None
    for i in range(pps):
        vpg = kv_refs[i][1].reshape(width, PAGE_SIZE).astype(BF16)
        d = _dot_nt(pb[:, i * PAGE_SIZE:(i + 1) * PAGE_SIZE], vpg)
        pv = d if pv is None else pv + d
    acc_ref[...] = alpha * acc_ref[...] + pv
    m_ref[...] = m_new

    @pl.when(s == n_steps - 1)
    def _():
        kn = kn_ref[...].astype(BF16).astype(F32)
        vn = vn_ref[...].astype(BF16).astype(F32)
        s_own = jnp.sum(qt.astype(F32) * kn, axis=-1, keepdims=True) + bo_ref[...]
        m = m_ref[...]
        m_all = jnp.maximum(m, s_own)
        a = jnp.exp(m - m_all)
        e_own = jnp.exp(s_own - m_all)
        l_all = a * l_ref[...] + e_own
        o = (a * acc_ref[...] + e_own * vn) / l_all
        r = lax.broadcasted_iota(I32, (n_maps, width), 0)
        cidx = lax.broadcasted_iota(I32, (n_maps, width), 1)
        d_fox = n_fox * HEAD_DIM
        fox_sel = jnp.logical_and(cidx < d_fox, r == cidx // HEAD_DIM)
        dh = (cidx - d_fox) // LANES
        d1_sel = jnp.logical_and(cidx >= d_fox, r == n_fox + 2 * dh)
        d2_sel = jnp.logical_and(cidx >= d_fox, r == n_fox + 2 * dh + 1)
        o_f = jnp.sum(jnp.where(fox_sel, o, 0.0), axis=0, keepdims=True)
        a1 = jnp.sum(jnp.where(d1_sel, o, 0.0), axis=0, keepdims=True)
        a2 = jnp.sum(jnp.where(d2_sel, o, 0.0), axis=0, keepdims=True)
        lam = _lambda_value(lp_ref[...], lam_init)
        o_d = a1 - lam * a2
        for tl in range(width // LANES):
            sl = slice(tl * LANES, (tl + 1) * LANES)
            if tl * LANES < d_fox:
                o_ref[:, sl] = o_f[:, sl]
            else:
                x = o_d[:, sl]
                ms = jnp.mean(x * x, axis=-1, keepdims=True)
                o_ref[:, sl] = x * lax.rsqrt(ms + EPS) * sg_ref[...] * (1.0 - lam_init)


def even_decode(pool_t, lf_t, li, pt_flat, n_pages, qt, cn, tbt, kn, vn, bo, lam_p, subln_g, lam_init, pps):
    DB, n_maps, width = qt.shape
    n_fox = lf_t.shape[2]
    n_steps = n_pages // pps
    ut = jnp.asarray(np.tril(np.ones((PAGE_SIZE, PAGE_SIZE), np.float32), -1), BF16)

    def page(i):
        return lambda b, s, pt: pt[b * n_pages + (n_steps - 1 - s) * pps + i]

    kv_block = (None, None, 2, n_maps, HEAD_DIM, PAGE_SIZE)
    kv_specs = [pl.BlockSpec(kv_block, lambda b, s, pt, f=page(i): (li, f(b, s, pt), 0, 0, 0, 0)) for i in range(pps)]
    lf_specs = [pl.BlockSpec((None, None, n_fox, PAGE_SIZE), lambda b, s, pt, f=page(i): (li, f(b, s, pt), 0, 0)) for i in range(pps)]
    per_b = lambda shape: pl.BlockSpec((None,) + shape, lambda b, s, pt: (b,) + (0,) * len(shape))
    const = lambda shape: pl.BlockSpec(shape, lambda b, s, pt: (0,) * len(shape))
    grid_spec = pltpu.PrefetchScalarGridSpec(
        num_scalar_prefetch=1,
        grid=(DB, n_steps),
        in_specs=kv_specs + lf_specs + [
            per_b((n_maps, width)), per_b((n_maps, 1)),
            pl.BlockSpec((pps, n_maps, PAGE_SIZE), lambda b, s, pt: (n_steps - 1 - s, 0, 0)),
            const((PAGE_SIZE, PAGE_SIZE)),
            per_b((1, width)), per_b((1, width)),
            const((n_maps, 1)), const((4, HEAD_DIM)), const((1, LANES))],
        out_specs=per_b((1, width)),
        scratch_shapes=[pltpu.VMEM((n_maps, 1), F32), pltpu.VMEM((n_maps, 1), F32), pltpu.VMEM((n_maps, width), F32),
                        pltpu.VMEM((n_maps, 1), F32)])
    kern = functools.partial(_even_dec_kernel, pps=pps, n_steps=n_steps, lam_init=lam_init)
    return pl.pallas_call(
        kern, grid_spec=grid_spec,
        out_shape=jax.ShapeDtypeStruct((DB, 1, width), F32),
        compiler_params=_cparams(2),
        name="even_decode",
    )(pt_flat, *([pool_t] * pps), *([lf_t] * pps), qt, cn, tbt, ut, kn, vn, bo, lam_p, subln_g.reshape(1, LANES))


def _odd_cmp_kernel(pt_ref, *refs, pps, n_steps, n_selb, n_top, t_pos):
    r_refs = refs[0:pps]
    (w0_ref, w1_ref, g_ref, b_ref, w0t_ref, w1t_ref, qz_ref, cb_ref, cov_ref, oc_ref, idx_ref, p0, p1) = refs[pps:]
    s = pl.program_id(1)
    width = w0_ref.shape[0]
    w0 = w0_ref[...]
    w1 = w1_ref[...]
    acc = None
    for i in range(pps):
        x = r_refs[i][...].reshape(width, PAGE_SIZE)
        a = _pool2(x, w0, w1, g_ref[i], False)
        acc = a if acc is None else acc + a
    col = pl.multiple_of(s * LANES, LANES)
    p0[:, pl.ds(col, LANES)] = acc[:, 0:LANES]
    p1[:, pl.ds(col, LANES)] = acc[:, LANES:]

    @pl.when(s == n_steps - 1)
    def _():
        nb = p0.shape[1]
        ck, cv = _cmp_finish_t(p0[...], p1[...], b_ref[...], w0t_ref[...], w1t_ref[...])
        st = _dot(qz_ref[...], ck.astype(BF16)) + cb_ref[...]
        n_idx = lax.broadcasted_iota(I32, (1, nb), 1)
        valid = n_idx * CMP_STRIDE + (CMP_BLOCK - 1) <= t_pos
        sm = jnp.where(valid, st, NEG)
        m = jnp.max(sm, axis=-1, keepdims=True)
        e = jnp.where(valid, jnp.exp(sm - m), 0.0)
        l = jnp.sum(e, axis=-1, keepdims=True)
        p = e / jnp.where(l > 0.0, l, 1.0)
        oc_ref[...] = _dot_nt(p.astype(BF16), cv.astype(BF16))
        gqa = p.shape[0] // 4
        rows = [jnp.sum(p[g * gqa:(g + 1) * gqa], axis=0, keepdims=True) for g in range(4)]
        psum = jnp.concatenate(rows + rows, axis=0)
        score = _dot_exact_rhs01(psum, cov_ref[...])
        nsp = score.shape[1]
        jb = lax.broadcasted_iota(I32, (1, nsp), 1)
        cur = t_pos // SEL_BLOCK
        forced = jnp.logical_or(jb == 0, jnp.logical_or(jb == cur, jb == cur - 1))
        score = jnp.where(forced, FORCE, score)
        score = jnp.where(jb * SEL_BLOCK <= t_pos, score, NEG)
        score = jnp.where(jb < n_selb, score, PAD_SCORE)
        jf = jb.astype(F32)
        slot = lax.broadcasted_iota(I32, (1, LANES), 1)
        picks = jnp.zeros((8, LANES), F32)
        for it in range(n_top):
            mx = jnp.max(score, axis=-1, keepdims=True)
            ix = jnp.min(jnp.where(score == mx, jf, 1e9), axis=-1, keepdims=True)
            picks = jnp.where(slot == it, ix, picks)
            score = jnp.where(jf == ix, PAD_SCORE, score)
        idx_ref[...] = picks.astype(I32)


def odd_decode_cmp(pool_t, li, pt_flat, n_pages, w0, w1, b_col, w0t, w1t, qz16, cbias, cover, n_selb, t_pos):
    DB = qz16.shape[0]
    width = w0.shape[0]
    half = width // 2
    pps = LANES // (PAGE_SIZE // CMP_STRIDE)
    assert n_pages % pps == 0
    n_steps = n_pages // pps
    nb = n_pages * PAGE_SIZE // CMP_STRIDE
    nsp = cover.shape[1]
    spp = PAGE_SIZE // CMP_STRIDE
    gstack = jnp.asarray(_group_sum_stack(pps), BF16)
    r_specs = [pl.BlockSpec((None, None, 2, 4, HEAD_DIM, PAGE_SIZE),
                            lambda b, s, pt, i=i: (li, pt[b * n_pages + s * pps + i], 0, 0, 0, 0)) for i in range(pps)]
    per_b = lambda shape: pl.BlockSpec((None,) + shape, lambda b, s, pt: (b,) + (0,) * len(shape))
    const = lambda shape: pl.BlockSpec(shape, lambda b, s, pt: (0,) * len(shape))
    grid_spec = pltpu.PrefetchScalarGridSpec(
        num_scalar_prefetch=1,
        grid=(DB, n_steps),
        in_specs=r_specs + [const((width, LANES)), const((width, LANES)), const((pps, 2 * LANES, 2 * LANES)), const((width, 1)),
                            const((half, half)), const((half, half)),
                            per_b((16, half)), const((16, nb)), const((nb, nsp))],
        out_specs=[per_b((16, half)), per_b((8, LANES))],
        scratch_shapes=[pltpu.VMEM((width, nb), F32), pltpu.VMEM((width, nb), F32)])
    kern = functools.partial(_odd_cmp_kernel, pps=pps, n_steps=n_steps, n_selb=n_selb, n_top=min(N_SEL, n_selb), t_pos=t_pos)
    return pl.pallas_call(
        kern, grid_spec=grid_spec,
        out_shape=[jax.ShapeDtypeStruct((DB, 16, half), F32), jax.ShapeDtypeStruct((DB, 8, LANES), I32)],
        compiler_params=_cparams(2),
        name="odd_decode_cmp",
    )(pt_flat, *([pool_t] * pps), w0, w1, gstack, b_col, w0t, w1t, qz16, cbias, cover)


def _odd_sel_kernel(idx_ref, pt_ref, *refs, n_top, n_past_blk, bpp, t_pos):
    kv_refs = refs[0:n_top]
    (q_ref, sn_ref, tb_ref, wkv_ref, wn_ref, tw_ref, tw0_ref, oc_ref, gt_ref, o_ref) = refs[n_top:]
    b = pl.program_id(0)
    g = pl.program_id(1)
    q = q_ref[...]
    qf = q.astype(F32)
    lane = lax.broadcasted_iota(I32, (1, PAGE_SIZE), 1)
    bias0 = tw0_ref[:, 0:1]

    logits, vals = [], []
    n_new = jnp.zeros((), I32)
    for k in range(n_top):
        blk = idx_ref[(b * 4 + g) * n_top + k]
        is_new = blk >= n_past_blk
        n_new = n_new + is_new.astype(I32)
        page = jnp.minimum(blk // bpp, n_past_blk // bpp - 1)
        bias = tb_ref[pl.ds(pl.multiple_of((page * 4 + g) * 8, 8), 8), :]
        s = _dot(q, kv_refs[k][0].astype(BF16)) + bias
        pos = page * PAGE_SIZE + lane
        valid = jnp.logical_and(jnp.logical_and(pos // SEL_BLOCK == blk, pos <= t_pos), jnp.logical_not(is_new))
        logits.append(jnp.where(valid, s, NEG))
        vals.append(kv_refs[k][1].astype(BF16))
    has_new = n_new > 0
    kn = sn_ref[0:1, :].astype(BF16).astype(F32)
    vn = sn_ref[1:2, :].astype(BF16).astype(F32)
    s_new = jnp.where(has_new, jnp.sum(qf * kn, axis=-1, keepdims=True) + bias0, NEG)
    m = s_new
    for s in logits:
        m = jnp.maximum(m, s.max(axis=-1, keepdims=True))
    p_new = jnp.where(has_new, jnp.exp(s_new - m), 0.0)
    l = p_new
    acc = p_new * vn
    for s, vt in zip(logits, vals):
        p = jnp.where(s > 0.5 * NEG, jnp.exp(s - m), 0.0)
        l = l + jnp.sum(p, axis=-1, keepdims=True)
        acc = acc + _dot_nt(p.astype(BF16), vt)
    o_s = acc / jnp.where(l > 0.0, l, 1.0)

    sw = _dot(q, wkv_ref[0].astype(BF16)) + tw_ref[...]
    kwn = wn_ref[0:1, :].astype(BF16).astype(F32)
    vwn = wn_ref[1:2, :].astype(BF16).astype(F32)
    sw_new = jnp.sum(qf * kwn, axis=-1, keepdims=True) + bias0
    mw = jnp.maximum(jnp.max(sw, axis=-1, keepdims=True), sw_new)
    pw = jnp.exp(sw - mw)
    pw_new = jnp.exp(sw_new - mw)
    lw = jnp.sum(pw, axis=-1, keepdims=True) + pw_new
    o_w = (_dot_nt(pw.astype(BF16), wkv_ref[1].astype(BF16)) + pw_new * vwn) / lw

    gates = 1.0 / (1.0 + jnp.exp(-gt_ref[...]))
    o_ref[...] = gates[:, 0:1] * oc_ref[...] + gates[:, 1:2] * o_s + gates[:, 2:3] * o_w


def odd_decode_sel(pool_t, li, idx_flat, pt_flat, n_pages, win_t, qg, s_new, tbp, w_new, tw, tw0, oc_g, graw, n_top, t_pos):
    DB = qg.shape[0]
    bpp = PAGE_SIZE // SEL_BLOCK
    wb = win_t.shape[-1]
    n_past_blk = n_pages * bpp

    def blk_spec(k):
        def imap(b, g, idx, pt):
            blk = idx[(b * 4 + g) * n_top + k]
            page = pt[b * n_pages + jnp.minimum(blk // bpp, n_pages - 1)]
            return (li, page, 1, g, 0, 0)
        return pl.BlockSpec((None, None, 2, None, HEAD_DIM, PAGE_SIZE), imap)

    per_bg = lambda shape: pl.BlockSpec((None, None) + shape, lambda b, g, idx, pt: (b, g) + (0,) * len(shape))
    grid_spec = pltpu.PrefetchScalarGridSpec(
        num_scalar_prefetch=2,
        grid=(DB, 4),
        in_specs=[blk_spec(k) for k in range(n_top)] + [
            per_bg((8, HEAD_DIM)), per_bg((2, HEAD_DIM)),
            pl.BlockSpec(tbp.shape, lambda b, g, idx, pt: (0, 0)),
            pl.BlockSpec((None, None, 2, None, HEAD_DIM, wb), lambda b, g, idx, pt: (li, b, 0, g, 0, 0)),
            per_bg((2, HEAD_DIM)),
            pl.BlockSpec((8, wb), lambda b, g, idx, pt: (g, 0)),
            pl.BlockSpec((8, LANES), lambda b, g, idx, pt: (g, 0)),
            per_bg((8, HEAD_DIM)), per_bg((8, LANES))],
        out_specs=per_bg((8, HEAD_DIM)))
    kern = functools.partial(_odd_sel_kernel, n_top=n_top, n_past_blk=n_past_blk, bpp=bpp, t_pos=t_pos)
    return pl.pallas_call(
        kern, grid_spec=grid_spec,
        out_shape=jax.ShapeDtypeStruct((DB, 4, 8, HEAD_DIM), F32),
        compiler_params=_cparams(2),
        name="odd_decode_sel",
    )(idx_flat, pt_flat, *([pool_t] * n_top), qg, s_new, tbp, win_t, w_new, tw, tw0, oc_g, graw)


def _pad_cols(w, n):
    return jnp.pad(w, ((0, 0), (0, n - w.shape[1])))


def _pad_rows8(x4):
    pad = [(0, 0)] * x4.ndim
    pad[-2] = (0, 4)
    return jnp.pad(x4, pad)


def even_layer(hp, hs, pool_kv, pool_lf, li, page_table, g0, w_in, b_f, lam_p, subln_g, rel_table, layer):
    B, T, D = hp.shape
    DB, S, _ = hs.shape
    assert S == 1
    n_fox, d_fox = 8, 512
    lam_init = 0.8 - 0.6 * math.exp(-0.3 * layer)
    cuts = np.cumsum((d_fox,) * 6)
    q_f, k_f, v_f, q_d, k_d, v_d, f_w = jnp.split(w_in, [int(c) for c in cuts], axis=1)
    w_q = jnp.concatenate([q_f, q_d, _pad_cols(f_w, LANES)], axis=1).astype(BF16)
    w_kv = jnp.concatenate([k_f, k_d, v_f, v_d], axis=1).astype(BF16)
    bf_pad = jnp.pad(b_f.astype(F32), (0, LANES - n_fox)).reshape(1, LANES)
    table4 = rel_table[:, :4]

    q_arr = norm_proj(hp.reshape(B * T, D), g0, w_q, 384).reshape(B, T, -1)
    kv_t = norm_proj_t(hp, g0, w_kv.T, 512)
    tq = min(T, 256)
    tk = min(T, 512)
    lf_pad, cum_pad = logf_cumsum(q_arr, 8, bf_pad, tq)
    cum_pairs = cum_pad[:, :, :n_fox].transpose(0, 2, 1).reshape(B, 4, 2, T)
    toe = bias_table(table4, toeplitz_dist(tq, T // tq))
    attn_p = even_prompt_attention(q_arr, kv_t, cum_pairs, toe, lam_p, subln_g, lam_init, tq, tk)
    ekv_p = kv_t.reshape(B, 2, 16, HEAD_DIM, T).transpose(0, 4, 1, 2, 3)
    elf_p = lf_pad[:, :, :n_fox]

    n_pages = page_table.shape[1]
    past = n_pages * PAGE_SIZE
    xs = hs.reshape(DB, D)
    qs_arr = norm_proj(xs, g0, w_q, 384)
    kvs_arr = norm_proj(xs, g0, w_kv, 512)
    lfs_pad, _ = logf_cumsum(qs_arr.reshape(1, DB, -1), 8, bf_pad, DB)
    lf_new = lfs_pad[0, :, :n_fox]
    seg = np.zeros((16, 1024), np.float32)
    for m_ in range(16):
        seg[m_, m_ * HEAD_DIM:(m_ + 1) * HEAD_DIM] = 1.0
    qt = (qs_arr[:, None, :1024] * SCALE * seg[None]).astype(BF16)
    cn = jnp.pad(lf_new, ((0, 0), (0, 8)))[:, :, None]
    pos = jnp.arange(past).reshape(n_pages, PAGE_SIZE)
    rb = jnp.repeat(bias_table(table4, past - pos), 2, axis=0)
    tbt = jnp.concatenate([jnp.zeros((n_pages, 8, PAGE_SIZE), F32), rb.transpose(1, 0, 2)], axis=1)
    bo = jnp.concatenate([jnp.zeros((8,), F32), jnp.repeat(rel_table[0, :4], 2)]).reshape(16, 1)
    pool_t = pool_kv.transpose(0, 1, 3, 4, 5, 2)
    lf_t = pool_lf.transpose(0, 1, 3, 2)
    attn_s = even_decode(pool_t, lf_t, li, page_table.reshape(-1), n_pages, qt, cn, tbt,
                         kvs_arr[:, None, :1024], kvs_arr[:, None, 1024:], bo, lam_p, subln_g, lam_init,
                         pps=math.gcd(n_pages, 16))
    ekv_s = kvs_arr.reshape(DB, 1, 2, 16, HEAD_DIM)
    elf_s = lf_new.reshape(DB, 1, n_fox)
    return attn_p.reshape(B * T, -1), attn_s.reshape(DB, -1), ekv_p, ekv_s, elf_p, elf_s


def odd_layer(hp, hs, pool, win_state, li, page_table, g0, w_in, b_gate, w_pos, b_cmp, w_cout, rel_table):
    B, T, D = hp.shape
    DB, S, _ = hs.shape
    assert S == 1
    d_q, d_kv = 1024, 512
    w_q, w_c, w_s, w_w, w_g = jnp.split(w_in, [d_q, d_q + d_kv, d_q + 2 * d_kv, d_q + 3 * d_kv], axis=1)
    n_gate = 12
    w_g4 = jnp.concatenate([_pad_cols(w_g[:, g * n_gate:(g + 1) * n_gate], LANES) for g in range(4)], axis=1)
    w_qg = jnp.concatenate([w_q, w_g4], axis=1).astype(BF16)
    w_cs = jnp.concatenate([w_c, w_s], axis=1).astype(BF16)
    w_w16 = w_w.astype(BF16)
    bg = b_gate.astype(F32)
    bg_pad = jnp.pad(bg.reshape(4, 1, n_gate), ((0, 0), (0, 0), (0, LANES - n_gate)))
    wp = w_pos.reshape(CMP_BLOCK, d_kv)
    w0, w1 = _pool_weights_t(wp)
    b_col = b_cmp.reshape(d_kv, 1).astype(F32)
    eye4 = jnp.eye(4, dtype=F32)
    w0t = jnp.kron(eye4, w_cout[0].T).astype(BF16)
    w1t = jnp.kron(eye4, w_cout[1].T).astype(BF16)
    table = rel_table

    d_g = d_kv // 4
    dup = lambda w: jnp.concatenate([w[:, g * HEAD_DIM:(g + 1) * HEAD_DIM] for g in range(4) for _ in range(2)], axis=1)
    w_qk = jnp.concatenate([w_q * (SCALE * LOG2E), dup(w_s[:, :2 * d_g]), dup(w_w[:, :2 * d_g])], axis=1).astype(BF16)
    w_g16 = jnp.concatenate([_pad_cols(w_g[:, g * n_gate:(g + 1) * n_gate], 16) for g in range(4)], axis=1)
    w_wg_t = jnp.concatenate([w_w, w_g16], axis=1).T.astype(BF16)
    bg_col = jnp.pad(bg.reshape(4, n_gate, 1), ((0, 0), (0, 16 - n_gate), (0, 0)))
    qk_arr = norm_proj(hp.reshape(B * T, D), g0, w_qk, 512, BF16).reshape(B, T, -1)
    cs_t = norm_proj_t(hp, g0, w_cs.T, 512)
    wg_t = norm_proj_t(hp, g0, w_wg_t, w_wg_t.shape[0])
    ck_t, cv_t = cmp_build(cs_t, w0, w1, b_col, w0t, w1t)
    nb = T // CMP_STRIDE
    n_blk = nb - 1
    n_selb = T // SEL_BLOCK
    tq = LANES
    tk = min(T, 512)
    t_all = jnp.arange(T)
    e_pos = jnp.arange(nb) * CMP_STRIDE + CMP_BLOCK - 1
    cbias_t = LOG2E * bias_table(table, t_all[None, :] - e_pos[:, None])
    toe_t = LOG2E * bias_table(table, toeplitz_dist(tq, T // tq).transpose(0, 2, 1))
    cover_t = jnp.asarray(cover_matrix_np(n_blk, n_selb, nb, LANES).T, BF16)
    expand_t = np.zeros((T, LANES), np.float32)
    expand_t[np.arange(T), np.arange(T) // SEL_BLOCK] = 1.0
    attn_p = nsa_prompt_attention_t(qk_arr, ck_t, cv_t, cs_t, wg_t, bg_col, cbias_t, toe_t, cover_t,
                                    jnp.asarray(expand_t, BF16), tq, tk)
    okv_p = cs_t.reshape(B, 4, 4, HEAD_DIM, T).transpose(0, 4, 1, 2, 3)
    wn = min(WINDOW, T)
    win_p = wg_t[:, :d_kv, T - wn:].reshape(B, 2, 4, HEAD_DIM, wn).transpose(0, 4, 1, 2, 3)

    n_pages = page_table.shape[1]
    past = n_pages * PAGE_SIZE
    xs = hs.reshape(DB, D)
    qgs = norm_proj(xs, g0, w_qg, 512)
    css = norm_proj(xs, g0, w_cs, 512)
    wns = norm_proj(xs, g0, w_w16, 512)
    pt_flat = page_table.reshape(-1)
    nbs = past // CMP_STRIDE
    n_past_blk = past // SEL_BLOCK
    n_selb_s = n_past_blk + 1
    nsp = -(-n_selb_s // LANES) * LANES
    qs = qgs[:, :d_q]
    segz = np.zeros((16, 256), np.float32)
    for h_ in range(16):
        segz[h_, (h_ // 4) * HEAD_DIM:(h_ // 4 + 1) * HEAD_DIM] = 1.0
    qz16 = (jnp.tile(qs.reshape(DB, 16, 1, HEAD_DIM), (1, 1, 4, 1)).reshape(DB, 16, 256) * SCALE * segz[None]).astype(BF16)
    e_pos_s = jnp.arange(nbs) * CMP_STRIDE + CMP_BLOCK - 1
    cbias_s = bias_table(table, past - e_pos_s)
    cover_s = jnp.asarray(cover_matrix_np(nbs, n_selb_s, nbs, nsp), BF16)
    pool_t = pool.transpose(0, 1, 3, 4, 5, 2)
    n_top = min(N_SEL, n_selb_s)
    oc16, idx8 = odd_decode_cmp(pool_t, li, pt_flat, n_pages, w0, w1, b_col, w0t, w1t, qz16, cbias_s, cover_s,
                                n_selb_s, past)
    idx_flat = idx8[:, :4, :n_top].reshape(-1)
    oc4 = oc16.reshape(DB, 4, 4, 4, HEAD_DIM)[:, np.arange(4), :, np.arange(4)].transpose(1, 0, 2, 3)
    qg = _pad_rows8(qs.reshape(DB, 4, 4, HEAD_DIM) * SCALE).astype(BF16)
    pos_p = jnp.arange(past).reshape(n_pages, PAGE_SIZE)
    tbp = bias_table(table, past - pos_p).reshape(4, 4, n_pages, PAGE_SIZE).transpose(2, 0, 1, 3)
    tbp = _pad_rows8(tbp).reshape(n_pages * 32, PAGE_SIZE)
    wb = win_state.shape[2]
    tw = _pad_rows8(bias_table(table, wb - jnp.arange(wb)).reshape(4, 4, wb)).reshape(32, wb)
    tw0 = _pad_rows8(jnp.broadcast_to(table[0].reshape(4, 4, 1), (4, 4, LANES))).reshape(32, LANES)
    g48 = jnp.concatenate([qgs[:, d_q + g * LANES:d_q + g * LANES + n_gate] for g in range(4)], axis=1) + bg[None]
    graw = jnp.pad(_pad_rows8(g48.reshape(DB, 4, 4, 3)), ((0, 0), (0, 0), (0, 0), (0, LANES - 3)))
    win_t = win_state.transpose(0, 1, 3, 4, 5, 2)
    s_new = css[:, d_kv:].reshape(DB, 2, 4, HEAD_DIM).transpose(0, 2, 1, 3)
    w_new = wns.reshape(DB, 2, 4, HEAD_DIM).transpose(0, 2, 1, 3)
    o4 = odd_decode_sel(pool_t, li, idx_flat, pt_flat, n_pages, win_t, qg, s_new, tbp, w_new, tw, tw0,
                        _pad_rows8(oc4), graw, n_top, past)
    attn_s = o4[:, :, :4, :].reshape(DB, d_q)
    okv_s = css.reshape(DB, 1, 4, 4, HEAD_DIM)
    keys_t = jnp.concatenate([win_t[li], wns.reshape(DB, 2, 4, HEAD_DIM, 1)], axis=-1)
    keep = min(WINDOW, wb + 1)
    win_s = keys_t[..., wb + 1 - keep:].transpose(0, 4, 1, 2, 3)
    return attn_p, attn_s, okv_p, okv_s, win_p, win_s


def kernel(x_prompt, x_sample, cache_even_kv, cache_even_logf, cache_odd_kv, state_odd_win, page_table, rel_table, norm_g, w_even_in, b_even_f, diff_lambda, diff_subln_g, w_even_out, w_odd_in, b_odd_gate, w_cmp_pos, b_cmp, w_cmp_out, w_odd_out, w_ffn_in, w_ffn_out):
    B, T, D = x_prompt.shape
    DB, S, _ = x_sample.shape
    depth = norm_g.shape[0]
    hp = x_prompt.reshape(B * T, D)
    hs = x_sample.reshape(DB * S, D)
    ekv_p, ekv_s, elf_p, elf_s = [], [], [], []
    okv_p, okv_s, win_p, win_s = [], [], [], []
    for layer in range(depth):
        g = norm_g[layer]
        li = layer // 2
        hp3, hs3 = hp.reshape(B, T, D), hs.reshape(DB, S, D)
        if layer % 2 == 0:
            a_p, a_s, kvp, kvs, lfp, lfs = even_layer(
                hp3, hs3, cache_even_kv, cache_even_logf, li, page_table, g[0], w_even_in[li], b_even_f[li],
                diff_lambda[li].astype(F32), diff_subln_g[li].astype(F32), rel_table, layer)
            w_o = w_even_out[li].astype(BF16)
            ekv_p.append(kvp); ekv_s.append(kvs); elf_p.append(lfp); elf_s.append(lfs)
        else:
            a_p, a_s, kvp, kvs, wp_, ws_ = odd_layer(
                hp3, hs3, cache_odd_kv, state_odd_win, li, page_table, g[0], w_odd_in[li], b_odd_gate[li],
                w_cmp_pos[li], b_cmp[li], w_cmp_out[li], rel_table)
            w_o = w_odd_out[li].astype(BF16)
            okv_p.append(kvp); okv_s.append(kvs); win_p.append(wp_); win_s.append(ws_)
        hp = (out_proj_residual if a_p.ndim == 2 else out_proj_residual_t)(a_p, w_o, g[1], hp)
        hs = out_proj_residual(a_s, w_o, g[1], hs)
        w1, w2 = ffn_weights(w_ffn_in[layer], w_ffn_out[layer], 256)
        hp = ffn_residual(hp, g[2], w1, w2, g[3])
        hs = ffn_residual(hs, g[2], w1, w2, g[3])
    return (hp.reshape(B, T, D), hs.reshape(DB, S, D), jnp.stack(ekv_p), jnp.stack(ekv_s), jnp.stack(elf_p), jnp.stack(elf_s),
            jnp.stack(okv_p), jnp.stack(okv_s), jnp.stack(win_p), jnp.stack(win_s))
```

```python
import functools
import math

import numpy as np
import jax
import jax.numpy as jnp
from jax import lax
from jax.experimental import pallas as pl
from jax.experimental.pallas import tpu as pltpu

F32 = jnp.float32
BF16 = jnp.bfloat16
I32 = jnp.int32

HEAD_DIM = 64
LANES = 128
PAGE_SIZE = 128
CMP_BLOCK = 32
CMP_STRIDE = 16
SEL_BLOCK = 64
N_SEL = 16
WINDOW = 512
N_BUCKETS = 32
MAX_DISTANCE = 1024
SCALE = HEAD_DIM ** -0.5
LOG2E = math.log2(math.e)
EPS = 1e-6
NEG = -1e30
FORCE = 1e9
PAD_SCORE = -3.0e38
VMEM_LIMIT = 56 * 1024 * 1024


def _cparams(n_axes):
    return pltpu.CompilerParams(dimension_semantics=("arbitrary",) * n_axes,
                                vmem_limit_bytes=VMEM_LIMIT)


def _dot(a, b):
    return jnp.dot(a, b, preferred_element_type=F32)


def _dot_nt(a, b):
    return lax.dot_general(a, b, (((1,), (1,)), ((), ())), preferred_element_type=F32)


def _split3(x):
    hi = x.astype(BF16)
    r1 = x - hi.astype(F32)
    mid = r1.astype(BF16)
    lo = (r1 - mid.astype(F32)).astype(BF16)
    return hi, mid, lo


def _dot_exact_rhs01(x, m01):
    hi, mid, lo = _split3(x)
    return _dot(hi, m01) + _dot(mid, m01) + _dot(lo, m01)


def _dot_exact_lhs01(m01, x):
    hi, mid, lo = _split3(x)
    return _dot(m01, hi) + _dot(m01, mid) + _dot(m01, lo)


def _lane_tile(x, n):
    return x if n == 1 else jnp.concatenate([x] * n, axis=1)


def _aligned(x, m):
    return x if isinstance(x, int) else pl.multiple_of(x, m)


def _shift_left_one_lane(x):
    n = x.shape[1] // LANES
    if n == 0:
        return pltpu.roll(x, x.shape[1] - 1, 1)
    keep = lax.broadcasted_iota(I32, (1, LANES), 1) < LANES - 1
    rolled = [pltpu.roll(x[:, c * LANES:(c + 1) * LANES], LANES - 1, 1) for c in range(n)]
    out = [jnp.where(keep, rolled[c], rolled[min(c + 1, n - 1)]) for c in range(n)]
    return out[0] if n == 1 else jnp.concatenate(out, axis=1)


def _silu(x):
    return x * (1.0 / (1.0 + jnp.exp(-x)))


def rel_bucket(dist):
    n = jnp.maximum(dist, 0)
    exact = N_BUCKETS // 2
    nf = jnp.maximum(n, 1).astype(F32)
    large = exact + (jnp.log(nf / exact) / math.log(MAX_DISTANCE / exact) * (N_BUCKETS - exact)).astype(I32)
    return jnp.where(n < exact, n, jnp.minimum(large, N_BUCKETS - 1))


def bias_table(table, dist):
    onehot = (rel_bucket(dist)[..., None] == jnp.arange(N_BUCKETS)).astype(F32)
    return jnp.einsum('...k,kh->h...', onehot, table.astype(F32), precision=lax.Precision.HIGHEST)


def _norm_rows(x, g):
    ms = jnp.mean(x * x, axis=-1, keepdims=True)
    return x * lax.rsqrt(ms + EPS) * g


def _proj_kernel(x_ref, g_ref, w_ref, o_ref, h_ref):
    @pl.when(pl.program_id(1) == 0)
    def _():
        h_ref[...] = _norm_rows(x_ref[...], g_ref[...]).astype(BF16)

    o_ref[...] = _dot(h_ref[...], w_ref[...]).astype(o_ref.dtype)


def norm_proj(x, g, w, tn, out_dtype=F32):
    M, D = x.shape
    N = w.shape[1]
    tm = min(M, 1024)
    return pl.pallas_call(
        _proj_kernel,
        grid=(M // tm, N // tn),
        in_specs=[pl.BlockSpec((tm, D), lambda i, j: (i, 0)),
                  pl.BlockSpec((1, D), lambda i, j: (0, 0)),
                  pl.BlockSpec((D, tn), lambda i, j: (0, j))],
        out_specs=pl.BlockSpec((tm, tn), lambda i, j: (i, j)),
        out_shape=jax.ShapeDtypeStruct((M, N), out_dtype),
        scratch_shapes=[pltpu.VMEM((tm, D), BF16)],
        compiler_params=_cparams(2),
        name="norm_proj",
    )(x, g.reshape(1, D), w)


def _proj_t_kernel(x_ref, g_ref, w_ref, o_ref, h_ref):
    @pl.when(pl.program_id(2) == 0)
    def _():
        h_ref[...] = _norm_rows(x_ref[...], g_ref[...]).astype(BF16)

    o_ref[...] = _dot_nt(w_ref[...], h_ref[...])


def norm_proj_t(x, g, w_t, tn):
    B, T, D = x.shape
    N = w_t.shape[0]
    tm = min(T, 1024)
    return pl.pallas_call(
        _proj_t_kernel,
        grid=(B, T // tm, N // tn),
        in_specs=[pl.BlockSpec((None, tm, D), lambda b, i, j: (b, i, 0)),
                  pl.BlockSpec((1, D), lambda b, i, j: (0, 0)),
                  pl.BlockSpec((tn, D), lambda b, i, j: (j, 0))],
        out_specs=pl.BlockSpec((None, tn, tm), lambda b, i, j: (b, j, i)),
        out_shape=jax.ShapeDtypeStruct((B, N, T), F32),
        scratch_shapes=[pltpu.VMEM((tm, D), BF16)],
        compiler_params=_cparams(3),
        name="norm_proj_t",
    )(x, g.reshape(1, D), w_t)


def _out_res_kernel(a_ref, w_ref, g_ref, r_ref, o_ref):
    y = _dot(a_ref[...].astype(BF16), w_ref[...])
    o_ref[...] = r_ref[...] + _norm_rows(y, g_ref[...])


def out_proj_residual(a, w, g, res):
    M, K = a.shape
    D = w.shape[1]
    tm = min(M, 512)
    return pl.pallas_call(
        _out_res_kernel,
        grid=(M // tm,),
        in_specs=[pl.BlockSpec((tm, K), lambda i: (i, 0)),
                  pl.BlockSpec((K, D), lambda i: (0, 0)),
                  pl.BlockSpec((1, D), lambda i: (0, 0)),
                  pl.BlockSpec((tm, D), lambda i: (i, 0))],
        out_specs=pl.BlockSpec((tm, D), lambda i: (i, 0)),
        out_shape=jax.ShapeDtypeStruct((M, D), F32),
        compiler_params=_cparams(1),
        name="out_proj_residual",
    )(a, w, g.reshape(1, D), res)


def _ffn_kernel(x_ref, g2_ref, wgu_ref, wo_ref, g3_ref, o_ref, h_ref, acc_ref):
    f = pl.program_id(1)
    tf = wo_ref.shape[0]

    @pl.when(f == 0)
    def _():
        h_ref[...] = _norm_rows(x_ref[...], g2_ref[...]).astype(BF16)
        acc_ref[...] = jnp.zeros_like(acc_ref)

    gu = _dot(h_ref[...], wgu_ref[...])
    acc_ref[...] += _dot((_silu(gu[:, :tf]) * gu[:, tf:]).astype(BF16), wo_ref[...])

    @pl.when(f == pl.num_programs(1) - 1)
    def _():
        o_ref[...] = x_ref[...] + _norm_rows(acc_ref[...], g3_ref[...])


def ffn_weights(w_in, w_out, tf):
    D = w_in.shape[0]
    Fdim = w_out.shape[0]
    nf = Fdim // tf
    gate = w_in[:, :Fdim].reshape(D, nf, tf)
    up = w_in[:, Fdim:].reshape(D, nf, tf)
    return jnp.concatenate([gate, up], axis=2).transpose(1, 0, 2).astype(BF16), w_out.astype(BF16)


def ffn_residual(x, g2, w_gu, w_out, g3):
    M, D = x.shape
    nf, _, tf2 = w_gu.shape
    tf = tf2 // 2
    tm = min(M, 1024)
    return pl.pallas_call(
        _ffn_kernel,
        grid=(M // tm, nf),
        in_specs=[pl.BlockSpec((tm, D), lambda i, f: (i, 0)),
                  pl.BlockSpec((1, D), lambda i, f: (0, 0)),
                  pl.BlockSpec((None, D, tf2), lambda i, f: (f, 0, 0)),
                  pl.BlockSpec((tf, D), lambda i, f: (f, 0)),
                  pl.BlockSpec((1, D), lambda i, f: (0, 0))],
        out_specs=pl.BlockSpec((tm, D), lambda i, f: (i, 0)),
        out_shape=jax.ShapeDtypeStruct((M, D), F32),
        scratch_shapes=[pltpu.VMEM((tm, D), BF16), pltpu.VMEM((tm, D), F32)],
        compiler_params=_cparams(2),
        name="ffn_residual",
    )(x, g2.reshape(1, D), w_gu, w_out, g3.reshape(1, D))


def _logf_kernel(f_ref, b_ref, tri_ref, lf_ref, cum_ref, carry_ref):
    @pl.when(pl.program_id(1) == 0)
    def _():
        carry_ref[...] = jnp.zeros_like(carry_ref)

    x = f_ref[...] + b_ref[...]
    lf = jnp.minimum(x, 0.0) - jnp.log(1.0 + jnp.exp(-jnp.abs(x)))
    lf_ref[...] = lf
    cum = _dot_exact_lhs01(tri_ref[...], lf) + carry_ref[...]
    cum_ref[...] = cum
    carry_ref[...] = cum[cum.shape[0] - 1:, :]


def logf_cumsum(fq, col_block, b_pad, tq):
    B, T, _ = fq.shape
    tri = jnp.asarray(np.tril(np.ones((tq, tq), np.float32)), BF16)
    return pl.pallas_call(
        _logf_kernel,
        grid=(B, T // tq),
        in_specs=[pl.BlockSpec((None, tq, LANES), lambda b, i: (b, i, col_block)),
                  pl.BlockSpec((1, LANES), lambda b, i: (0, 0)),
                  pl.BlockSpec((tq, tq), lambda b, i: (0, 0))],
        out_specs=[pl.BlockSpec((None, tq, LANES), lambda b, i: (b, i, 0)),
                   pl.BlockSpec((None, tq, LANES), lambda b, i: (b, i, 0))],
        out_shape=[jax.ShapeDtypeStruct((B, T, LANES), F32)] * 2,
        scratch_shapes=[pltpu.VMEM((1, LANES), F32)],
        compiler_params=_cparams(2),
        name="logf_cumsum",
    )(fq, b_pad, tri)


def _lambda_value(lp, lam_init):
    a = jnp.sum(lp[0:1, :] * lp[1:2, :], axis=-1, keepdims=True)
    b = jnp.sum(lp[2:3, :] * lp[3:4, :], axis=-1, keepdims=True)
    return jnp.exp(a) - jnp.exp(b) + lam_init


def _even_attn_kernel(lp_ref, q_ref, k_ref, v_ref, c_ref, tb_ref, sg_ref, o_ref, kb, vb, m_ref, l_ref, acc_ref,
                      *, tq, tk, n_fox, lam_init):
    c = pl.program_id(1)
    qi = pl.program_id(2)

    @pl.when(qi == 0)
    def _():
        kb[...] = k_ref[...].astype(BF16)
        vb[...] = v_ref[...].astype(BF16)

    lo = lax.broadcasted_iota(I32, (1, LANES), 1) < HEAD_DIM
    q = q_ref[...] * SCALE
    qh = (jnp.where(lo, q, 0.0).astype(BF16), jnp.where(lo, 0.0, q).astype(BF16))
    t0 = qi * tq
    n_full = t0 // tk
    sub = tk // tq
    nd = tb_ref.shape[0]

    def tile(kt, bias_fn, first, diag):
        start = _aligned(kt * tk, tk)
        k = kb[:, pl.ds(start, tk)]
        v = vb[:, pl.ds(start, tk)]
        if diag:
            row = t0 + lax.broadcasted_iota(I32, (tq, tk), 0)
            col = start + lax.broadcasted_iota(I32, (tq, tk), 1)
            causal = col <= row
        for h in range(2):
            s = _dot(qh[h], k) + bias_fn(h, kt, start)
            if diag:
                s = jnp.where(causal, s, NEG)
            rmax = jnp.broadcast_to(jnp.max(s, axis=-1, keepdims=True), (tq, LANES))
            if first:
                m_new = rmax
            else:
                m_prev = m_ref[h]
                m_new = jnp.maximum(m_prev, rmax)
                alpha = jnp.exp(m_prev - m_new)
            p = jnp.exp(s - _lane_tile(m_new, tk // LANES))
            rsum = jnp.broadcast_to(jnp.sum(p, axis=-1, keepdims=True), (tq, LANES))
            pv = _dot_nt(p.astype(BF16), v)
            if first:
                l_ref[h] = rsum
                acc_ref[h] = pv
            else:
                l_ref[h] = alpha * l_ref[h] + rsum
                acc_ref[h] = alpha * acc_ref[h] + pv
            m_ref[h] = m_new

    def run(bias_fn):
        @pl.when(n_full == 0)
        def _():
            tile(0, bias_fn, True, True)

        @pl.when(n_full > 0)
        def _():
            tile(0, bias_fn, True, False)

            def body(kt, carry):
                tile(kt, bias_fn, False, False)
                return carry

            lax.fori_loop(1, n_full, body, 0)
            tile(n_full, bias_fn, False, True)

    @pl.when(c < n_fox)
    def _():
        def bias_fn(h, kt, start):
            return -c_ref[h:h + 1, pl.ds(start, tk)]

        run(bias_fn)
        o_ref[...] = jnp.where(lo, acc_ref[0] / l_ref[0], acc_ref[1] / l_ref[1])

    @pl.when(c >= n_fox)
    def _():
        def bias_fn(h, kt, start):
            d0 = qi - kt * sub
            return jnp.concatenate([tb_ref[jnp.clip(d0 - cc, 0, nd - 1)] for cc in range(sub)], axis=1)

        run(bias_fn)
        lam = _lambda_value(lp_ref[...], lam_init)
        o = acc_ref[0] / l_ref[0] - lam * (acc_ref[1] / l_ref[1])
        ms = jnp.mean(o * o, axis=-1, keepdims=True)
        o_ref[...] = o * lax.rsqrt(ms + EPS) * sg_ref[...] * (1.0 - lam_init)


def even_prompt_attention(q_arr, kv_t, cum_pairs, toe, lam_p, subln_g, lam_init, tq, tk):
    B, T, _ = q_arr.shape
    n_fox = 4
    n_tiles = 8
    nd = toe.shape[1]
    kern = functools.partial(_even_attn_kernel, tq=tq, tk=tk, n_fox=n_fox, lam_init=lam_init)
    return pl.pallas_call(
        kern,
        grid=(B, n_tiles, T // tq),
        in_specs=[pl.BlockSpec((4, HEAD_DIM), lambda b, c, i: (0, 0)),
                  pl.BlockSpec((None, tq, LANES), lambda b, c, i: (b, i, c)),
                  pl.BlockSpec((None, LANES, T), lambda b, c, i: (b, c, 0)),
                  pl.BlockSpec((None, LANES, T), lambda b, c, i: (b, n_tiles + c, 0)),
                  pl.BlockSpec((None, None, 2, T), lambda b, c, i: (b, jnp.minimum(c, n_fox - 1), 0, 0)),
                  pl.BlockSpec((None, nd, tq, tq), lambda b, c, i: (jnp.maximum(c - n_fox, 0), 0, 0, 0)),
                  pl.BlockSpec((1, LANES), lambda b, c, i: (0, 0))],
        out_specs=pl.BlockSpec((None, tq, LANES), lambda b, c, i: (b, i, c)),
        out_shape=jax.ShapeDtypeStruct((B, T, n_tiles * LANES), F32),
        scratch_shapes=[pltpu.VMEM((LANES, T), BF16), pltpu.VMEM((LANES, T), BF16),
                        pltpu.VMEM((2, tq, LANES), F32), pltpu.VMEM((2, tq, LANES), F32), pltpu.VMEM((2, tq, LANES), F32)],
        compiler_params=_cparams(3),
        name="even_prompt_attention",
    )(lam_p, q_arr, kv_t, kv_t, cum_pairs, toe, subln_g.reshape(1, LANES))


def _even_attn_t_kernel(lp_ref, q_ref, k_ref, v_ref, c_ref, tbt_ref, sg_ref, o_ref, kb, vb, cb, m_ref, acc_ref,
                        *, tq, tk, n_fox, lam_init):
    c = pl.program_id(1)
    qi = pl.program_id(2)
    T = k_ref.shape[1]
    n_val = v_ref.shape[0]

    @pl.when(qi == 0)
    def _():
        kb[...] = k_ref[...].T.astype(BF16)
        vb[0:n_val, :] = v_ref[...].astype(BF16)
        vb[n_val:, :] = jnp.ones((16, T), BF16)
        for h in range(2):
            cb[h] = jnp.broadcast_to(c_ref[h:h + 1, :] * (-LOG2E), (LANES, T)).T

    lo = lax.broadcasted_iota(I32, (1, LANES), 1) < HEAD_DIM
    q = q_ref[...] * (SCALE * LOG2E)
    qs = jnp.concatenate([jnp.where(lo, q, 0.0), jnp.where(lo, 0.0, q)], axis=0).astype(BF16)
    t0 = qi * tq
    tpos = t0 + lax.broadcasted_iota(I32, (1, tq), 1)
    n_full = t0 // tk
    sub = tk // tq
    nd = tbt_ref.shape[0]

    def tile(kt, bias_fn, first, diag):
        start = _aligned(kt * tk, tk)
        s = _dot_nt(kb[pl.ds(start, tk), :], qs) + bias_fn(kt, start)
        if diag:
            visible = start + lax.broadcasted_iota(I32, (tk, 1), 0) <= tpos
            s = s + _lane_tile(jnp.where(visible, 0.0, NEG), 2)
        s_max = jnp.max(s, axis=0, keepdims=True)
        if first:
            m_new = s_max
        else:
            m_prev = m_ref[...]
            m_new = jnp.maximum(m_prev, s_max)
        p = jnp.exp2((s - m_new).astype(BF16))
        pv = _dot(vb[:, pl.ds(start, tk)], p)
        acc_ref[...] = pv if first else jnp.exp2(m_prev - m_new) * acc_ref[...] + pv
        m_ref[...] = m_new

    def run(bias_fn):
        @pl.when(n_full == 0)
        def _():
            tile(0, bias_fn, True, True)

        @pl.when(n_full > 0)
        def _():
            tile(0, bias_fn, True, False)

            def body(kt, carry):
                tile(kt, bias_fn, False, False)
                return carry

            lax.fori_loop(1, n_full, body, 0)
            tile(n_full, bias_fn, False, True)

    @pl.when(c < n_fox)
    def _():
        def bias_fn(kt, start):
            return jnp.concatenate([_lane_tile(cb[h, pl.ds(start, tk), :], tq // LANES) for h in range(2)], axis=1)

        run(bias_fn)
        acc = acc_ref[...]
        o_ref[0:HEAD_DIM, :] = acc[0:HEAD_DIM, 0:tq] / acc[n_val:n_val + 1, 0:tq]
        o_ref[HEAD_DIM:, :] = acc[HEAD_DIM:n_val, tq:] / acc[n_val:n_val + 1, tq:]

    @pl.when(c >= n_fox)
    def _():
        def bias_fn(kt, start):
            d0 = qi - kt * sub
            rows = jnp.concatenate([tbt_ref[jnp.clip(d0 - cc, 0, nd - 1)] for cc in range(sub)], axis=0)
            return _lane_tile(rows, 2)

        run(bias_fn)
        acc = acc_ref[...]
        lam = _lambda_value(lp_ref[...], lam_init)
        o = acc[0:n_val, 0:tq] / acc[n_val:n_val + 1, 0:tq] - lam * (acc[0:n_val, tq:] / acc[n_val:n_val + 1, tq:])
        ms = jnp.mean(o * o, axis=0, keepdims=True)
        o_ref[...] = o * lax.rsqrt(ms + EPS) * sg_ref[...] * (1.0 - lam_init)


def even_prompt_attention_t(q_arr, kv_t, cum_pairs, toe_t, lam_p, subln_g, lam_init, tq, tk):
    B, T, _ = q_arr.shape
    n_fox = 4
    n_tiles = 8
    nd = toe_t.shape[1]
    assert tq % LANES == 0 and tk % tq == 0 and T % tk == 0
    kern = functools.partial(_even_attn_t_kernel, tq=tq, tk=tk, n_fox=n_fox, lam_init=lam_init)
    return pl.pallas_call(
        kern,
        grid=(B, n_tiles, T // tq),
        in_specs=[pl.BlockSpec((4, HEAD_DIM), lambda b, c, i: (0, 0)),
                  pl.BlockSpec((None, tq, LANES), lambda b, c, i: (b, i, c)),
                  pl.BlockSpec((None, LANES, T), lambda b, c, i: (b, c, 0)),
                  pl.BlockSpec((None, LANES, T), lambda b, c, i: (b, n_tiles + c, 0)),
                  pl.BlockSpec((None, None, 2, T), lambda b, c, i: (b, jnp.minimum(c, n_fox - 1), 0, 0)),
                  pl.BlockSpec((None, nd, tq, tq), lambda b, c, i: (jnp.maximum(c - n_fox, 0), 0, 0, 0)),
                  pl.BlockSpec((LANES, 1), lambda b, c, i: (0, 0))],
        out_specs=pl.BlockSpec((None, LANES, tq), lambda b, c, i: (b, c, i)),
        out_shape=jax.ShapeDtypeStruct((B, n_tiles * LANES, T), F32),
        scratch_shapes=[pltpu.VMEM((T, LANES), BF16), pltpu.VMEM((LANES + 16, T), BF16), pltpu.VMEM((2, T, LANES), F32),
                        pltpu.VMEM((1, 2 * tq), F32), pltpu.VMEM((LANES + 16, 2 * tq), F32)],
        compiler_params=_cparams(3),
        name="even_prompt_attention_t",
    )(lam_p, q_arr, kv_t, kv_t, cum_pairs, toe_t, subln_g.reshape(LANES, 1))


def toeplitz_dist(tq, nd):
    i = np.arange(tq)[:, None]
    j = np.arange(tq)[None, :]
    return jnp.asarray(np.maximum(np.arange(nd)[:, None, None] * tq + (i - j)[None], 0), I32)


def _pool_weights_t(w_pos_rows):
    reps = LANES // CMP_STRIDE
    w0 = jnp.tile(w_pos_rows[:CMP_STRIDE].T, (1, reps))
    w1 = jnp.tile(w_pos_rows[CMP_STRIDE:].T, (1, reps))
    return w0.astype(F32), w1.astype(F32)


def _group_sum_stack(n_tiles):
    spt = LANES // CMP_STRIDE
    assert n_tiles * spt <= LANES
    m = np.zeros((n_tiles, 2 * LANES, 2 * LANES), np.float32)
    pos = np.arange(LANES)
    for i in range(n_tiles):
        m[i, pos, i * spt + pos // CMP_STRIDE] = 1.0
        m[i, LANES + pos, LANES + i * spt + pos // CMP_STRIDE] = 1.0
    return m


def _pool2(x, w0, w1, gmat2, two_pass):
    y = jnp.concatenate([x * w0, x * w1], axis=1)
    hi = y.astype(BF16)
    out = _dot(hi, gmat2)
    if two_pass:
        out = out + _dot((y - hi.astype(F32)).astype(BF16), gmat2)
    return out


def _cmp_finish_t(p0, p1, b_col, w0t, w1t):
    half = w0t.shape[0]
    z = _silu(p0 + _shift_left_one_lane(p1) + b_col)
    ck = _dot(w0t, z[:half].astype(BF16))
    cv = _dot(w1t, z[half:].astype(BF16))
    return ck, cv


def _cmp_build_kernel(c_ref, w0_ref, w1_ref, g_ref, b_ref, w0t_ref, w1t_ref, ck_ref, cv_ref):
    T = c_ref.shape[1]
    nb = ck_ref.shape[1]
    w0 = w0_ref[...]
    w1 = w1_ref[...]
    acc = None
    for lt in range(T // LANES):
        a = _pool2(c_ref[:, lt * LANES:(lt + 1) * LANES], w0, w1, g_ref[lt], True)
        acc = a if acc is None else acc + a
    ck, cv = _cmp_finish_t(acc[:, 0:nb], acc[:, LANES:LANES + nb], b_ref[...], w0t_ref[...], w1t_ref[...])
    ck_ref[...] = ck
    cv_ref[...] = cv


def cmp_build(cs_t, w0, w1, b_col, w0t, w1t):
    B, _, T = cs_t.shape
    nb = T // CMP_STRIDE
    width = w0.shape[0]
    half = width // 2
    n_tiles = T // LANES
    gmat = jnp.asarray(_group_sum_stack(n_tiles), BF16)
    const = lambda shape: pl.BlockSpec(shape, lambda b: (0,) * len(shape))
    return pl.pallas_call(
        _cmp_build_kernel,
        grid=(B,),
        in_specs=[pl.BlockSpec((None, width, T), lambda b: (b, 0, 0)),
                  const((width, LANES)), const((width, LANES)), const((n_tiles, 2 * LANES, 2 * LANES)), const((width, 1)),
                  const((half, half)), const((half, half))],
        out_specs=[pl.BlockSpec((None, half, nb), lambda b: (b, 0, 0))] * 2,
        out_shape=[jax.ShapeDtypeStruct((B, half, nb), F32)] * 2,
        compiler_params=_cparams(1),
        name="cmp_build",
    )(cs_t, w0, w1, gmat, b_col, w0t, w1t)


def _nsa_kernel(q_ref, ck_ref, cv_ref, sk_ref, sv_ref, wk_ref, wv_ref, gt_ref, bg_ref, cb_ref, tb_ref, cov_ref, ex_ref,
                o_ref, sk2, sv2, wk2, wv2, ck2, cv2, madd, p_ref, m_ref, l_ref, a_ref, acc_ref,
                *, tq, tk, n_blk, n_selb, n_top, win_chunks):
    qi = pl.program_id(2)

    @pl.when(qi == 0)
    def _():
        for src, dst in ((sk_ref, sk2), (sv_ref, sv2), (wk_ref, wk2), (wv_ref, wv2), (ck_ref, ck2), (cv_ref, cv2)):
            x = src[...].astype(BF16)
            dst[0:HEAD_DIM, :] = x
            dst[HEAD_DIM:, :] = x

    T = sk_ref.shape[1]
    nb = ck_ref.shape[1]
    nd = tb_ref.shape[1]
    sub = tk // tq
    t0 = qi * tq
    lane = lax.broadcasted_iota(I32, (1, LANES), 1)
    lo = lane < HEAD_DIM
    tcol = t0 + lax.broadcasted_iota(I32, (tq, 1), 0)
    q = q_ref[...] * SCALE
    parts = []
    for j in range(4):
        tile = q[:, (j // 2) * LANES:(j // 2 + 1) * LANES]
        parts.append(jnp.where(lo if j % 2 == 0 else jnp.logical_not(lo), tile, 0.0))
    qs = jnp.concatenate(parts, axis=0).astype(BF16)
    gates = 1.0 / (1.0 + jnp.exp(-(gt_ref[...] + bg_ref[...])))

    n_idx = lax.broadcasted_iota(I32, (1, nb), 1)
    validc = jnp.logical_and(n_idx * CMP_STRIDE + (CMP_BLOCK - 1) <= tcol, n_idx < n_blk)
    sc = _dot(qs, ck2[...])
    o_c = []
    psum = jnp.zeros((tq, nb), F32)
    for j in range(4):
        s = jnp.where(validc, sc[j * tq:(j + 1) * tq] + cb_ref[j], NEG)
        m = jnp.max(s, axis=-1, keepdims=True)
        e = jnp.where(validc, jnp.exp(s - m), 0.0)
        l = jnp.sum(e, axis=-1, keepdims=True)
        p = e / jnp.where(l > 0.0, l, 1.0)
        psum = psum + p
        o_c.append(_dot_nt(p.astype(BF16), cv2[...]))

    score = _dot_exact_rhs01(psum, cov_ref[...])
    cur = tcol // SEL_BLOCK
    forced = jnp.logical_or(lane == 0, jnp.logical_or(lane == cur, lane == cur - 1))
    score = jnp.where(forced, FORCE, score)
    score = jnp.where(lane * SEL_BLOCK <= tcol, score, NEG)
    score = jnp.where(lane < n_selb, score, PAD_SCORE)
    n_selp = -(-n_selb // 8) * 8
    sc_t = score.T[:n_selp]
    jrow = lax.broadcasted_iota(I32, (n_selp, 1), 0)
    rank = jnp.zeros((n_selp, tq), F32)
    for i in range(n_selb):
        ri = sc_t[i:i + 1, :]
        tie = jnp.where(i < jrow, 1.0, 0.0)
        rank = rank + jnp.where(ri > sc_t, 1.0, jnp.where(ri == sc_t, tie, 0.0))
    sel_t = jnp.where(rank < n_top, 1.0, 0.0)
    if n_selp < LANES:
        sel_t = jnp.concatenate([sel_t, jnp.zeros((LANES - n_selp, tq), F32)], axis=0)
    chosen = _dot(sel_t.T.astype(BF16), ex_ref[...])
    kpos = lax.broadcasted_iota(I32, (1, T), 1)
    madd[...] = jnp.where(jnp.logical_and(chosen > 0.5, kpos <= tcol), 0.0, NEG)

    def softmax_rows(j, s, first):
        rows = slice(j * tq, (j + 1) * tq)
        w = s.shape[1]
        rmax = jnp.broadcast_to(jnp.max(s, axis=-1, keepdims=True), (tq, LANES))
        if first:
            m_new = rmax
        else:
            m_prev = m_ref[rows]
            m_new = jnp.maximum(m_prev, rmax)
            alpha = jnp.exp(m_prev - m_new)
            a_ref[rows] = alpha
        p = jnp.exp(s - _lane_tile(m_new, w // LANES))
        rsum = jnp.broadcast_to(jnp.sum(p, axis=-1, keepdims=True), (tq, LANES))
        l_ref[rows] = rsum if first else alpha * l_ref[rows] + rsum
        m_ref[rows] = m_new
        p_ref[rows, 0:w] = p.astype(BF16)

    def toe_bias(j, d0, n_chunks):
        return jnp.concatenate([tb_ref[j, jnp.clip(d0 - cc, 0, nd - 1)] for cc in range(n_chunks)], axis=1)

    kt_d = qi // sub

    def sel_tile(kt, first):
        start = _aligned(kt * tk, tk)
        mask = madd[:, pl.ds(start, tk)]
        s4 = _dot(qs, sk2[:, pl.ds(start, tk)])
        for j in range(4):
            softmax_rows(j, s4[j * tq:(j + 1) * tq] + toe_bias(j, qi - kt * sub, sub) + mask, first)
        pv = _dot_nt(p_ref[:, 0:tk], sv2[:, pl.ds(start, tk)])
        acc_ref[...] = pv if first else a_ref[...] * acc_ref[...] + pv

    sel_tile(kt_d, True)

    def sel_body(i, carry):
        sel_tile(kt_d - 1 - i, False)
        return carry

    lax.fori_loop(0, kt_d, sel_body, 0)
    o_s = acc_ref[...] / l_ref[...]

    ww = win_chunks * LANES
    w0 = jnp.maximum(qi - (win_chunks - 1), 0)
    wstart = pl.multiple_of(w0 * tq, tq)
    dist = tcol - (wstart + lax.broadcasted_iota(I32, (1, ww), 1))
    wmask = jnp.where(jnp.logical_and(dist >= 0, dist <= WINDOW), 0.0, NEG)
    s4 = _dot(qs, wk2[:, pl.ds(wstart, ww)])
    for j in range(4):
        softmax_rows(j, s4[j * tq:(j + 1) * tq] + toe_bias(j, qi - w0, win_chunks) + wmask, True)
    o_w = _dot_nt(p_ref[:, 0:ww], wv2[:, pl.ds(wstart, ww)]) / l_ref[...]

    outs = []
    for j in range(4):
        rows = slice(j * tq, (j + 1) * tq)
        outs.append(gates[:, 3 * j:3 * j + 1] * o_c[j] + gates[:, 3 * j + 1:3 * j + 2] * o_s[rows]
                    + gates[:, 3 * j + 2:3 * j + 3] * o_w[rows])
    o_ref[:, 0:LANES] = jnp.where(lo, outs[0], outs[1])
    o_ref[:, LANES:2 * LANES] = jnp.where(lo, outs[2], outs[3])


def nsa_prompt_attention(qg_arr, ck_t, cv_t, cs_t, w_t, bg_pad, cbias, toe, cover, expand, tq, tk):
    B, T, _ = qg_arr.shape
    nb = ck_t.shape[2]
    nd = toe.shape[1]
    n_selb = T // SEL_BLOCK
    win_chunks = WINDOW // tq + 1
    ww = win_chunks * LANES
    assert tq == LANES and T >= ww and tk % tq == 0 and T % tk == 0
    kern = functools.partial(_nsa_kernel, tq=tq, tk=tk, n_blk=nb - 1, n_selb=n_selb, n_top=min(N_SEL, n_selb),
                             win_chunks=win_chunks)
    rows64 = lambda blk0: pl.BlockSpec((None, HEAD_DIM, T), lambda g, b, i: (b, blk0 + g, 0))
    wide = max(tk, ww)
    return pl.pallas_call(
        kern,
        grid=(4, B, T // tq),
        in_specs=[pl.BlockSpec((None, tq, 2 * LANES), lambda g, b, i: (b, i, g)),
                  pl.BlockSpec((None, HEAD_DIM, nb), lambda g, b, i: (b, g, 0)),
                  pl.BlockSpec((None, HEAD_DIM, nb), lambda g, b, i: (b, g, 0)),
                  rows64(8), rows64(12), rows64(0), rows64(4),
                  pl.BlockSpec((None, tq, LANES), lambda g, b, i: (b, i, 8 + g)),
                  pl.BlockSpec((None, 1, LANES), lambda g, b, i: (g, 0, 0)),
                  pl.BlockSpec((4, tq, nb), lambda g, b, i: (g, i, 0)),
                  pl.BlockSpec((4, nd, tq, tq), lambda g, b, i: (g, 0, 0, 0)),
                  pl.BlockSpec((nb, LANES), lambda g, b, i: (0, 0)),
                  pl.BlockSpec((LANES, T), lambda g, b, i: (0, 0))],
        out_specs=pl.BlockSpec((None, tq, 2 * LANES), lambda g, b, i: (b, i, g)),
        out_shape=jax.ShapeDtypeStruct((B, T, 8 * LANES), F32),
        scratch_shapes=[pltpu.VMEM((LANES, T), BF16)] * 4 + [pltpu.VMEM((LANES, nb), BF16)] * 2
                       + [pltpu.VMEM((tq, T), F32), pltpu.VMEM((4 * tq, wide), BF16)]
                       + [pltpu.VMEM((4 * tq, LANES), F32)] * 4,
        compiler_params=_cparams(3),
        name="nsa_prompt_attention",
    )(qg_arr, ck_t, cv_t, cs_t, cs_t, w_t, w_t, qg_arr, bg_pad, cbias, toe, cover, expand)


def _nsa_t_kernel(q_ref, ckt_ref, cvt_ref, sk_ref, svt_ref, wk_ref, wvt_ref, gt_ref, bg_ref, cbt_ref, tbt_ref, covt_ref, ext_ref,
                  o_ref, svb, wvb, ck2, cvb, *, tq, tk, n_blk, n_selb, n_top, win_chunks, n_chains):
    qi = pl.program_id(2)

    @pl.when(qi == 0)
    def _():
        ones = jnp.ones((16, svt_ref.shape[1]), BF16)
        svb[0:HEAD_DIM, :] = svt_ref[...].astype(BF16)
        svb[HEAD_DIM:, :] = ones
        wvb[0:HEAD_DIM, :] = wvt_ref[...].astype(BF16)
        wvb[HEAD_DIM:, :] = ones
        cvb[...] = cvt_ref[...].astype(BF16)
        ckt = ckt_ref[...]
        ck2[...] = jnp.concatenate([ckt, ckt], axis=0).T.astype(BF16)

    T = svt_ref.shape[1]
    nb = ckt_ref.shape[1]
    nd = tbt_ref.shape[1]
    sub = tk // tq
    t0 = qi * tq
    lane = lax.broadcasted_iota(I32, (1, LANES), 1)
    lo = lane < HEAD_DIM
    tpos = t0 + lax.broadcasted_iota(I32, (1, tq), 1)
    q = q_ref[...]
    zero = jnp.zeros((), BF16)
    parts = []
    for j in range(4):
        tile = q[:, (j // 2) * LANES:(j // 2 + 1) * LANES]
        parts.append(jnp.where(lo if j % 2 == 0 else jnp.logical_not(lo), tile, zero))
    qs = jnp.concatenate(parts, axis=0)
    gt = 1.0 / (1.0 + jnp.exp(-(gt_ref[...] + bg_ref[...])))

    def gate(r):
        return jnp.concatenate([gt[3 * j + r:3 * j + r + 1, :] for j in range(4)], axis=1)

    n_col = lax.broadcasted_iota(I32, (nb, 1), 0)
    validc = jnp.logical_and(n_col * CMP_STRIDE + (CMP_BLOCK - 1) <= tpos, n_col < n_blk)
    validc4 = _lane_tile(validc, 4)
    sc = _dot_nt(ck2[...], qs) + jnp.concatenate([cbt_ref[j] for j in range(4)], axis=1)
    sc = jnp.where(validc4, sc, NEG)
    m = jnp.max(sc, axis=0, keepdims=True)
    e = jnp.where(validc4, jnp.exp2(sc - m), 0.0)
    l = jnp.sum(e, axis=0, keepdims=True)
    p = e / jnp.where(l > 0.0, l, 1.0)
    o_c = _dot(cvb[...], p.astype(BF16))
    psum = p[:, 0:tq] + p[:, tq:2 * tq] + p[:, 2 * tq:3 * tq] + p[:, 3 * tq:4 * tq]

    score = _dot_exact_lhs01(covt_ref[...], psum)
    jrow = lax.broadcasted_iota(I32, (LANES, 1), 0)
    cur = tpos // SEL_BLOCK
    forced = jnp.logical_or(jrow == 0, jnp.logical_or(jrow == cur, jrow == cur - 1))
    score = jnp.where(forced, FORCE, score)
    score = jnp.where(jrow * SEL_BLOCK <= tpos, score, NEG)
    n_selp = -(-n_selb // 8) * 8
    sc_t = score[:n_selp]
    jr = jrow[:n_selp]
    rank = jnp.zeros((n_selp, tq), F32)
    for i in range(n_selb):
        ri = sc_t[i:i + 1, :]
        tie = jnp.where(i < jr, 1.0, 0.0)
        rank = rank + jnp.where(ri > sc_t, 1.0, jnp.where(ri == sc_t, tie, 0.0))
    sel_t = jnp.where(jnp.logical_and(rank < n_top, jr < n_selb), 1.0, 0.0)
    if n_selp < LANES:
        sel_t = jnp.concatenate([sel_t, jnp.zeros((LANES - n_selp, tq), F32)], axis=0)
    sel_b = sel_t.astype(BF16)

    hpc = 4 // n_chains
    qs_c = [qs[c * hpc * tq:(c + 1) * hpc * tq] for c in range(n_chains)]

    def toe_bias(c, d0, n_chunks):
        cols = [jnp.concatenate([tbt_ref[j, jnp.clip(d0 - cc, 0, nd - 1)] for cc in range(n_chunks)], axis=0)
                for j in range(c * hpc, (c + 1) * hpc)]
        return cols[0] if hpc == 1 else jnp.concatenate(cols, axis=1)

    def update(s, vt, state):
        s_max = jnp.max(s, axis=0, keepdims=True)
        if state is None:
            m_new = s_max
        else:
            m_prev, acc_prev = state
            m_new = jnp.maximum(m_prev, s_max)
        p = jnp.exp2((s - m_new).astype(BF16))
        pv = _dot(vt, p)
        if state is None:
            return m_new, pv
        return m_new, jnp.exp2(m_prev - m_new) * acc_prev + pv

    def attend(k_ref, v_ref, start, width, d0, mask, states):
        k = k_ref[pl.ds(start, width), :]
        v = v_ref[:, pl.ds(start, width)]
        mask_c = _lane_tile(mask, hpc)
        cs = range(n_chains)
        raws = [_dot_nt(k, qs_c[c]) for c in cs]
        ss = [raws[c] + toe_bias(c, d0, width // tq) + mask_c for c in cs]
        ms = [jnp.max(ss[c], axis=0, keepdims=True) for c in cs]
        if states is not None:
            ms = [jnp.maximum(states[c][0], ms[c]) for c in cs]
        ps = [jnp.exp2((ss[c] - ms[c]).astype(BF16)) for c in cs]
        pvs = [_dot(v, ps[c]) for c in cs]
        if states is None:
            return tuple((ms[c], pvs[c]) for c in cs)
        return tuple((ms[c], jnp.exp2(states[c][0] - ms[c]) * states[c][1] + pvs[c]) for c in cs)

    kt_d = qi // sub

    def sel_tile(kt, states):
        start = _aligned(kt * tk, tk)
        chosen = _dot(ext_ref[pl.ds(start, tk), :], sel_b) > 0.5
        if states is None:
            chosen = jnp.logical_and(chosen, start + lax.broadcasted_iota(I32, (tk, 1), 0) <= tpos)
        return attend(sk_ref, svb, start, tk, qi - kt * sub, jnp.where(chosen, 0.0, NEG), states)

    states = lax.fori_loop(0, kt_d, lambda i, st: sel_tile(kt_d - 1 - i, st), sel_tile(kt_d, None))
    o_s = jnp.concatenate([acc[0:HEAD_DIM] / acc[HEAD_DIM:HEAD_DIM + 1] for _, acc in states], axis=1)

    ww = win_chunks * tq
    w0 = jnp.maximum(qi - (win_chunks - 1), 0)
    wstart = pl.multiple_of(w0 * tq, tq)
    dist = tpos - (wstart + lax.broadcasted_iota(I32, (ww, 1), 0))
    wmask = jnp.where(jnp.logical_and(dist >= 0, dist <= WINDOW), 0.0, NEG)
    states = attend(wk_ref, wvb, wstart, ww, qi - w0, wmask, None)
    o_w = jnp.concatenate([acc[0:HEAD_DIM] / acc[HEAD_DIM:HEAD_DIM + 1] for _, acc in states], axis=1)

    o = gate(0) * o_c + gate(1) * o_s + gate(2) * o_w
    for j in range(4):
        o_ref[j * HEAD_DIM:(j + 1) * HEAD_DIM, :] = o[:, j * tq:(j + 1) * tq]


def nsa_prompt_attention_t(qk_arr, ck_t, cv_t, cs_t, wg_t, bg_col, cbias_t, toe_t, cover_t, expand_t, tq, tk):
    B, T, _ = qk_arr.shape
    nb = ck_t.shape[2]
    nd = toe_t.shape[1]
    n_selb = T // SEL_BLOCK
    win_chunks = WINDOW // tq + 1
    assert tq == LANES and T >= win_chunks * tq and tk % tq == 0 and T % tk == 0 and nb <= LANES
    kern = functools.partial(_nsa_t_kernel, tq=tq, tk=tk, n_blk=nb - 1, n_selb=n_selb, n_top=min(N_SEL, n_selb),
                             win_chunks=win_chunks, n_chains=1)
    return pl.pallas_call(
        kern,
        grid=(4, B, T // tq),
        in_specs=[pl.BlockSpec((None, tq, 2 * LANES), lambda g, b, i: (b, i, g)),
                  pl.BlockSpec((None, HEAD_DIM, nb), lambda g, b, i: (b, g, 0)),
                  pl.BlockSpec((None, HEAD_DIM, nb), lambda g, b, i: (b, g, 0)),
                  pl.BlockSpec((None, T, LANES), lambda g, b, i: (b, 0, 8 + g)),
                  pl.BlockSpec((None, HEAD_DIM, T), lambda g, b, i: (b, 12 + g, 0)),
                  pl.BlockSpec((None, T, LANES), lambda g, b, i: (b, 0, 12 + g)),
                  pl.BlockSpec((None, HEAD_DIM, T), lambda g, b, i: (b, 4 + g, 0)),
                  pl.BlockSpec((None, 16, tq), lambda g, b, i: (b, 32 + g, i)),
                  pl.BlockSpec((None, 16, 1), lambda g, b, i: (g, 0, 0)),
                  pl.BlockSpec((4, nb, tq), lambda g, b, i: (g, 0, i)),
                  pl.BlockSpec((4, nd, tq, tq), lambda g, b, i: (g, 0, 0, 0)),
                  pl.BlockSpec((LANES, nb), lambda g, b, i: (0, 0)),
                  pl.BlockSpec((T, LANES), lambda g, b, i: (0, 0))],
        out_specs=pl.BlockSpec((None, 4 * HEAD_DIM, tq), lambda g, b, i: (b, g, i)),
        out_shape=jax.ShapeDtypeStruct((B, 16 * HEAD_DIM, T), F32),
        scratch_shapes=[pltpu.VMEM((HEAD_DIM + 16, T), BF16), pltpu.VMEM((HEAD_DIM + 16, T), BF16),
                        pltpu.VMEM((nb, LANES), BF16), pltpu.VMEM((HEAD_DIM, nb), BF16)],
        compiler_params=_cparams(3),
        name="nsa_prompt_attention_t",
    )(qk_arr, ck_t, cv_t, qk_arr, cs_t, qk_arr, wg_t, wg_t, bg_col, cbias_t, toe_t, cover_t, expand_t)


def _out_res_t_kernel(a_ref, w_ref, g_ref, r_ref, o_ref):
    y = _dot(a_ref[...].T.astype(BF16), w_ref[...])
    o_ref[...] = r_ref[...] + _norm_rows(y, g_ref[...])


def out_proj_residual_t(a_t, w, g, res):
    B, K, T = a_t.shape
    D = w.shape[1]
    tm = min(T, 512)
    nt = T // tm
    return pl.pallas_call(
        _out_res_t_kernel,
        grid=(B, nt),
        in_specs=[pl.BlockSpec((None, K, tm), lambda b, i: (b, 0, i)),
                  pl.BlockSpec((K, D), lambda b, i: (0, 0)),
                  pl.BlockSpec((1, D), lambda b, i: (0, 0)),
                  pl.BlockSpec((tm, D), lambda b, i: (b * nt + i, 0))],
        out_specs=pl.BlockSpec((tm, D), lambda b, i: (b * nt + i, 0)),
        out_shape=jax.ShapeDtypeStruct((B * T, D), F32),
        compiler_params=_cparams(2),
        name="out_proj_residual_t",
    )(a_t, w, g.reshape(1, D), res)


def cover_matrix_np(n_blk, n_selb, rows, cols):
    i = np.arange(rows)[:, None]
    j = np.arange(cols)[None, :]
    m = (i * CMP_STRIDE < (j + 1) * SEL_BLOCK) & (i * CMP_STRIDE + CMP_BLOCK > j * SEL_BLOCK) & (i < n_blk) & (j < n_selb)
    return m.astype(np.float32)


def _even_dec_kernel(pt_ref, *refs, pps, n_steps, lam_init):
    kv_refs = refs[0:pps]
    lf_refs = refs[pps:2 * pps]
    (qt_ref, cn_ref, tbt_ref, ut_ref, kn_ref, vn_ref, bo_ref, lp_ref, sg_ref,
     o_ref, m_ref, l_ref, acc_ref, carry_ref) = refs[2 * pps:]
    s = pl.program_id(1)
    n_maps, width = qt_ref.shape
    n_fox = lf_refs[0].shape[0]

    @pl.when(s == 0)
    def _():
        m_ref[...] = jnp.full_like(m_ref, NEG)
        l_ref[...] = jnp.zeros_like(l_ref)
        acc_ref[...] = jnp.zeros_like(acc_ref)
        carry_ref[...] = jnp.zeros_like(carry_ref)

    qt = qt_ref[...]
    is_fox = lax.broadcasted_iota(I32, (n_maps, 1), 0) < n_fox
    pad = jnp.zeros((n_maps - n_fox, PAGE_SIZE), F32)
    lfts = jnp.concatenate([jnp.concatenate([lf_refs[i][...], pad], axis=0) for i in range(pps)], axis=0)
    within = _dot_exact_rhs01(lfts, ut_ref[...])
    totals = jnp.sum(lfts, axis=-1, keepdims=True)
    run = carry_ref[...]
    later = [None] * pps
    for i in reversed(range(pps)):
        later[i] = run
        run = run + totals[i * n_maps:(i + 1) * n_maps]
    carry_ref[...] = run
    cn = cn_ref[...]
    sts = []
    for i in range(pps):
        kpg = kv_refs[i][0].reshape(width, PAGE_SIZE).astype(BF16)
        suffix = within[i * n_maps:(i + 1) * n_maps] + later[i] + cn
        sts.append(_dot(qt, kpg) + jnp.where(is_fox, suffix, tbt_ref[i]))
    st = jnp.concatenate(sts, axis=1)
    m = m_ref[...]
    m_new = jnp.maximum(m, jnp.max(st, axis=-1, keepdims=True))
    alpha = jnp.exp(m - m_new)
    p = jnp.exp(st - m_new)
    l_ref[...] = alpha * l_ref[...] + jnp.sum(p, axis=-1, keepdims=True)
    pb = p.astype(BF16)
    pv = None
    for i in range(pps):
        vpg = kv_refs[i][1].reshape(width, PAGE_SIZE).astype(BF16)
        d = _dot_nt(pb[:, i * PAGE_SIZE:(i + 1) * PAGE_SIZE], vpg)
        pv = d if pv is None else pv + d
    acc_ref[...] = alpha * acc_ref[...] + pv
    m_ref[...] = m_new

    @pl.when(s == n_steps - 1)
    def _():
        kn = kn_ref[...].astype(BF16).astype(F32)
        vn = vn_ref[...].astype(BF16).astype(F32)
        s_own = jnp.sum(qt.astype(F32) * kn, axis=-1, keepdims=True) + bo_ref[...]
        m = m_ref[...]
        m_all = jnp.maximum(m, s_own)
        a = jnp.exp(m - m_all)
        e_own = jnp.exp(s_own - m_all)
        l_all = a * l_ref[...] + e_own
        o = (a * acc_ref[...] + e_own * vn) / l_all
        r = lax.broadcasted_iota(I32, (n_maps, width), 0)
        cidx = lax.broadcasted_iota(I32, (n_maps, width), 1)
        d_fox = n_fox * HEAD_DIM
        fox_sel = jnp.logical_and(cidx < d_fox, r == cidx // HEAD_DIM)
        dh = (cidx - d_fox) // LANES
        d1_sel = jnp.logical_and(cidx >= d_fox, r == n_fox + 2 * dh)
        d2_sel = jnp.logical_and(cidx >= d_fox, r == n_fox + 2 * dh + 1)
        o_f = jnp.sum(jnp.where(fox_sel, o, 0.0), axis=0, keepdims=True)
        a1 = jnp.sum(jnp.where(d1_sel, o, 0.0), axis=0, keepdims=True)
        a2 = jnp.sum(jnp.where(d2_sel, o, 0.0), axis=0, keepdims=True)
        lam = _lambda_value(lp_ref[...], lam_init)
        o_d = a1 - lam * a2
        for tl in range(width // LANES):
            sl = slice(tl * LANES, (tl + 1) * LANES)
            if tl * LANES < d_fox:
                o_ref[:, sl] = o_f[:, sl]
            else:
                x = o_d[:, sl]
                ms = jnp.mean(x * x, axis=-1, keepdims=True)
                o_ref[:, sl] = x * lax.rsqrt(ms + EPS) * sg_ref[...] * (1.0 - lam_init)


def even_decode(pool_t, lf_t, li, pt_flat, n_pages, qt, cn, tbt, kn, vn, bo, lam_p, subln_g, lam_init, pps):
    DB, n_maps, width = qt.shape
    n_fox = lf_t.shape[2]
    n_steps = n_pages // pps
    ut = jnp.asarray(np.tril(np.ones((PAGE_SIZE, PAGE_SIZE), np.float32), -1), BF16)

    def page(i):
        return lambda b, s, pt: pt[b * n_pages + (n_steps - 1 - s) * pps + i]

    kv_block = (None, None, 2, n_maps, HEAD_DIM, PAGE_SIZE)
    kv_specs = [pl.BlockSpec(kv_block, lambda b, s, pt, f=page(i): (li, f(b, s, pt), 0, 0, 0, 0)) for i in range(pps)]
    lf_specs = [pl.BlockSpec((None, None, n_fox, PAGE_SIZE), lambda b, s, pt, f=page(i): (li, f(b, s, pt), 0, 0)) for i in range(pps)]
    per_b = lambda shape: pl.BlockSpec((None,) + shape, lambda b, s, pt: (b,) + (0,) * len(shape))
    const = lambda shape: pl.BlockSpec(shape, lambda b, s, pt: (0,) * len(shape))
    grid_spec = pltpu.PrefetchScalarGridSpec(
        num_scalar_prefetch=1,
        grid=(DB, n_steps),
        in_specs=kv_specs + lf_specs + [
            per_b((n_maps, width)), per_b((n_maps, 1)),
            pl.BlockSpec((pps, n_maps, PAGE_SIZE), lambda b, s, pt: (n_steps - 1 - s, 0, 0)),
            const((PAGE_SIZE, PAGE_SIZE)),
            per_b((1, width)), per_b((1, width)),
            const((n_maps, 1)), const((4, HEAD_DIM)), const((1, LANES))],
        out_specs=per_b((1, width)),
        scratch_shapes=[pltpu.VMEM((n_maps, 1), F32), pltpu.VMEM((n_maps, 1), F32), pltpu.VMEM((n_maps, width), F32),
                        pltpu.VMEM((n_maps, 1), F32)])
    kern = functools.partial(_even_dec_kernel, pps=pps, n_steps=n_steps, lam_init=lam_init)
    return pl.pallas_call(
        kern, grid_spec=grid_spec,
        out_shape=jax.ShapeDtypeStruct((DB, 1, width), F32),
        compiler_params=_cparams(2),
        name="even_decode",
    )(pt_flat, *([pool_t] * pps), *([lf_t] * pps), qt, cn, tbt, ut, kn, vn, bo, lam_p, subln_g.reshape(1, LANES))


def _odd_cmp_kernel(pt_ref, *refs, pps, n_steps, n_selb, n_top, t_pos):
    r_refs = refs[0:pps]
    (w0_ref, w1_ref, g_ref, b_ref, w0t_ref, w1t_ref, qz_ref, cb_ref, cov_ref, oc_ref, idx_ref, p0, p1) = refs[pps:]
    s = pl.program_id(1)
    width = w0_ref.shape[0]
    w0 = w0_ref[...]
    w1 = w1_ref[...]
    acc = None
    for i in range(pps):
        x = r_refs[i][...].reshape(width, PAGE_SIZE)
        a = _pool2(x, w0, w1, g_ref[i], False)
        acc = a if acc is None else acc + a
    col = pl.multiple_of(s * LANES, LANES)
    p0[:, pl.ds(col, LANES)] = acc[:, 0:LANES]
    p1[:, pl.ds(col, LANES)] = acc[:, LANES:]

    @pl.when(s == n_steps - 1)
    def _():
        nb = p0.shape[1]
        ck, cv = _cmp_finish_t(p0[...], p1[...], b_ref[...], w0t_ref[...], w1t_ref[...])
        st = _dot(qz_ref[...], ck.astype(BF16)) + cb_ref[...]
        n_idx = lax.broadcasted_iota(I32, (1, nb), 1)
        valid = n_idx * CMP_STRIDE + (CMP_BLOCK - 1) <= t_pos
        sm = jnp.where(valid, st, NEG)
        m = jnp.max(sm, axis=-1, keepdims=True)
        e = jnp.where(valid, jnp.exp(sm - m), 0.0)
        l = jnp.sum(e, axis=-1, keepdims=True)
        p = e / jnp.where(l > 0.0, l, 1.0)
        oc_ref[...] = _dot_nt(p.astype(BF16), cv.astype(BF16))
        gqa = p.shape[0] // 4
        rows = [jnp.sum(p[g * gqa:(g + 1) * gqa], axis=0, keepdims=True) for g in range(4)]
        psum = jnp.concatenate(rows + rows, axis=0)
        score = _dot_exact_rhs01(psum, cov_ref[...])
        nsp = score.shape[1]
        jb = lax.broadcasted_iota(I32, (1, nsp), 1)
        cur = t_pos // SEL_BLOCK
        forced = jnp.logical_or(jb == 0, jnp.logical_or(jb == cur, jb == cur - 1))
        score = jnp.where(forced, FORCE, score)
        score = jnp.where(jb * SEL_BLOCK <= t_pos, score, NEG)
        score = jnp.where(jb < n_selb, score, PAD_SCORE)
        jf = jb.astype(F32)
        slot = lax.broadcasted_iota(I32, (1, LANES), 1)
        picks = jnp.zeros((8, LANES), F32)
        for it in range(n_top):
            mx = jnp.max(score, axis=-1, keepdims=True)
            ix = jnp.min(jnp.where(score == mx, jf, 1e9), axis=-1, keepdims=True)
            picks = jnp.where(slot == it, ix, picks)
            score = jnp.where(jf == ix, PAD_SCORE, score)
        idx_ref[...] = picks.astype(I32)


def odd_decode_cmp(pool_t, li, pt_flat, n_pages, w0, w1, b_col, w0t, w1t, qz16, cbias, cover, n_selb, t_pos):
    DB = qz16.shape[0]
    width = w0.shape[0]
    half = width // 2
    pps = LANES // (PAGE_SIZE // CMP_STRIDE)
    assert n_pages % pps == 0
    n_steps = n_pages // pps
    nb = n_pages * PAGE_SIZE // CMP_STRIDE
    nsp = cover.shape[1]
    spp = PAGE_SIZE // CMP_STRIDE
    gstack = jnp.asarray(_group_sum_stack(pps), BF16)
    r_specs = [pl.BlockSpec((None, None, 2, 4, HEAD_DIM, PAGE_SIZE),
                            lambda b, s, pt, i=i: (li, pt[b * n_pages + s * pps + i], 0, 0, 0, 0)) for i in range(pps)]
    per_b = lambda shape: pl.BlockSpec((None,) + shape, lambda b, s, pt: (b,) + (0,) * len(shape))
    const = lambda shape: pl.BlockSpec(shape, lambda b, s, pt: (0,) * len(shape))
    grid_spec = pltpu.PrefetchScalarGridSpec(
        num_scalar_prefetch=1,
        grid=(DB, n_steps),
        in_specs=r_specs + [const((width, LANES)), const((width, LANES)), const((pps, 2 * LANES, 2 * LANES)), const((width, 1)),
                            const((half, half)), const((half, half)),
                            per_b((16, half)), const((16, nb)), const((nb, nsp))],
        out_specs=[per_b((16, half)), per_b((8, LANES))],
        scratch_shapes=[pltpu.VMEM((width, nb), F32), pltpu.VMEM((width, nb), F32)])
    kern = functools.partial(_odd_cmp_kernel, pps=pps, n_steps=n_steps, n_selb=n_selb, n_top=min(N_SEL, n_selb), t_pos=t_pos)
    return pl.pallas_call(
        kern, grid_spec=grid_spec,
        out_shape=[jax.ShapeDtypeStruct((DB, 16, half), F32), jax.ShapeDtypeStruct((DB, 8, LANES), I32)],
        compiler_params=_cparams(2),
        name="odd_decode_cmp",
    )(pt_flat, *([pool_t] * pps), w0, w1, gstack, b_col, w0t, w1t, qz16, cbias, cover)


def _odd_sel_kernel(idx_ref, pt_ref, *refs, n_top, n_past_blk, bpp, t_pos):
    kv_refs = refs[0:n_top]
    (q_ref, sn_ref, tb_ref, wkv_ref, wn_ref, tw_ref, tw0_ref, oc_ref, gt_ref, o_ref) = refs[n_top:]
    b = pl.program_id(0)
    g = pl.program_id(1)
    q = q_ref[...]
    qf = q.astype(F32)
    lane = lax.broadcasted_iota(I32, (1, PAGE_SIZE), 1)
    bias0 = tw0_ref[:, 0:1]

    logits, vals = [], []
    n_new = jnp.zeros((), I32)
    for k in range(n_top):
        blk = idx_ref[(b * 4 + g) * n_top + k]
        is_new = blk >= n_past_blk
        n_new = n_new + is_new.astype(I32)
        page = jnp.minimum(blk // bpp, n_past_blk // bpp - 1)
        bias = tb_ref[pl.ds(pl.multiple_of((page * 4 + g) * 8, 8), 8), :]
        s = _dot(q, kv_refs[k][0].astype(BF16)) + bias
        pos = page * PAGE_SIZE + lane
        valid = jnp.logical_and(jnp.logical_and(pos // SEL_BLOCK == blk, pos <= t_pos), jnp.logical_not(is_new))
        logits.append(jnp.where(valid, s, NEG))
        vals.append(kv_refs[k][1].astype(BF16))
    has_new = n_new > 0
    kn = sn_ref[0:1, :].astype(BF16).astype(F32)
    vn = sn_ref[1:2, :].astype(BF16).astype(F32)
    s_new = jnp.where(has_new, jnp.sum(qf * kn, axis=-1, keepdims=True) + bias0, NEG)
    m = s_new
    for s in logits:
        m = jnp.maximum(m, s.max(axis=-1, keepdims=True))
    p_new = jnp.where(has_new, jnp.exp(s_new - m), 0.0)
    l = p_new
    acc = p_new * vn
    for s, vt in zip(logits, vals):
        p = jnp.where(s > 0.5 * NEG, jnp.exp(s - m), 0.0)
        l = l + jnp.sum(p, axis=-1, keepdims=True)
        acc = acc + _dot_nt(p.astype(BF16), vt)
    o_s = acc / jnp.where(l > 0.0, l, 1.0)

    sw = _dot(q, wkv_ref[0].astype(BF16)) + tw_ref[...]
    kwn = wn_ref[0:1, :].astype(BF16).astype(F32)
    vwn = wn_ref[1:2, :].astype(BF16).astype(F32)
    sw_new = jnp.sum(qf * kwn, axis=-1, keepdims=True) + bias0
    mw = jnp.maximum(jnp.max(sw, axis=-1, keepdims=True), sw_new)
    pw = jnp.exp(sw - mw)
    pw_new = jnp.exp(sw_new - mw)
    lw = jnp.sum(pw, axis=-1, keepdims=True) + pw_new
    o_w = (_dot_nt(pw.astype(BF16), wkv_ref[1].astype(BF16)) + pw_new * vwn) / lw

    gates = 1.0 / (1.0 + jnp.exp(-gt_ref[...]))
    o_ref[...] = gates[:, 0:1] * oc_ref[...] + gates[:, 1:2] * o_s + gates[:, 2:3] * o_w


def odd_decode_sel(pool_t, li, idx_flat, pt_flat, n_pages, win_t, qg, s_new, tbp, w_new, tw, tw0, oc_g, graw, n_top, t_pos):
    DB = qg.shape[0]
    bpp = PAGE_SIZE // SEL_BLOCK
    wb = win_t.shape[-1]
    n_past_blk = n_pages * bpp

    def blk_spec(k):
        def imap(b, g, idx, pt):
            blk = idx[(b * 4 + g) * n_top + k]
            page = pt[b * n_pages + jnp.minimum(blk // bpp, n_pages - 1)]
            return (li, page, 1, g, 0, 0)
        return pl.BlockSpec((None, None, 2, None, HEAD_DIM, PAGE_SIZE), imap)

    per_bg = lambda shape: pl.BlockSpec((None, None) + shape, lambda b, g, idx, pt: (b, g) + (0,) * len(shape))
    grid_spec = pltpu.PrefetchScalarGridSpec(
        num_scalar_prefetch=2,
        grid=(DB, 4),
        in_specs=[blk_spec(k) for k in range(n_top)] + [
            per_bg((8, HEAD_DIM)), per_bg((2, HEAD_DIM)),
            pl.BlockSpec(tbp.shape, lambda b, g, idx, pt: (0, 0)),
            pl.BlockSpec((None, None, 2, None, HEAD_DIM, wb), lambda b, g, idx, pt: (li, b, 0, g, 0, 0)),
            per_bg((2, HEAD_DIM)),
            pl.BlockSpec((8, wb), lambda b, g, idx, pt: (g, 0)),
            pl.BlockSpec((8, LANES), lambda b, g, idx, pt: (g, 0)),
            per_bg((8, HEAD_DIM)), per_bg((8, LANES))],
        out_specs=per_bg((8, HEAD_DIM)))
    kern = functools.partial(_odd_sel_kernel, n_top=n_top, n_past_blk=n_past_blk, bpp=bpp, t_pos=t_pos)
    return pl.pallas_call(
        kern, grid_spec=grid_spec,
        out_shape=jax.ShapeDtypeStruct((DB, 4, 8, HEAD_DIM), F32),
        compiler_params=_cparams(2),
        name="odd_decode_sel",
    )(idx_flat, pt_flat, *([pool_t] * n_top), qg, s_new, tbp, win_t, w_new, tw, tw0, oc_g, graw)


def _pad_cols(w, n):
    return jnp.pad(w, ((0, 0), (0, n - w.shape[1])))


def _pad_rows8(x4):
    pad = [(0, 0)] * x4.ndim
    pad[-2] = (0, 4)
    return jnp.pad(x4, pad)


def even_layer(hp, hs, pool_kv, pool_lf, li, page_table, g0, w_in, b_f, lam_p, subln_g, rel_table, layer):
    B, T, D = hp.shape
    DB, S, _ = hs.shape
    assert S == 1
    n_fox, d_fox = 8, 512
    lam_init = 0.8 - 0.6 * math.exp(-0.3 * layer)
    cuts = np.cumsum((d_fox,) * 6)
    q_f, k_f, v_f, q_d, k_d, v_d, f_w = jnp.split(w_in, [int(c) for c in cuts], axis=1)
    w_q = jnp.concatenate([q_f, q_d, _pad_cols(f_w, LANES)], axis=1).astype(BF16)
    w_kv = jnp.concatenate([k_f, k_d, v_f, v_d], axis=1).astype(BF16)
    bf_pad = jnp.pad(b_f.astype(F32), (0, LANES - n_fox)).reshape(1, LANES)
    table4 = rel_table[:, :4]

    q_arr = norm_proj(hp.reshape(B * T, D), g0, w_q, 384).reshape(B, T, -1)
    kv_t = norm_proj_t(hp, g0, w_kv.T, 512)
    tq = min(T, 256)
    tk = min(T, 512)
    lf_pad, cum_pad = logf_cumsum(q_arr, 8, bf_pad, tq)
    cum_pairs = cum_pad[:, :, :n_fox].transpose(0, 2, 1).reshape(B, 4, 2, T)
    toe_t = LOG2E * bias_table(table4, toeplitz_dist(tq, T // tq).transpose(0, 2, 1))
    attn_p = even_prompt_attention_t(q_arr, kv_t, cum_pairs, toe_t, lam_p, subln_g, lam_init, tq, tk)
    ekv_p = kv_t.reshape(B, 2, 16, HEAD_DIM, T).transpose(0, 4, 1, 2, 3)
    elf_p = lf_pad[:, :, :n_fox]

    n_pages = page_table.shape[1]
    past = n_pages * PAGE_SIZE
    xs = hs.reshape(DB, D)
    qs_arr = norm_proj(xs, g0, w_q, 384)
    kvs_arr = norm_proj(xs, g0, w_kv, 512)
    lfs_pad, _ = logf_cumsum(qs_arr.reshape(1, DB, -1), 8, bf_pad, DB)
    lf_new = lfs_pad[0, :, :n_fox]
    seg = np.zeros((16, 1024), np.float32)
    for m_ in range(16):
        seg[m_, m_ * HEAD_DIM:(m_ + 1) * HEAD_DIM] = 1.0
    qt = (qs_arr[:, None, :1024] * SCALE * seg[None]).astype(BF16)
    cn = jnp.pad(lf_new, ((0, 0), (0, 8)))[:, :, None]
    pos = jnp.arange(past).reshape(n_pages, PAGE_SIZE)
    rb = jnp.repeat(bias_table(table4, past - pos), 2, axis=0)
    tbt = jnp.concatenate([jnp.zeros((n_pages, 8, PAGE_SIZE), F32), rb.transpose(1, 0, 2)], axis=1)
    bo = jnp.concatenate([jnp.zeros((8,), F32), jnp.repeat(rel_table[0, :4], 2)]).reshape(16, 1)
    pool_t = pool_kv.transpose(0, 1, 3, 4, 5, 2)
    lf_t = pool_lf.transpose(0, 1, 3, 2)
    attn_s = even_decode(pool_t, lf_t, li, page_table.reshape(-1), n_pages, qt, cn, tbt,
                         kvs_arr[:, None, :1024], kvs_arr[:, None, 1024:], bo, lam_p, subln_g, lam_init,
                         pps=math.gcd(n_pages, 16))
    ekv_s = kvs_arr.reshape(DB, 1, 2, 16, HEAD_DIM)
    elf_s = lf_new.reshape(DB, 1, n_fox)
    return attn_p, attn_s.reshape(DB, -1), ekv_p, ekv_s, elf_p, elf_s


def odd_layer(hp, hs, pool, win_state, li, page_table, g0, w_in, b_gate, w_pos, b_cmp, w_cout, rel_table):
    B, T, D = hp.shape
    DB, S, _ = hs.shape
    assert S == 1
    d_q, d_kv = 1024, 512
    w_q, w_c, w_s, w_w, w_g = jnp.split(w_in, [d_q, d_q + d_kv, d_q + 2 * d_kv, d_q + 3 * d_kv], axis=1)
    n_gate = 12
    w_g4 = jnp.concatenate([_pad_cols(w_g[:, g * n_gate:(g + 1) * n_gate], LANES) for g in range(4)], axis=1)
    w_qg = jnp.concatenate([w_q, w_g4], axis=1).astype(BF16)
    w_cs = jnp.concatenate([w_c, w_s], axis=1).astype(BF16)
    w_w16 = w_w.astype(BF16)
    bg = b_gate.astype(F32)
    bg_pad = jnp.pad(bg.reshape(4, 1, n_gate), ((0, 0), (0, 0), (0, LANES - n_gate)))
    wp = w_pos.reshape(CMP_BLOCK, d_kv)
    w0, w1 = _pool_weights_t(wp)
    b_col = b_cmp.reshape(d_kv, 1).astype(F32)
    eye4 = jnp.eye(4, dtype=F32)
    w0t = jnp.kron(eye4, w_cout[0].T).astype(BF16)
    w1t = jnp.kron(eye4, w_cout[1].T).astype(BF16)
    table = rel_table

    d_g = d_kv // 4
    dup = lambda w: jnp.concatenate([w[:, g * HEAD_DIM:(g + 1) * HEAD_DIM] for g in range(4) for _ in range(2)], axis=1)
    w_qk = jnp.concatenate([w_q * (SCALE * LOG2E), dup(w_s[:, :2 * d_g]), dup(w_w[:, :2 * d_g])], axis=1).astype(BF16)
    w_g16 = jnp.concatenate([_pad_cols(w_g[:, g * n_gate:(g + 1) * n_gate], 16) for g in range(4)], axis=1)
    w_wg_t = jnp.concatenate([w_w, w_g16], axis=1).T.astype(BF16)
    bg_col = jnp.pad(bg.reshape(4, n_gate, 1), ((0, 0), (0, 16 - n_gate), (0, 0)))
    qk_arr = norm_proj(hp.reshape(B * T, D), g0, w_qk, 512, BF16).reshape(B, T, -1)
    cs_t = norm_proj_t(hp, g0, w_cs.T, 512)
    wg_t = norm_proj_t(hp, g0, w_wg_t, w_wg_t.shape[0])
    ck_t, cv_t = cmp_build(cs_t, w0, w1, b_col, w0t, w1t)
    nb = T // CMP_STRIDE
    n_blk = nb - 1
    n_selb = T // SEL_BLOCK
    tq = LANES
    tk = min(T, 512)
    t_all = jnp.arange(T)
    e_pos = jnp.arange(nb) * CMP_STRIDE + CMP_BLOCK - 1
    cbias_t = LOG2E * bias_table(table, t_all[None, :] - e_pos[:, None])
    toe_t = LOG2E * bias_table(table, toeplitz_dist(tq, T // tq).transpose(0, 2, 1))
    cover_t = jnp.asarray(cover_matrix_np(n_blk, n_selb, nb, LANES).T, BF16)
    expand_t = np.zeros((T, LANES), np.float32)
    expand_t[np.arange(T), np.arange(T) // SEL_BLOCK] = 1.0
    attn_p = nsa_prompt_attention_t(qk_arr, ck_t, cv_t, cs_t, wg_t, bg_col, cbias_t, toe_t, cover_t,
                                    jnp.asarray(expand_t, BF16), tq, tk)
    okv_p = cs_t.reshape(B, 4, 4, HEAD_DIM, T).transpose(0, 4, 1, 2, 3)
    wn = min(WINDOW, T)
    win_p = wg_t[:, :d_kv, T - wn:].reshape(B, 2, 4, HEAD_DIM, wn).transpose(0, 4, 1, 2, 3)

    n_pages = page_table.shape[1]
    past = n_pages * PAGE_SIZE
    xs = hs.reshape(DB, D)
    qgs = norm_proj(xs, g0, w_qg, 512)
    css = norm_proj(xs, g0, w_cs, 512)
    wns = norm_proj(xs, g0, w_w16, 512)
    pt_flat = page_table.reshape(-1)
    nbs = past // CMP_STRIDE
    n_past_blk = past // SEL_BLOCK
    n_selb_s = n_past_blk + 1
    nsp = -(-n_selb_s // LANES) * LANES
    qs = qgs[:, :d_q]
    segz = np.zeros((16, 256), np.float32)
    for h_ in range(16):
        segz[h_, (h_ // 4) * HEAD_DIM:(h_ // 4 + 1) * HEAD_DIM] = 1.0
    qz16 = (jnp.tile(qs.reshape(DB, 16, 1, HEAD_DIM), (1, 1, 4, 1)).reshape(DB, 16, 256) * SCALE * segz[None]).astype(BF16)
    e_pos_s = jnp.arange(nbs) * CMP_STRIDE + CMP_BLOCK - 1
    cbias_s = bias_table(table, past - e_pos_s)
    cover_s = jnp.asarray(cover_matrix_np(nbs, n_selb_s, nbs, nsp), BF16)
    pool_t = pool.transpose(0, 1, 3, 4, 5, 2)
    n_top = min(N_SEL, n_selb_s)
    oc16, idx8 = odd_decode_cmp(pool_t, li, pt_flat, n_pages, w0, w1, b_col, w0t, w1t, qz16, cbias_s, cover_s,
                                n_selb_s, past)
    idx_flat = idx8[:, :4, :n_top].reshape(-1)
    oc4 = oc16.reshape(DB, 4, 4, 4, HEAD_DIM)[:, np.arange(4), :, np.arange(4)].transpose(1, 0, 2, 3)
    qg = _pad_rows8(qs.reshape(DB, 4, 4, HEAD_DIM) * SCALE).astype(BF16)
    pos_p = jnp.arange(past).reshape(n_pages, PAGE_SIZE)
    tbp = bias_table(table, past - pos_p).reshape(4, 4, n_pages, PAGE_SIZE).transpose(2, 0, 1, 3)
    tbp = _pad_rows8(tbp).reshape(n_pages * 32, PAGE_SIZE)
    wb = win_state.shape[2]
    tw = _pad_rows8(bias_table(table, wb - jnp.arange(wb)).reshape(4, 4, wb)).reshape(32, wb)
    tw0 = _pad_rows8(jnp.broadcast_to(table[0].reshape(4, 4, 1), (4, 4, LANES))).reshape(32, LANES)
    g48 = jnp.concatenate([qgs[:, d_q + g * LANES:d_q + g * LANES + n_gate] for g in range(4)], axis=1) + bg[None]
    graw = jnp.pad(_pad_rows8(g48.reshape(DB, 4, 4, 3)), ((0, 0), (0, 0), (0, 0), (0, LANES - 3)))
    win_t = win_state.transpose(0, 1, 3, 4, 5, 2)
    s_new = css[:, d_kv:].reshape(DB, 2, 4, HEAD_DIM).transpose(0, 2, 1, 3)
    w_new = wns.reshape(DB, 2, 4, HEAD_DIM).transpose(0, 2, 1, 3)
    o4 = odd_decode_sel(pool_t, li, idx_flat, pt_flat, n_pages, win_t, qg, s_new, tbp, w_new, tw, tw0,
                        _pad_rows8(oc4), graw, n_top, past)
    attn_s = o4[:, :, :4, :].reshape(DB, d_q)
    okv_s = css.reshape(DB, 1, 4, 4, HEAD_DIM)
    keys_t = jnp.concatenate([win_t[li], wns.reshape(DB, 2, 4, HEAD_DIM, 1)], axis=-1)
    keep = min(WINDOW, wb + 1)
    win_s = keys_t[..., wb + 1 - keep:].transpose(0, 4, 1, 2, 3)
    return attn_p, attn_s, okv_p, okv_s, win_p, win_s


def kernel(x_prompt, x_sample, cache_even_kv, cache_even_logf, cache_odd_kv, state_odd_win, page_table, rel_table, norm_g, w_even_in, b_even_f, diff_lambda, diff_subln_g, w_even_out, w_odd_in, b_odd_gate, w_cmp_pos, b_cmp, w_cmp_out, w_odd_out, w_ffn_in, w_ffn_out):
    B, T, D = x_prompt.shape
    DB, S, _ = x_sample.shape
    depth = norm_g.shape[0]
    hp = x_prompt.reshape(B * T, D)
    hs = x_sample.reshape(DB * S, D)
    ekv_p, ekv_s, elf_p, elf_s = [], [], [], []
    okv_p, okv_s, win_p, win_s = [], [], [], []
    for layer in range(depth):
        g = norm_g[layer]
        li = layer // 2
        hp3, hs3 = hp.reshape(B, T, D), hs.reshape(DB, S, D)
        if layer % 2 == 0:
            a_p, a_s, kvp, kvs, lfp, lfs = even_layer(
                hp3, hs3, cache_even_kv, cache_even_logf, li, page_table, g[0], w_even_in[li], b_even_f[li],
                diff_lambda[li].astype(F32), diff_subln_g[li].astype(F32), rel_table, layer)
            w_o = w_even_out[li].astype(BF16)
            ekv_p.append(kvp); ekv_s.append(kvs); elf_p.append(lfp); elf_s.append(lfs)
        else:
            a_p, a_s, kvp, kvs, wp_, ws_ = odd_layer(
                hp3, hs3, cache_odd_kv, state_odd_win, li, page_table, g[0], w_odd_in[li], b_odd_gate[li],
                w_cmp_pos[li], b_cmp[li], w_cmp_out[li], rel_table)
            w_o = w_odd_out[li].astype(BF16)
            okv_p.append(kvp); okv_s.append(kvs); win_p.append(wp_); win_s.append(ws_)
        hp = (out_proj_residual if a_p.ndim == 2 else out_proj_residual_t)(a_p, w_o, g[1], hp)
        hs = out_proj_residual(a_s, w_o, g[1], hs)
        w1, w2 = ffn_weights(w_ffn_in[layer], w_ffn_out[layer], 256)
        hp = ffn_residual(hp, g[2], w1, w2, g[3])
        hs = ffn_residual(hs, g[2], w1, w2, g[3])
    return (hp.reshape(B, T, D), hs.reshape(DB, S, D), jnp.stack(ekv_p), jnp.stack(ekv_s), jnp.stack(elf_p), jnp.stack(elf_s),
            jnp.stack(okv_p), jnp.stack(okv_s), jnp.stack(win_p), jnp.stack(win_s))
```

```python
import functools
import math

import numpy as np
import jax
import jax.numpy as jnp
from jax import lax
from jax.experimental import pallas as pl
from jax.experimental.pallas import tpu as pltpu

F32 = jnp.float32
BF16 = jnp.bfloat16
I32 = jnp.int32

HEAD_DIM = 64
LANES = 128
PAGE_SIZE = 128
CMP_BLOCK = 32
CMP_STRIDE = 16
SEL_BLOCK = 64
N_SEL = 16
WINDOW = 512
N_BUCKETS = 32
MAX_DISTANCE = 1024
SCALE = HEAD_DIM ** -0.5
LOG2E = math.log2(math.e)
EPS = 1e-6
NEG = -1e30
FORCE = 1e9
PAD_SCORE = -3.0e38
VMEM_LIMIT = 56 * 1024 * 1024


def _cparams(n_axes):
    return pltpu.CompilerParams(dimension_semantics=("arbitrary",) * n_axes,
                                vmem_limit_bytes=VMEM_LIMIT)


def _dot(a, b):
    return jnp.dot(a, b, preferred_element_type=F32)


def _dot_nt(a, b):
    return lax.dot_general(a, b, (((1,), (1,)), ((), ())), preferred_element_type=F32)


def _split3(x):
    hi = x.astype(BF16)
    r1 = x - hi.astype(F32)
    mid = r1.astype(BF16)
    lo = (r1 - mid.astype(F32)).astype(BF16)
    return hi, mid, lo


def _dot_exact_rhs01(x, m01):
    hi, mid, lo = _split3(x)
    return _dot(hi, m01) + _dot(mid, m01) + _dot(lo, m01)


def _dot_exact_lhs01(m01, x):
    hi, mid, lo = _split3(x)
    return _dot(m01, hi) + _dot(m01, mid) + _dot(m01, lo)


def _lane_tile(x, n):
    return x if n == 1 else jnp.concatenate([x] * n, axis=1)


def _aligned(x, m):
    return x if isinstance(x, int) else pl.multiple_of(x, m)


def _shift_left_one_lane(x):
    n = x.shape[1] // LANES
    if n == 0:
        return pltpu.roll(x, x.shape[1] - 1, 1)
    keep = lax.broadcasted_iota(I32, (1, LANES), 1) < LANES - 1
    rolled = [pltpu.roll(x[:, c * LANES:(c + 1) * LANES], LANES - 1, 1) for c in range(n)]
    out = [jnp.where(keep, rolled[c], rolled[min(c + 1, n - 1)]) for c in range(n)]
    return out[0] if n == 1 else jnp.concatenate(out, axis=1)


def _silu(x):
    return x * (1.0 / (1.0 + jnp.exp(-x)))


def rel_bucket(dist):
    n = jnp.maximum(dist, 0)
    exact = N_BUCKETS // 2
    nf = jnp.maximum(n, 1).astype(F32)
    large = exact + (jnp.log(nf / exact) / math.log(MAX_DISTANCE / exact) * (N_BUCKETS - exact)).astype(I32)
    return jnp.where(n < exact, n, jnp.minimum(large, N_BUCKETS - 1))


def bias_table(table, dist):
    onehot = (rel_bucket(dist)[..., None] == jnp.arange(N_BUCKETS)).astype(F32)
    return jnp.einsum('...k,kh->h...', onehot, table.astype(F32), precision=lax.Precision.HIGHEST)


def _norm_rows(x, g):
    ms = jnp.mean(x * x, axis=-1, keepdims=True)
    return x * lax.rsqrt(ms + EPS) * g


def _proj_kernel(x_ref, g_ref, w_ref, o_ref, h_ref):
    @pl.when(pl.program_id(1) == 0)
    def _():
        h_ref[...] = _norm_rows(x_ref[...], g_ref[...]).astype(BF16)

    o_ref[...] = _dot(h_ref[...], w_ref[...]).astype(o_ref.dtype)


def norm_proj(x, g, w, tn, out_dtype=F32):
    M, D = x.shape
    N = w.shape[1]
    tm = min(M, 1024)
    return pl.pallas_call(
        _proj_kernel,
        grid=(M // tm, N // tn),
        in_specs=[pl.BlockSpec((tm, D), lambda i, j: (i, 0)),
                  pl.BlockSpec((1, D), lambda i, j: (0, 0)),
                  pl.BlockSpec((D, tn), lambda i, j: (0, j))],
        out_specs=pl.BlockSpec((tm, tn), lambda i, j: (i, j)),
        out_shape=jax.ShapeDtypeStruct((M, N), out_dtype),
        scratch_shapes=[pltpu.VMEM((tm, D), BF16)],
        compiler_params=_cparams(2),
        name="norm_proj",
    )(x, g.reshape(1, D), w)


def _proj_t_kernel(x_ref, g_ref, w_ref, o_ref, h_ref):
    @pl.when(pl.program_id(2) == 0)
    def _():
        h_ref[...] = _norm_rows(x_ref[...], g_ref[...]).astype(BF16)

    o_ref[...] = _dot_nt(w_ref[...], h_ref[...])


def norm_proj_t(x, g, w_t, tn):
    B, T, D = x.shape
    N = w_t.shape[0]
    tm = min(T, 1024)
    return pl.pallas_call(
        _proj_t_kernel,
        grid=(B, T // tm, N // tn),
        in_specs=[pl.BlockSpec((None, tm, D), lambda b, i, j: (b, i, 0)),
                  pl.BlockSpec((1, D), lambda b, i, j: (0, 0)),
                  pl.BlockSpec((tn, D), lambda b, i, j: (j, 0))],
        out_specs=pl.BlockSpec((None, tn, tm), lambda b, i, j: (b, j, i)),
        out_shape=jax.ShapeDtypeStruct((B, N, T), F32),
        scratch_shapes=[pltpu.VMEM((tm, D), BF16)],
        compiler_params=_cparams(3),
        name="norm_proj_t",
    )(x, g.reshape(1, D), w_t)


def _out_res_kernel(a_ref, w_ref, g_ref, r_ref, o_ref):
    y = _dot(a_ref[...].astype(BF16), w_ref[...])
    o_ref[...] = r_ref[...] + _norm_rows(y, g_ref[...])


def out_proj_residual(a, w, g, res):
    M, K = a.shape
    D = w.shape[1]
    tm = min(M, 512)
    return pl.pallas_call(
        _out_res_kernel,
        grid=(M // tm,),
        in_specs=[pl.BlockSpec((tm, K), lambda i: (i, 0)),
                  pl.BlockSpec((K, D), lambda i: (0, 0)),
                  pl.BlockSpec((1, D), lambda i: (0, 0)),
                  pl.BlockSpec((tm, D), lambda i: (i, 0))],
        out_specs=pl.BlockSpec((tm, D), lambda i: (i, 0)),
        out_shape=jax.ShapeDtypeStruct((M, D), F32),
        compiler_params=_cparams(1),
        name="out_proj_residual",
    )(a, w, g.reshape(1, D), res)


def _ffn_kernel(x_ref, g2_ref, wgu_ref, wo_ref, g3_ref, o_ref, h_ref, acc_ref):
    f = pl.program_id(1)
    tf = wo_ref.shape[0]

    @pl.when(f == 0)
    def _():
        h_ref[...] = _norm_rows(x_ref[...], g2_ref[...]).astype(BF16)
        acc_ref[...] = jnp.zeros_like(acc_ref)

    gu = _dot(h_ref[...], wgu_ref[...])
    acc_ref[...] += _dot((_silu(gu[:, :tf]) * gu[:, tf:]).astype(BF16), wo_ref[...])

    @pl.when(f == pl.num_programs(1) - 1)
    def _():
        o_ref[...] = x_ref[...] + _norm_rows(acc_ref[...], g3_ref[...])


def ffn_weights(w_in, w_out, tf):
    D = w_in.shape[0]
    Fdim = w_out.shape[0]
    nf = Fdim // tf
    gate = w_in[:, :Fdim].reshape(D, nf, tf)
    up = w_in[:, Fdim:].reshape(D, nf, tf)
    return jnp.concatenate([gate, up], axis=2).transpose(1, 0, 2).astype(BF16), w_out.astype(BF16)


def ffn_residual(x, g2, w_gu, w_out, g3):
    M, D = x.shape
    nf, _, tf2 = w_gu.shape
    tf = tf2 // 2
    tm = min(M, 1024)
    return pl.pallas_call(
        _ffn_kernel,
        grid=(M // tm, nf),
        in_specs=[pl.BlockSpec((tm, D), lambda i, f: (i, 0)),
                  pl.BlockSpec((1, D), lambda i, f: (0, 0)),
                  pl.BlockSpec((None, D, tf2), lambda i, f: (f, 0, 0)),
                  pl.BlockSpec((tf, D), lambda i, f: (f, 0)),
                  pl.BlockSpec((1, D), lambda i, f: (0, 0))],
        out_specs=pl.BlockSpec((tm, D), lambda i, f: (i, 0)),
        out_shape=jax.ShapeDtypeStruct((M, D), F32),
        scratch_shapes=[pltpu.VMEM((tm, D), BF16), pltpu.VMEM((tm, D), F32)],
        compiler_params=_cparams(2),
        name="ffn_residual",
    )(x, g2.reshape(1, D), w_gu, w_out, g3.reshape(1, D))


def _logf_kernel(f_ref, b_ref, tri_ref, lf_ref, cum_ref, carry_ref):
    @pl.when(pl.program_id(1) == 0)
    def _():
        carry_ref[...] = jnp.zeros_like(carry_ref)

    x = f_ref[...] + b_ref[...]
    lf = jnp.minimum(x, 0.0) - jnp.log(1.0 + jnp.exp(-jnp.abs(x)))
    lf_ref[...] = lf
    cum = _dot_exact_lhs01(tri_ref[...], lf) + carry_ref[...]
    cum_ref[...] = cum
    carry_ref[...] = cum[cum.shape[0] - 1:, :]


def logf_cumsum(fq, col_block, b_pad, tq):
    B, T, _ = fq.shape
    tri = jnp.asarray(np.tril(np.ones((tq, tq), np.float32)), BF16)
    return pl.pallas_call(
        _logf_kernel,
        grid=(B, T // tq),
        in_specs=[pl.BlockSpec((None, tq, LANES), lambda b, i: (b, i, col_block)),
                  pl.BlockSpec((1, LANES), lambda b, i: (0, 0)),
                  pl.BlockSpec((tq, tq), lambda b, i: (0, 0))],
        out_specs=[pl.BlockSpec((None, tq, LANES), lambda b, i: (b, i, 0)),
                   pl.BlockSpec((None, tq, LANES), lambda b, i: (b, i, 0))],
        out_shape=[jax.ShapeDtypeStruct((B, T, LANES), F32)] * 2,
        scratch_shapes=[pltpu.VMEM((1, LANES), F32)],
        compiler_params=_cparams(2),
        name="logf_cumsum",
    )(fq, b_pad, tri)


def _lambda_value(lp, lam_init):
    a = jnp.sum(lp[0:1, :] * lp[1:2, :], axis=-1, keepdims=True)
    b = jnp.sum(lp[2:3, :] * lp[3:4, :], axis=-1, keepdims=True)
    return jnp.exp(a) - jnp.exp(b) + lam_init


def _even_attn_kernel(lp_ref, q_ref, k_ref, v_ref, c_ref, tb_ref, sg_ref, o_ref, kb, vb, m_ref, l_ref, acc_ref,
                      *, tq, tk, n_fox, lam_init):
    c = pl.program_id(1)
    qi = pl.program_id(2)

    @pl.when(qi == 0)
    def _():
        kb[...] = k_ref[...].astype(BF16)
        vb[...] = v_ref[...].astype(BF16)

    lo = lax.broadcasted_iota(I32, (1, LANES), 1) < HEAD_DIM
    q = q_ref[...] * SCALE
    qh = (jnp.where(lo, q, 0.0).astype(BF16), jnp.where(lo, 0.0, q).astype(BF16))
    t0 = qi * tq
    n_full = t0 // tk
    sub = tk // tq
    nd = tb_ref.shape[0]

    def tile(kt, bias_fn, first, diag):
        start = _aligned(kt * tk, tk)
        k = kb[:, pl.ds(start, tk)]
        v = vb[:, pl.ds(start, tk)]
        if diag:
            row = t0 + lax.broadcasted_iota(I32, (tq, tk), 0)
            col = start + lax.broadcasted_iota(I32, (tq, tk), 1)
            causal = col <= row
        for h in range(2):
            s = _dot(qh[h], k) + bias_fn(h, kt, start)
            if diag:
                s = jnp.where(causal, s, NEG)
            rmax = jnp.broadcast_to(jnp.max(s, axis=-1, keepdims=True), (tq, LANES))
            if first:
                m_new = rmax
            else:
                m_prev = m_ref[h]
                m_new = jnp.maximum(m_prev, rmax)
                alpha = jnp.exp(m_prev - m_new)
            p = jnp.exp(s - _lane_tile(m_new, tk // LANES))
            rsum = jnp.broadcast_to(jnp.sum(p, axis=-1, keepdims=True), (tq, LANES))
            pv = _dot_nt(p.astype(BF16), v)
            if first:
                l_ref[h] = rsum
                acc_ref[h] = pv
            else:
                l_ref[h] = alpha * l_ref[h] + rsum
                acc_ref[h] = alpha * acc_ref[h] + pv
            m_ref[h] = m_new

    def run(bias_fn):
        @pl.when(n_full == 0)
        def _():
            tile(0, bias_fn, True, True)

        @pl.when(n_full > 0)
        def _():
            tile(0, bias_fn, True, False)

            def body(kt, carry):
                tile(kt, bias_fn, False, False)
                return carry

            lax.fori_loop(1, n_full, body, 0)
            tile(n_full, bias_fn, False, True)

    @pl.when(c < n_fox)
    def _():
        def bias_fn(h, kt, start):
            return -c_ref[h:h + 1, pl.ds(start, tk)]

        run(bias_fn)
        o_ref[...] = jnp.where(lo, acc_ref[0] / l_ref[0], acc_ref[1] / l_ref[1])

    @pl.when(c >= n_fox)
    def _():
        def bias_fn(h, kt, start):
            d0 = qi - kt * sub
            return jnp.concatenate([tb_ref[jnp.clip(d0 - cc, 0, nd - 1)] for cc in range(sub)], axis=1)

        run(bias_fn)
        lam = _lambda_value(lp_ref[...], lam_init)
        o = acc_ref[0] / l_ref[0] - lam * (acc_ref[1] / l_ref[1])
        ms = jnp.mean(o * o, axis=-1, keepdims=True)
        o_ref[...] = o * lax.rsqrt(ms + EPS) * sg_ref[...] * (1.0 - lam_init)


def even_prompt_attention(q_arr, kv_t, cum_pairs, toe, lam_p, subln_g, lam_init, tq, tk):
    B, T, _ = q_arr.shape
    n_fox = 4
    n_tiles = 8
    nd = toe.shape[1]
    kern = functools.partial(_even_attn_kernel, tq=tq, tk=tk, n_fox=n_fox, lam_init=lam_init)
    return pl.pallas_call(
        kern,
        grid=(B, n_tiles, T // tq),
        in_specs=[pl.BlockSpec((4, HEAD_DIM), lambda b, c, i: (0, 0)),
                  pl.BlockSpec((None, tq, LANES), lambda b, c, i: (b, i, c)),
                  pl.BlockSpec((None, LANES, T), lambda b, c, i: (b, c, 0)),
                  pl.BlockSpec((None, LANES, T), lambda b, c, i: (b, n_tiles + c, 0)),
                  pl.BlockSpec((None, None, 2, T), lambda b, c, i: (b, jnp.minimum(c, n_fox - 1), 0, 0)),
                  pl.BlockSpec((None, nd, tq, tq), lambda b, c, i: (jnp.maximum(c - n_fox, 0), 0, 0, 0)),
                  pl.BlockSpec((1, LANES), lambda b, c, i: (0, 0))],
        out_specs=pl.BlockSpec((None, tq, LANES), lambda b, c, i: (b, i, c)),
        out_shape=jax.ShapeDtypeStruct((B, T, n_tiles * LANES), F32),
        scratch_shapes=[pltpu.VMEM((LANES, T), BF16), pltpu.VMEM((LANES, T), BF16),
                        pltpu.VMEM((2, tq, LANES), F32), pltpu.VMEM((2, tq, LANES), F32), pltpu.VMEM((2, tq, LANES), F32)],
        compiler_params=_cparams(3),
        name="even_prompt_attention",
    )(lam_p, q_arr, kv_t, kv_t, cum_pairs, toe, subln_g.reshape(1, LANES))


def _even_attn_t_kernel(lp_ref, q_ref, k_ref, v_ref, c_ref, tbt_ref, sg_ref, o_ref, kb, vb, cb, m_ref, acc_ref,
                        *, tq, tk, n_fox, lam_init):
    c = pl.program_id(1)
    qi = pl.program_id(2)
    T = k_ref.shape[1]
    n_val = v_ref.shape[0]

    @pl.when(qi == 0)
    def _():
        kb[...] = k_ref[...].T.astype(BF16)
        vb[0:n_val, :] = v_ref[...].astype(BF16)
        vb[n_val:, :] = jnp.ones((16, T), BF16)
        for h in range(2):
            cb[h] = jnp.broadcast_to(c_ref[h:h + 1, :] * (-LOG2E), (LANES, T)).T

    lo = lax.broadcasted_iota(I32, (1, LANES), 1) < HEAD_DIM
    q = q_ref[...] * (SCALE * LOG2E)
    qs = jnp.concatenate([jnp.where(lo, q, 0.0), jnp.where(lo, 0.0, q)], axis=0).astype(BF16)
    t0 = qi * tq
    tpos = t0 + lax.broadcasted_iota(I32, (1, tq), 1)
    n_full = t0 // tk
    sub = tk // tq
    nd = tbt_ref.shape[0]

    def tile(kt, bias_fn, first, diag):
        start = _aligned(kt * tk, tk)
        s = _dot_nt(kb[pl.ds(start, tk), :], qs) + bias_fn(kt, start)
        if diag:
            visible = start + lax.broadcasted_iota(I32, (tk, 1), 0) <= tpos
            s = s + _lane_tile(jnp.where(visible, 0.0, NEG), 2)
        s_max = jnp.max(s, axis=0, keepdims=True)
        if first:
            m_new = s_max
        else:
            m_prev = m_ref[...]
            m_new = jnp.maximum(m_prev, s_max)
        p = jnp.exp2((s - m_new).astype(BF16))
        pv = _dot(vb[:, pl.ds(start, tk)], p)
        acc_ref[...] = pv if first else jnp.exp2(m_prev - m_new) * acc_ref[...] + pv
        m_ref[...] = m_new

    def run(bias_fn):
        @pl.when(n_full == 0)
        def _():
            tile(0, bias_fn, True, True)

        @pl.when(n_full > 0)
        def _():
            tile(0, bias_fn, True, False)

            def body(kt, carry):
                tile(kt, bias_fn, False, False)
                return carry

            lax.fori_loop(1, n_full, body, 0)
            tile(n_full, bias_fn, False, True)

    @pl.when(c < n_fox)
    def _():
        def bias_fn(kt, start):
            return jnp.concatenate([_lane_tile(cb[h, pl.ds(start, tk), :], tq // LANES) for h in range(2)], axis=1)

        run(bias_fn)
        acc = acc_ref[...]
        o_ref[0:HEAD_DIM, :] = acc[0:HEAD_DIM, 0:tq] / acc[n_val:n_val + 1, 0:tq]
        o_ref[HEAD_DIM:, :] = acc[HEAD_DIM:n_val, tq:] / acc[n_val:n_val + 1, tq:]

    @pl.when(c >= n_fox)
    def _():
        def bias_fn(kt, start):
            d0 = qi - kt * sub
            rows = jnp.concatenate([tbt_ref[jnp.clip(d0 - cc, 0, nd - 1)] for cc in range(sub)], axis=0)
            return _lane_tile(rows, 2)

        run(bias_fn)
        acc = acc_ref[...]
        lam = _lambda_value(lp_ref[...], lam_init)
        o = acc[0:n_val, 0:tq] / acc[n_val:n_val + 1, 0:tq] - lam * (acc[0:n_val, tq:] / acc[n_val:n_val + 1, tq:])
        ms = jnp.mean(o * o, axis=0, keepdims=True)
        o_ref[...] = o * lax.rsqrt(ms + EPS) * sg_ref[...] * (1.0 - lam_init)


def even_prompt_attention_t(q_arr, kv_t, cum_pairs, toe_t, lam_p, subln_g, lam_init, tq, tk):
    B, T, _ = q_arr.shape
    n_fox = 4
    n_tiles = 8
    nd = toe_t.shape[1]
    assert tq % LANES == 0 and tk % tq == 0 and T % tk == 0
    kern = functools.partial(_even_attn_t_kernel, tq=tq, tk=tk, n_fox=n_fox, lam_init=lam_init)
    return pl.pallas_call(
        kern,
        grid=(B, n_tiles, T // tq),
        in_specs=[pl.BlockSpec((4, HEAD_DIM), lambda b, c, i: (0, 0)),
                  pl.BlockSpec((None, tq, LANES), lambda b, c, i: (b, i, c)),
                  pl.BlockSpec((None, LANES, T), lambda b, c, i: (b, c, 0)),
                  pl.BlockSpec((None, LANES, T), lambda b, c, i: (b, n_tiles + c, 0)),
                  pl.BlockSpec((None, None, 2, T), lambda b, c, i: (b, jnp.minimum(c, n_fox - 1), 0, 0)),
                  pl.BlockSpec((None, nd, tq, tq), lambda b, c, i: (jnp.maximum(c - n_fox, 0), 0, 0, 0)),
                  pl.BlockSpec((LANES, 1), lambda b, c, i: (0, 0))],
        out_specs=pl.BlockSpec((None, LANES, tq), lambda b, c, i: (b, c, i)),
        out_shape=jax.ShapeDtypeStruct((B, n_tiles * LANES, T), F32),
        scratch_shapes=[pltpu.VMEM((T, LANES), BF16), pltpu.VMEM((LANES + 16, T), BF16), pltpu.VMEM((2, T, LANES), F32),
                        pltpu.VMEM((1, 2 * tq), F32), pltpu.VMEM((LANES + 16, 2 * tq), F32)],
        compiler_params=_cparams(3),
        name="even_prompt_attention_t",
    )(lam_p, q_arr, kv_t, kv_t, cum_pairs, toe_t, subln_g.reshape(LANES, 1))


def toeplitz_dist(tq, nd):
    i = np.arange(tq)[:, None]
    j = np.arange(tq)[None, :]
    return jnp.asarray(np.maximum(np.arange(nd)[:, None, None] * tq + (i - j)[None], 0), I32)


def _pool_weights_t(w_pos_rows):
    reps = LANES // CMP_STRIDE
    w0 = jnp.tile(w_pos_rows[:CMP_STRIDE].T, (1, reps))
    w1 = jnp.tile(w_pos_rows[CMP_STRIDE:].T, (1, reps))
    return w0.astype(F32), w1.astype(F32)


def _group_sum_stack(n_tiles):
    spt = LANES // CMP_STRIDE
    assert n_tiles * spt <= LANES
    m = np.zeros((n_tiles, 2 * LANES, 2 * LANES), np.float32)
    pos = np.arange(LANES)
    for i in range(n_tiles):
        m[i, pos, i * spt + pos // CMP_STRIDE] = 1.0
        m[i, LANES + pos, LANES + i * spt + pos // CMP_STRIDE] = 1.0
    return m


def _pool2(x, w0, w1, gmat2, two_pass):
    y = jnp.concatenate([x * w0, x * w1], axis=1)
    hi = y.astype(BF16)
    out = _dot(hi, gmat2)
    if two_pass:
        out = out + _dot((y - hi.astype(F32)).astype(BF16), gmat2)
    return out


def _cmp_finish_t(p0, p1, b_col, w0t, w1t):
    half = w0t.shape[0]
    z = _silu(p0 + _shift_left_one_lane(p1) + b_col)
    ck = _dot(w0t, z[:half].astype(BF16))
    cv = _dot(w1t, z[half:].astype(BF16))
    return ck, cv


def _cmp_build_kernel(c_ref, w0_ref, w1_ref, g_ref, b_ref, w0t_ref, w1t_ref, ck_ref, cv_ref):
    T = c_ref.shape[1]
    nb = ck_ref.shape[1]
    w0 = w0_ref[...]
    w1 = w1_ref[...]
    acc = None
    for lt in range(T // LANES):
        a = _pool2(c_ref[:, lt * LANES:(lt + 1) * LANES], w0, w1, g_ref[lt], True)
        acc = a if acc is None else acc + a
    ck, cv = _cmp_finish_t(acc[:, 0:nb], acc[:, LANES:LANES + nb], b_ref[...], w0t_ref[...], w1t_ref[...])
    ck_ref[...] = ck
    cv_ref[...] = cv


def cmp_build(cs_t, w0, w1, b_col, w0t, w1t):
    B, _, T = cs_t.shape
    nb = T // CMP_STRIDE
    width = w0.shape[0]
    half = width // 2
    n_tiles = T // LANES
    gmat = jnp.asarray(_group_sum_stack(n_tiles), BF16)
    const = lambda shape: pl.BlockSpec(shape, lambda b: (0,) * len(shape))
    return pl.pallas_call(
        _cmp_build_kernel,
        grid=(B,),
        in_specs=[pl.BlockSpec((None, width, T), lambda b: (b, 0, 0)),
                  const((width, LANES)), const((width, LANES)), const((n_tiles, 2 * LANES, 2 * LANES)), const((width, 1)),
                  const((half, half)), const((half, half))],
        out_specs=[pl.BlockSpec((None, half, nb), lambda b: (b, 0, 0))] * 2,
        out_shape=[jax.ShapeDtypeStruct((B, half, nb), F32)] * 2,
        compiler_params=_cparams(1),
        name="cmp_build",
    )(cs_t, w0, w1, gmat, b_col, w0t, w1t)


def _nsa_kernel(q_ref, ck_ref, cv_ref, sk_ref, sv_ref, wk_ref, wv_ref, gt_ref, bg_ref, cb_ref, tb_ref, cov_ref, ex_ref,
                o_ref, sk2, sv2, wk2, wv2, ck2, cv2, madd, p_ref, m_ref, l_ref, a_ref, acc_ref,
                *, tq, tk, n_blk, n_selb, n_top, win_chunks):
    qi = pl.program_id(2)

    @pl.when(qi == 0)
    def _():
        for src, dst in ((sk_ref, sk2), (sv_ref, sv2), (wk_ref, wk2), (wv_ref, wv2), (ck_ref, ck2), (cv_ref, cv2)):
            x = src[...].astype(BF16)
            dst[0:HEAD_DIM, :] = x
            dst[HEAD_DIM:, :] = x

    T = sk_ref.shape[1]
    nb = ck_ref.shape[1]
    nd = tb_ref.shape[1]
    sub = tk // tq
    t0 = qi * tq
    lane = lax.broadcasted_iota(I32, (1, LANES), 1)
    lo = lane < HEAD_DIM
    tcol = t0 + lax.broadcasted_iota(I32, (tq, 1), 0)
    q = q_ref[...] * SCALE
    parts = []
    for j in range(4):
        tile = q[:, (j // 2) * LANES:(j // 2 + 1) * LANES]
        parts.append(jnp.where(lo if j % 2 == 0 else jnp.logical_not(lo), tile, 0.0))
    qs = jnp.concatenate(parts, axis=0).astype(BF16)
    gates = 1.0 / (1.0 + jnp.exp(-(gt_ref[...] + bg_ref[...])))

    n_idx = lax.broadcasted_iota(I32, (1, nb), 1)
    validc = jnp.logical_and(n_idx * CMP_STRIDE + (CMP_BLOCK - 1) <= tcol, n_idx < n_blk)
    sc = _dot(qs, ck2[...])
    o_c = []
    psum = jnp.zeros((tq, nb), F32)
    for j in range(4):
        s = jnp.where(validc, sc[j * tq:(j + 1) * tq] + cb_ref[j], NEG)
        m = jnp.max(s, axis=-1, keepdims=True)
        e = jnp.where(validc, jnp.exp(s - m), 0.0)
        l = jnp.sum(e, axis=-1, keepdims=True)
        p = e / jnp.where(l > 0.0, l, 1.0)
        psum = psum + p
        o_c.append(_dot_nt(p.astype(BF16), cv2[...]))

    score = _dot_exact_rhs01(psum, cov_ref[...])
    cur = tcol // SEL_BLOCK
    forced = jnp.logical_or(lane == 0, jnp.logical_or(lane == cur, lane == cur - 1))
    score = jnp.where(forced, FORCE, score)
    score = jnp.where(lane * SEL_BLOCK <= tcol, score, NEG)
    score = jnp.where(lane < n_selb, score, PAD_SCORE)
    n_selp = -(-n_selb // 8) * 8
    sc_t = score.T[:n_selp]
    jrow = lax.broadcasted_iota(I32, (n_selp, 1), 0)
    rank = jnp.zeros((n_selp, tq), F32)
    for i in range(n_selb):
        ri = sc_t[i:i + 1, :]
        tie = jnp.where(i < jrow, 1.0, 0.0)
        rank = rank + jnp.where(ri > sc_t, 1.0, jnp.where(ri == sc_t, tie, 0.0))
    sel_t = jnp.where(rank < n_top, 1.0, 0.0)
    if n_selp < LANES:
        sel_t = jnp.concatenate([sel_t, jnp.zeros((LANES - n_selp, tq), F32)], axis=0)
    chosen = _dot(sel_t.T.astype(BF16), ex_ref[...])
    kpos = lax.broadcasted_iota(I32, (1, T), 1)
    madd[...] = jnp.where(jnp.logical_and(chosen > 0.5, kpos <= tcol), 0.0, NEG)

    def softmax_rows(j, s, first):
        rows = slice(j * tq, (j + 1) * tq)
        w = s.shape[1]
        rmax = jnp.broadcast_to(jnp.max(s, axis=-1, keepdims=True), (tq, LANES))
        if first:
            m_new = rmax
        else:
            m_prev = m_ref[rows]
            m_new = jnp.maximum(m_prev, rmax)
            alpha = jnp.exp(m_prev - m_new)
            a_ref[rows] = alpha
        p = jnp.exp(s - _lane_tile(m_new, w // LANES))
        rsum = jnp.broadcast_to(jnp.sum(p, axis=-1, keepdims=True), (tq, LANES))
        l_ref[rows] = rsum if first else alpha * l_ref[rows] + rsum
        m_ref[rows] = m_new
        p_ref[rows, 0:w] = p.astype(BF16)

    def toe_bias(j, d0, n_chunks):
        return jnp.concatenate([tb_ref[j, jnp.clip(d0 - cc, 0, nd - 1)] for cc in range(n_chunks)], axis=1)

    kt_d = qi // sub

    def sel_tile(kt, first):
        start = _aligned(kt * tk, tk)
        mask = madd[:, pl.ds(start, tk)]
        s4 = _dot(qs, sk2[:, pl.ds(start, tk)])
        for j in range(4):
            softmax_rows(j, s4[j * tq:(j + 1) * tq] + toe_bias(j, qi - kt * sub, sub) + mask, first)
        pv = _dot_nt(p_ref[:, 0:tk], sv2[:, pl.ds(start, tk)])
        acc_ref[...] = pv if first else a_ref[...] * acc_ref[...] + pv

    sel_tile(kt_d, True)

    def sel_body(i, carry):
        sel_tile(kt_d - 1 - i, False)
        return carry

    lax.fori_loop(0, kt_d, sel_body, 0)
    o_s = acc_ref[...] / l_ref[...]

    ww = win_chunks * LANES
    w0 = jnp.maximum(qi - (win_chunks - 1), 0)
    wstart = pl.multiple_of(w0 * tq, tq)
    dist = tcol - (wstart + lax.broadcasted_iota(I32, (1, ww), 1))
    wmask = jnp.where(jnp.logical_and(dist >= 0, dist <= WINDOW), 0.0, NEG)
    s4 = _dot(qs, wk2[:, pl.ds(wstart, ww)])
    for j in range(4):
        softmax_rows(j, s4[j * tq:(j + 1) * tq] + toe_bias(j, qi - w0, win_chunks) + wmask, True)
    o_w = _dot_nt(p_ref[:, 0:ww], wv2[:, pl.ds(wstart, ww)]) / l_ref[...]

    outs = []
    for j in range(4):
        rows = slice(j * tq, (j + 1) * tq)
        outs.append(gates[:, 3 * j:3 * j + 1] * o_c[j] + gates[:, 3 * j + 1:3 * j + 2] * o_s[rows]
                    + gates[:, 3 * j + 2:3 * j + 3] * o_w[rows])
    o_ref[:, 0:LANES] = jnp.where(lo, outs[0], outs[1])
    o_ref[:, LANES:2 * LANES] = jnp.where(lo, outs[2], outs[3])


def nsa_prompt_attention(qg_arr, ck_t, cv_t, cs_t, w_t, bg_pad, cbias, toe, cover, expand, tq, tk):
    B, T, _ = qg_arr.shape
    nb = ck_t.shape[2]
    nd = toe.shape[1]
    n_selb = T // SEL_BLOCK
    win_chunks = WINDOW // tq + 1
    ww = win_chunks * LANES
    assert tq == LANES and T >= ww and tk % tq == 0 and T % tk == 0
    kern = functools.partial(_nsa_kernel, tq=tq, tk=tk, n_blk=nb - 1, n_selb=n_selb, n_top=min(N_SEL, n_selb),
                             win_chunks=win_chunks)
    rows64 = lambda blk0: pl.BlockSpec((None, HEAD_DIM, T), lambda g, b, i: (b, blk0 + g, 0))
    wide = max(tk, ww)
    return pl.pallas_call(
        kern,
        grid=(4, B, T // tq),
        in_specs=[pl.BlockSpec((None, tq, 2 * LANES), lambda g, b, i: (b, i, g)),
                  pl.BlockSpec((None, HEAD_DIM, nb), lambda g, b, i: (b, g, 0)),
                  pl.BlockSpec((None, HEAD_DIM, nb), lambda g, b, i: (b, g, 0)),
                  rows64(8), rows64(12), rows64(0), rows64(4),
                  pl.BlockSpec((None, tq, LANES), lambda g, b, i: (b, i, 8 + g)),
                  pl.BlockSpec((None, 1, LANES), lambda g, b, i: (g, 0, 0)),
                  pl.BlockSpec((4, tq, nb), lambda g, b, i: (g, i, 0)),
                  pl.BlockSpec((4, nd, tq, tq), lambda g, b, i: (g, 0, 0, 0)),
                  pl.BlockSpec((nb, LANES), lambda g, b, i: (0, 0)),
                  pl.BlockSpec((LANES, T), lambda g, b, i: (0, 0))],
        out_specs=pl.BlockSpec((None, tq, 2 * LANES), lambda g, b, i: (b, i, g)),
        out_shape=jax.ShapeDtypeStruct((B, T, 8 * LANES), F32),
        scratch_shapes=[pltpu.VMEM((LANES, T), BF16)] * 4 + [pltpu.VMEM((LANES, nb), BF16)] * 2
                       + [pltpu.VMEM((tq, T), F32), pltpu.VMEM((4 * tq, wide), BF16)]
                       + [pltpu.VMEM((4 * tq, LANES), F32)] * 4,
        compiler_params=_cparams(3),
        name="nsa_prompt_attention",
    )(qg_arr, ck_t, cv_t, cs_t, cs_t, w_t, w_t, qg_arr, bg_pad, cbias, toe, cover, expand)


def _nsa_t_kernel(q_ref, ckt_ref, cvt_ref, sk_ref, svt_ref, wk_ref, wvt_ref, gt_ref, bg_ref, cbt_ref, tbt_ref, covt_ref, ext_ref,
                  o_ref, svb, wvb, ck2, cvb, *, tq, tk, n_blk, n_selb, n_top, win_chunks, n_chains):
    qi = pl.program_id(2)

    @pl.when(qi == 0)
    def _():
        ones = jnp.ones((16, svt_ref.shape[1]), BF16)
        svb[0:HEAD_DIM, :] = svt_ref[...].astype(BF16)
        svb[HEAD_DIM:, :] = ones
        wvb[0:HEAD_DIM, :] = wvt_ref[...].astype(BF16)
        wvb[HEAD_DIM:, :] = ones
        cvb[...] = cvt_ref[...].astype(BF16)
        ckt = ckt_ref[...]
        ck2[...] = jnp.concatenate([ckt, ckt], axis=0).T.astype(BF16)

    T = svt_ref.shape[1]
    nb = ckt_ref.shape[1]
    nd = tbt_ref.shape[1]
    sub = tk // tq
    t0 = qi * tq
    lane = lax.broadcasted_iota(I32, (1, LANES), 1)
    lo = lane < HEAD_DIM
    tpos = t0 + lax.broadcasted_iota(I32, (1, tq), 1)
    q = q_ref[...]
    zero = jnp.zeros((), BF16)
    parts = []
    for j in range(4):
        tile = q[:, (j // 2) * LANES:(j // 2 + 1) * LANES]
        parts.append(jnp.where(lo if j % 2 == 0 else jnp.logical_not(lo), tile, zero))
    qs = jnp.concatenate(parts, axis=0)
    gt = 1.0 / (1.0 + jnp.exp(-(gt_ref[...] + bg_ref[...])))

    def gate(r):
        return jnp.concatenate([gt[3 * j + r:3 * j + r + 1, :] for j in range(4)], axis=1)

    n_col = lax.broadcasted_iota(I32, (nb, 1), 0)
    validc = jnp.logical_and(n_col * CMP_STRIDE + (CMP_BLOCK - 1) <= tpos, n_col < n_blk)
    validc4 = _lane_tile(validc, 4)
    sc = _dot_nt(ck2[...], qs) + jnp.concatenate([cbt_ref[j] for j in range(4)], axis=1)
    sc = jnp.where(validc4, sc, NEG)
    m = jnp.max(sc, axis=0, keepdims=True)
    e = jnp.where(validc4, jnp.exp2(sc - m), 0.0)
    l = jnp.sum(e, axis=0, keepdims=True)
    p = e / jnp.where(l > 0.0, l, 1.0)
    o_c = _dot(cvb[...], p.astype(BF16))
    psum = p[:, 0:tq] + p[:, tq:2 * tq] + p[:, 2 * tq:3 * tq] + p[:, 3 * tq:4 * tq]

    score = _dot_exact_lhs01(covt_ref[...], psum)
    jrow = lax.broadcasted_iota(I32, (LANES, 1), 0)
    cur = tpos // SEL_BLOCK
    forced = jnp.logical_or(jrow == 0, jnp.logical_or(jrow == cur, jrow == cur - 1))
    score = jnp.where(forced, FORCE, score)
    score = jnp.where(jrow * SEL_BLOCK <= tpos, score, NEG)
    n_selp = -(-n_selb // 8) * 8
    sc_t = score[:n_selp]
    jr = jrow[:n_selp]
    rank = jnp.zeros((n_selp, tq), F32)
    for i in range(n_selb):
        ri = sc_t[i:i + 1, :]
        tie = jnp.where(i < jr, 1.0, 0.0)
        rank = rank + jnp.where(ri > sc_t, 1.0, jnp.where(ri == sc_t, tie, 0.0))
    sel_t = jnp.where(jnp.logical_and(rank < n_top, jr < n_selb), 1.0, 0.0)
    if n_selp < LANES:
        sel_t = jnp.concatenate([sel_t, jnp.zeros((LANES - n_selp, tq), F32)], axis=0)
    sel_b = sel_t.astype(BF16)

    hpc = 4 // n_chains
    qs_c = [qs[c * hpc * tq:(c + 1) * hpc * tq] for c in range(n_chains)]

    def toe_bias(c, d0, n_chunks):
        cols = [jnp.concatenate([tbt_ref[j, jnp.clip(d0 - cc, 0, nd - 1)] for cc in range(n_chunks)], axis=0)
                for j in range(c * hpc, (c + 1) * hpc)]
        return cols[0] if hpc == 1 else jnp.concatenate(cols, axis=1)

    def update(s, vt, state):
        s_max = jnp.max(s, axis=0, keepdims=True)
        if state is None:
            m_new = s_max
        else:
            m_prev, acc_prev = state
            m_new = jnp.maximum(m_prev, s_max)
        p = jnp.exp2((s - m_new).astype(BF16))
        pv = _dot(vt, p)
        if state is None:
            return m_new, pv
        return m_new, jnp.exp2(m_prev - m_new) * acc_prev + pv

    def attend(k_ref, v_ref, start, width, d0, mask, states):
        k = k_ref[pl.ds(start, width), :]
        v = v_ref[:, pl.ds(start, width)]
        mask_c = _lane_tile(mask, hpc)
        cs = range(n_chains)
        raws = [_dot_nt(k, qs_c[c]) for c in cs]
        ss = [raws[c] + toe_bias(c, d0, width // tq) + mask_c for c in cs]
        ms = [jnp.max(ss[c], axis=0, keepdims=True) for c in cs]
        if states is not None:
            ms = [jnp.maximum(states[c][0], ms[c]) for c in cs]
        ps = [jnp.exp2((ss[c] - ms[c]).astype(BF16)) for c in cs]
        pvs = [_dot(v, ps[c]) for c in cs]
        if states is None:
            return tuple((ms[c], pvs[c]) for c in cs)
        return tuple((ms[c], jnp.exp2(states[c][0] - ms[c]) * states[c][1] + pvs[c]) for c in cs)

    kt_d = qi // sub

    def sel_tile(kt, states):
        start = _aligned(kt * tk, tk)
        chosen = _dot(ext_ref[pl.ds(start, tk), :], sel_b) > 0.5
        if states is None:
            chosen = jnp.logical_and(chosen, start + lax.broadcasted_iota(I32, (tk, 1), 0) <= tpos)
        return attend(sk_ref, svb, start, tk, qi - kt * sub, jnp.where(chosen, 0.0, NEG), states)

    states = lax.fori_loop(0, kt_d, lambda i, st: sel_tile(kt_d - 1 - i, st), sel_tile(kt_d, None))
    o_s = jnp.concatenate([acc[0:HEAD_DIM] / acc[HEAD_DIM:HEAD_DIM + 1] for _, acc in states], axis=1)

    ww = win_chunks * tq
    w0 = jnp.maximum(qi - (win_chunks - 1), 0)
    wstart = pl.multiple_of(w0 * tq, tq)
    dist = tpos - (wstart + lax.broadcasted_iota(I32, (ww, 1), 0))
    wmask = jnp.where(jnp.logical_and(dist >= 0, dist <= WINDOW), 0.0, NEG)
    states = attend(wk_ref, wvb, wstart, ww, qi - w0, wmask, None)
    o_w = jnp.concatenate([acc[0:HEAD_DIM] / acc[HEAD_DIM:HEAD_DIM + 1] for _, acc in states], axis=1)

    o = gate(0) * o_c + gate(1) * o_s + gate(2) * o_w
    for j in range(4):
        o_ref[j * HEAD_DIM:(j + 1) * HEAD_DIM, :] = o[:, j * tq:(j + 1) * tq]


def nsa_prompt_attention_t(qk_arr, ck_t, cv_t, cs_t, wg_t, bg_col, cbias_t, toe_t, cover_t, expand_t, tq, tk):
    B, T, _ = qk_arr.shape
    nb = ck_t.shape[2]
    nd = toe_t.shape[1]
    n_selb = T // SEL_BLOCK
    win_chunks = WINDOW // tq + 1
    assert tq == LANES and T >= win_chunks * tq and tk % tq == 0 and T % tk == 0 and nb <= LANES
    kern = functools.partial(_nsa_t_kernel, tq=tq, tk=tk, n_blk=nb - 1, n_selb=n_selb, n_top=min(N_SEL, n_selb),
                             win_chunks=win_chunks, n_chains=1)
    return pl.pallas_call(
        kern,
        grid=(4, B, T // tq),
        in_specs=[pl.BlockSpec((None, tq, 2 * LANES), lambda g, b, i: (b, i, g)),
                  pl.BlockSpec((None, HEAD_DIM, nb), lambda g, b, i: (b, g, 0)),
                  pl.BlockSpec((None, HEAD_DIM, nb), lambda g, b, i: (b, g, 0)),
                  pl.BlockSpec((None, T, LANES), lambda g, b, i: (b, 0, 8 + g)),
                  pl.BlockSpec((None, HEAD_DIM, T), lambda g, b, i: (b, 12 + g, 0)),
                  pl.BlockSpec((None, T, LANES), lambda g, b, i: (b, 0, 12 + g)),
                  pl.BlockSpec((None, HEAD_DIM, T), lambda g, b, i: (b, 4 + g, 0)),
                  pl.BlockSpec((None, 16, tq), lambda g, b, i: (b, 32 + g, i)),
                  pl.BlockSpec((None, 16, 1), lambda g, b, i: (g, 0, 0)),
                  pl.BlockSpec((4, nb, tq), lambda g, b, i: (g, 0, i)),
                  pl.BlockSpec((4, nd, tq, tq), lambda g, b, i: (g, 0, 0, 0)),
                  pl.BlockSpec((LANES, nb), lambda g, b, i: (0, 0)),
                  pl.BlockSpec((T, LANES), lambda g, b, i: (0, 0))],
        out_specs=pl.BlockSpec((None, 4 * HEAD_DIM, tq), lambda g, b, i: (b, g, i)),
        out_shape=jax.ShapeDtypeStruct((B, 16 * HEAD_DIM, T), F32),
        scratch_shapes=[pltpu.VMEM((HEAD_DIM + 16, T), BF16), pltpu.VMEM((HEAD_DIM + 16, T), BF16),
                        pltpu.VMEM((nb, LANES), BF16), pltpu.VMEM((HEAD_DIM, nb), BF16)],
        compiler_params=_cparams(3),
        name="nsa_prompt_attention_t",
    )(qk_arr, ck_t, cv_t, qk_arr, cs_t, qk_arr, wg_t, wg_t, bg_col, cbias_t, toe_t, cover_t, expand_t)


def _out_res_t_kernel(a_ref, w_ref, g_ref, r_ref, o_ref):
    y = _dot(a_ref[...].T.astype(BF16), w_ref[...])
    o_ref[...] = r_ref[...] + _norm_rows(y, g_ref[...])


def out_proj_residual_t(a_t, w, g, res):
    B, K, T = a_t.shape
    D = w.shape[1]
    tm = min(T, 512)
    nt = T // tm
    return pl.pallas_call(
        _out_res_t_kernel,
        grid=(B, nt),
        in_specs=[pl.BlockSpec((None, K, tm), lambda b, i: (b, 0, i)),
                  pl.BlockSpec((K, D), lambda b, i: (0, 0)),
                  pl.BlockSpec((1, D), lambda b, i: (0, 0)),
                  pl.BlockSpec((tm, D), lambda b, i: (b * nt + i, 0))],
        out_specs=pl.BlockSpec((tm, D), lambda b, i: (b * nt + i, 0)),
        out_shape=jax.ShapeDtypeStruct((B * T, D), F32),
        compiler_params=_cparams(2),
        name="out_proj_residual_t",
    )(a_t, w, g.reshape(1, D), res)


def cover_matrix_np(n_blk, n_selb, rows, cols):
    i = np.arange(rows)[:, None]
    j = np.arange(cols)[None, :]
    m = (i * CMP_STRIDE < (j + 1) * SEL_BLOCK) & (i * CMP_STRIDE + CMP_BLOCK > j * SEL_BLOCK) & (i < n_blk) & (j < n_selb)
    return m.astype(np.float32)


def _even_dec_kernel(pt_ref, *refs, pps, n_steps, lam_init):
    kv_refs = refs[0:pps]
    lf_refs = refs[pps:2 * pps]
    (qt_ref, cn_ref, tbt_ref, ut_ref, kn_ref, vn_ref, bo_ref, lp_ref, sg_ref,
     o_ref, m_ref, l_ref, acc_ref, carry_ref) = refs[2 * pps:]
    s = pl.program_id(1)
    n_maps, width = qt_ref.shape
    n_fox = lf_refs[0].shape[0]

    @pl.when(s == 0)
    def _():
        m_ref[...] = jnp.full_like(m_ref, NEG)
        l_ref[...] = jnp.zeros_like(l_ref)
        acc_ref[...] = jnp.zeros_like(acc_ref)
        carry_ref[...] = jnp.zeros_like(carry_ref)

    qt = qt_ref[...]
    is_fox = lax.broadcasted_iota(I32, (n_maps, 1), 0) < n_fox
    pad = jnp.zeros((n_maps - n_fox, PAGE_SIZE), F32)
    lfts = jnp.concatenate([jnp.concatenate([lf_refs[i][...], pad], axis=0) for i in range(pps)], axis=0)
    within = _dot_exact_rhs01(lfts, ut_ref[...])
    totals = jnp.sum(lfts, axis=-1, keepdims=True)
    run = carry_ref[...]
    later = [None] * pps
    for i in reversed(range(pps)):
        later[i] = run
        run = run + totals[i * n_maps:(i + 1) * n_maps]
    carry_ref[...] = run
    cn = cn_ref[...]
    sts = []
    for i in range(pps):
        kpg = kv_refs[i][0].reshape(width, PAGE_SIZE).astype(BF16)
        suffix = within[i * n_maps:(i + 1) * n_maps] + later[i] + cn
        sts.append(_dot(qt, kpg) + jnp.where(is_fox, suffix, tbt_ref[i]))
    st = jnp.concatenate(sts, axis=1)
    m = m_ref[...]
    m_new = jnp.maximum(m, jnp.max(st, axis=-1, keepdims=True))
    alpha = jnp.exp(m - m_new)
    p = jnp.exp(st - m_new)
    l_ref[...] = alpha * l_ref[...] + jnp.sum(p, axis=-1, keepdims=True)
    pb = p.astype(BF16)
    pv = None
    for i in range(pps):
        vpg = kv_refs[i][1].reshape(width, PAGE_SIZE).astype(BF16)
        d = _dot_nt(pb[:, i * PAGE_SIZE:(i + 1) * PAGE_SIZE], vpg)
        pv = d if pv is None else pv + d
    acc_ref[...] = alpha * acc_ref[...] + pv
    m_ref[...] = m_new

    @pl.when(s == n_steps - 1)
    def _():
        kn = kn_ref[...].astype(BF16).astype(F32)
        vn = vn_ref[...].astype(BF16).astype(F32)
        s_own = jnp.sum(qt.astype(F32) * kn, axis=-1, keepdims=True) + bo_ref[...]
        m = m_ref[...]
        m_all = jnp.maximum(m, s_own)
        a = jnp.exp(m - m_all)
        e_own = jnp.exp(s_own - m_all)
        l_all = a * l_ref[...] + e_own
        o = (a * acc_ref[...] + e_own * vn) / l_all
        r = lax.broadcasted_iota(I32, (n_maps, width), 0)
        cidx = lax.broadcasted_iota(I32, (n_maps, width), 1)
        d_fox = n_fox * HEAD_DIM
        fox_sel = jnp.logical_and(cidx < d_fox, r == cidx // HEAD_DIM)
        dh = (cidx - d_fox) // LANES
        d1_sel = jnp.logical_and(cidx >= d_fox, r == n_fox + 2 * dh)
        d2_sel = jnp.logical_and(cidx >= d_fox, r == n_fox + 2 * dh + 1)
        o_f = jnp.sum(jnp.where(fox_sel, o, 0.0), axis=0, keepdims=True)
        a1 = jnp.sum(jnp.where(d1_sel, o, 0.0), axis=0, keepdims=True)
        a2 = jnp.sum(jnp.where(d2_sel, o, 0.0), axis=0, keepdims=True)
        lam = _lambda_value(lp_ref[...], lam_init)
        o_d = a1 - lam * a2
        for tl in range(width // LANES):
            sl = slice(tl * LANES, (tl + 1) * LANES)
            if tl * LANES < d_fox:
                o_ref[:, sl] = o_f[:, sl]
            else:
                x = o_d[:, sl]
                ms = jnp.mean(x * x, axis=-1, keepdims=True)
                o_ref[:, sl] = x * lax.rsqrt(ms + EPS) * sg_ref[...] * (1.0 - lam_init)


def even_decode(pool_t, lf_t, li, pt_flat, n_pages, qt, cn, tbt, kn, vn, bo, lam_p, subln_g, lam_init, pps):
    DB, n_maps, width = qt.shape
    n_fox = lf_t.shape[2]
    n_steps = n_pages // pps
    ut = jnp.asarray(np.tril(np.ones((PAGE_SIZE, PAGE_SIZE), np.float32), -1), BF16)

    def page(i):
        return lambda b, s, pt: pt[b * n_pages + (n_steps - 1 - s) * pps + i]

    kv_block = (None, None, 2, n_maps, HEAD_DIM, PAGE_SIZE)
    kv_specs = [pl.BlockSpec(kv_block, lambda b, s, pt, f=page(i): (li, f(b, s, pt), 0, 0, 0, 0)) for i in range(pps)]
    lf_specs = [pl.BlockSpec((None, None, n_fox, PAGE_SIZE), lambda b, s, pt, f=page(i): (li, f(b, s, pt), 0, 0)) for i in range(pps)]
    per_b = lambda shape: pl.BlockSpec((None,) + shape, lambda b, s, pt: (b,) + (0,) * len(shape))
    const = lambda shape: pl.BlockSpec(shape, lambda b, s, pt: (0,) * len(shape))
    grid_spec = pltpu.PrefetchScalarGridSpec(
        num_scalar_prefetch=1,
        grid=(DB, n_steps),
        in_specs=kv_specs + lf_specs + [
            per_b((n_maps, width)), per_b((n_maps, 1)),
            pl.BlockSpec((pps, n_maps, PAGE_SIZE), lambda b, s, pt: (n_steps - 1 - s, 0, 0)),
            const((PAGE_SIZE, PAGE_SIZE)),
            per_b((1, width)), per_b((1, width)),
            const((n_maps, 1)), const((4, HEAD_DIM)), const((1, LANES))],
        out_specs=per_b((1, width)),
        scratch_shapes=[pltpu.VMEM((n_maps, 1), F32), pltpu.VMEM((n_maps, 1), F32), pltpu.VMEM((n_maps, width), F32),
                        pltpu.VMEM((n_maps, 1), F32)])
    kern = functools.partial(_even_dec_kernel, pps=pps, n_steps=n_steps, lam_init=lam_init)
    return pl.pallas_call(
        kern, grid_spec=grid_spec,
        out_shape=jax.ShapeDtypeStruct((DB, 1, width), F32),
        compiler_params=_cparams(2),
        name="even_decode",
    )(pt_flat, *([pool_t] * pps), *([lf_t] * pps), qt, cn, tbt, ut, kn, vn, bo, lam_p, subln_g.reshape(1, LANES))


def _odd_cmp_kernel(pt_ref, *refs, pps, n_steps, n_selb, n_top, t_pos):
    r_refs = refs[0:pps]
    (w0_ref, w1_ref, g_ref, b_ref, w0t_ref, w1t_ref, qz_ref, cb_ref, cov_ref, oc_ref, idx_ref, p0, p1) = refs[pps:]
    s = pl.program_id(1)
    width = w0_ref.shape[0]
    half = width // 2
    spp = PAGE_SIZE // CMP_STRIDE
    w0 = w0_ref[...]
    w1 = w1_ref[...]
    g = g_ref[...]
    for i in range(pps):
        base = pl.multiple_of((s * pps + i) * spp, spp)
        for fc in range(width // LANES):
            cols = slice(fc * LANES, (fc + 1) * LANES)
            x = r_refs[i][fc // 2, 2 * (fc % 2):2 * (fc % 2) + 2].reshape(LANES, PAGE_SIZE)
            y = jnp.concatenate([x * w0[cols], x * w1[cols]], axis=0).astype(BF16)
            sums = _dot_nt(g, y)
            p0[pl.ds(base, spp), cols] = sums[0:spp, 0:LANES]
            p1[pl.ds(base, spp), cols] = sums[0:spp, LANES:]

    @pl.when(s == n_steps - 1)
    def _():
        nb = p0.shape[0]
        z = _silu(p0[...] + pltpu.roll(p1[...], nb - 1, 0) + b_ref[...])
        ck = _dot(z[:, :half].astype(BF16), w0t_ref[...]).astype(BF16)
        cv = _dot(z[:, half:].astype(BF16), w1t_ref[...]).astype(BF16)
        st = _dot_nt(qz_ref[...], ck) + cb_ref[...]
        n_idx = lax.broadcasted_iota(I32, (1, nb), 1)
        valid = n_idx * CMP_STRIDE + (CMP_BLOCK - 1) <= t_pos
        sm = jnp.where(valid, st, NEG)
        m = jnp.max(sm, axis=-1, keepdims=True)
        e = jnp.where(valid, jnp.exp(sm - m), 0.0)
        l = jnp.sum(e, axis=-1, keepdims=True)
        p = e / jnp.where(l > 0.0, l, 1.0)
        oc_ref[...] = _dot(p.astype(BF16), cv)
        gqa = p.shape[0] // 4
        rows = [jnp.sum(p[g * gqa:(g + 1) * gqa], axis=0, keepdims=True) for g in range(4)]
        psum = jnp.concatenate(rows + rows, axis=0)
        score = _dot_exact_rhs01(psum, cov_ref[...])
        nsp = score.shape[1]
        jb = lax.broadcasted_iota(I32, (1, nsp), 1)
        cur = t_pos // SEL_BLOCK
        forced = jnp.logical_or(jb == 0, jnp.logical_or(jb == cur, jb == cur - 1))
        score = jnp.where(forced, FORCE, score)
        score = jnp.where(jb * SEL_BLOCK <= t_pos, score, NEG)
        score = jnp.where(jb < n_selb, score, PAD_SCORE)
        jf = jb.astype(F32)
        slot = lax.broadcasted_iota(I32, (1, LANES), 1)
        picks = jnp.zeros((8, LANES), F32)
        for it in range(n_top):
            mx = jnp.max(score, axis=-1, keepdims=True)
            ix = jnp.min(jnp.where(score == mx, jf, 1e9), axis=-1, keepdims=True)
            picks = jnp.where(slot == it, ix, picks)
            score = jnp.where(jf == ix, PAD_SCORE, score)
        idx_ref[...] = picks.astype(I32)


def odd_decode_cmp(pool_t, li, pt_flat, n_pages, w0, w1, b_row, w0bd, w1bd, qz16, cbias, cover, n_selb, t_pos, pps):
    DB = qz16.shape[0]
    width = w0.shape[0]
    half = width // 2
    n_steps = n_pages // pps
    nb = n_pages * PAGE_SIZE // CMP_STRIDE
    nsp = cover.shape[1]
    spp = PAGE_SIZE // CMP_STRIDE
    gsum = np.zeros((16, PAGE_SIZE), np.float32)
    gsum[np.arange(PAGE_SIZE) // CMP_STRIDE, np.arange(PAGE_SIZE)] = 1.0
    gsum = jnp.asarray(gsum, BF16)
    r_specs = [pl.BlockSpec((None, None, 2, 4, HEAD_DIM, PAGE_SIZE),
                            lambda b, s, pt, i=i: (li, pt[b * n_pages + s * pps + i], 0, 0, 0, 0)) for i in range(pps)]
    per_b = lambda shape: pl.BlockSpec((None,) + shape, lambda b, s, pt: (b,) + (0,) * len(shape))
    const = lambda shape: pl.BlockSpec(shape, lambda b, s, pt: (0,) * len(shape))
    grid_spec = pltpu.PrefetchScalarGridSpec(
        num_scalar_prefetch=1,
        grid=(DB, n_steps),
        in_specs=r_specs + [const((width, LANES)), const((width, LANES)), const((16, PAGE_SIZE)), const((1, width)),
                            const((half, half)), const((half, half)),
                            per_b((16, half)), const((16, nb)), const((nb, nsp))],
        out_specs=[per_b((16, half)), per_b((8, LANES))],
        scratch_shapes=[pltpu.VMEM((nb, width), F32), pltpu.VMEM((nb, width), F32)])
    kern = functools.partial(_odd_cmp_kernel, pps=pps, n_steps=n_steps, n_selb=n_selb, n_top=min(N_SEL, n_selb), t_pos=t_pos)
    return pl.pallas_call(
        kern, grid_spec=grid_spec,
        out_shape=[jax.ShapeDtypeStruct((DB, 16, half), F32), jax.ShapeDtypeStruct((DB, 8, LANES), I32)],
        compiler_params=_cparams(2),
        name="odd_decode_cmp",
    )(pt_flat, *([pool_t] * pps), w0, w1, gsum, b_row, w0bd, w1bd, qz16, cbias, cover)


def _odd_sel_kernel(idx_ref, pt_ref, *refs, n_top, n_past_blk, bpp, t_pos):
    kv_refs = refs[0:n_top]
    (q_ref, sn_ref, tb_ref, wkv_ref, wn_ref, tw_ref, tw0_ref, oc_ref, gt_ref, o_ref) = refs[n_top:]
    b = pl.program_id(0)
    g = pl.program_id(1)
    q = q_ref[...]
    qf = q.astype(F32)
    lane = lax.broadcasted_iota(I32, (1, PAGE_SIZE), 1)
    bias0 = tw0_ref[:, 0:1]

    logits, vals = [], []
    n_new = jnp.zeros((), I32)
    for k in range(n_top):
        blk = idx_ref[(b * 4 + g) * n_top + k]
        is_new = blk >= n_past_blk
        n_new = n_new + is_new.astype(I32)
        page = jnp.minimum(blk // bpp, n_past_blk // bpp - 1)
        bias = tb_ref[pl.ds(pl.multiple_of((page * 4 + g) * 8, 8), 8), :]
        s = _dot(q, kv_refs[k][0].astype(BF16)) + bias
        pos = page * PAGE_SIZE + lane
        valid = jnp.logical_and(jnp.logical_and(pos // SEL_BLOCK == blk, pos <= t_pos), jnp.logical_not(is_new))
        logits.append(jnp.where(valid, s, NEG))
        vals.append(kv_refs[k][1].astype(BF16))
    has_new = n_new > 0
    kn = sn_ref[0:1, :].astype(BF16).astype(F32)
    vn = sn_ref[1:2, :].astype(BF16).astype(F32)
    s_new = jnp.where(has_new, jnp.sum(qf * kn, axis=-1, keepdims=True) + bias0, NEG)
    m = s_new
    for s in logits:
        m = jnp.maximum(m, s.max(axis=-1, keepdims=True))
    p_new = jnp.where(has_new, jnp.exp(s_new - m), 0.0)
    l = p_new
    acc = p_new * vn
    for s, vt in zip(logits, vals):
        p = jnp.where(s > 0.5 * NEG, jnp.exp(s - m), 0.0)
        l = l + jnp.sum(p, axis=-1, keepdims=True)
        acc = acc + _dot_nt(p.astype(BF16), vt)
    o_s = acc / jnp.where(l > 0.0, l, 1.0)

    sw = _dot(q, wkv_ref[0].astype(BF16)) + tw_ref[...]
    kwn = wn_ref[0:1, :].astype(BF16).astype(F32)
    vwn = wn_ref[1:2, :].astype(BF16).astype(F32)
    sw_new = jnp.sum(qf * kwn, axis=-1, keepdims=True) + bias0
    mw = jnp.maximum(jnp.max(sw, axis=-1, keepdims=True), sw_new)
    pw = jnp.exp(sw - mw)
    pw_new = jnp.exp(sw_new - mw)
    lw = jnp.sum(pw, axis=-1, keepdims=True) + pw_new
    o_w = (_dot_nt(pw.astype(BF16), wkv_ref[1].astype(BF16)) + pw_new * vwn) / lw

    gates = 1.0 / (1.0 + jnp.exp(-gt_ref[...]))
    o_ref[...] = gates[:, 0:1] * oc_ref[...] + gates[:, 1:2] * o_s + gates[:, 2:3] * o_w


def odd_decode_sel(pool_t, li, idx_flat, pt_flat, n_pages, win_t, qg, s_new, tbp, w_new, tw, tw0, oc_g, graw, n_top, t_pos):
    DB = qg.shape[0]
    bpp = PAGE_SIZE // SEL_BLOCK
    wb = win_t.shape[-1]
    n_past_blk = n_pages * bpp

    def blk_spec(k):
        def imap(b, g, idx, pt):
            blk = idx[(b * 4 + g) * n_top + k]
            page = pt[b * n_pages + jnp.minimum(blk // bpp, n_pages - 1)]
            return (li, page, 1, g, 0, 0)
        return pl.BlockSpec((None, None, 2, None, HEAD_DIM, PAGE_SIZE), imap)

    per_bg = lambda shape: pl.BlockSpec((None, None) + shape, lambda b, g, idx, pt: (b, g) + (0,) * len(shape))
    grid_spec = pltpu.PrefetchScalarGridSpec(
        num_scalar_prefetch=2,
        grid=(DB, 4),
        in_specs=[blk_spec(k) for k in range(n_top)] + [
            per_bg((8, HEAD_DIM)), per_bg((2, HEAD_DIM)),
            pl.BlockSpec(tbp.shape, lambda b, g, idx, pt: (0, 0)),
            pl.BlockSpec((None, None, 2, None, HEAD_DIM, wb), lambda b, g, idx, pt: (li, b, 0, g, 0, 0)),
            per_bg((2, HEAD_DIM)),
            pl.BlockSpec((8, wb), lambda b, g, idx, pt: (g, 0)),
            pl.BlockSpec((8, LANES), lambda b, g, idx, pt: (g, 0)),
            per_bg((8, HEAD_DIM)), per_bg((8, LANES))],
        out_specs=per_bg((8, HEAD_DIM)))
    kern = functools.partial(_odd_sel_kernel, n_top=n_top, n_past_blk=n_past_blk, bpp=bpp, t_pos=t_pos)
    return pl.pallas_call(
        kern, grid_spec=grid_spec,
        out_shape=jax.ShapeDtypeStruct((DB, 4, 8, HEAD_DIM), F32),
        compiler_params=_cparams(2),
        name="odd_decode_sel",
    )(idx_flat, pt_flat, *([pool_t] * n_top), qg, s_new, tbp, win_t, w_new, tw, tw0, oc_g, graw)


def _pad_cols(w, n):
    return jnp.pad(w, ((0, 0), (0, n - w.shape[1])))


def _pad_rows8(x4):
    pad = [(0, 0)] * x4.ndim
    pad[-2] = (0, 4)
    return jnp.pad(x4, pad)


def even_layer(hp, hs, pool_kv, pool_lf, li, page_table, g0, w_in, b_f, lam_p, subln_g, rel_table, layer):
    B, T, D = hp.shape
    DB, S, _ = hs.shape
    assert S == 1
    n_fox, d_fox = 8, 512
    lam_init = 0.8 - 0.6 * math.exp(-0.3 * layer)
    cuts = np.cumsum((d_fox,) * 6)
    q_f, k_f, v_f, q_d, k_d, v_d, f_w = jnp.split(w_in, [int(c) for c in cuts], axis=1)
    w_q = jnp.concatenate([q_f, q_d, _pad_cols(f_w, LANES)], axis=1).astype(BF16)
    w_kv = jnp.concatenate([k_f, k_d, v_f, v_d], axis=1).astype(BF16)
    bf_pad = jnp.pad(b_f.astype(F32), (0, LANES - n_fox)).reshape(1, LANES)
    table4 = rel_table[:, :4]

    q_arr = norm_proj(hp.reshape(B * T, D), g0, w_q, w_q.shape[1]).reshape(B, T, -1)
    kv_t = norm_proj_t(hp, g0, w_kv.T, 2048)
    tq = min(T, 256)
    tk = min(T, 512)
    lf_pad, cum_pad = logf_cumsum(q_arr, 8, bf_pad, tq)
    cum_pairs = cum_pad[:, :, :n_fox].transpose(0, 2, 1).reshape(B, 4, 2, T)
    toe_t = LOG2E * bias_table(table4, toeplitz_dist(tq, T // tq).transpose(0, 2, 1))
    attn_p = even_prompt_attention_t(q_arr, kv_t, cum_pairs, toe_t, lam_p, subln_g, lam_init, tq, tk)
    ekv_p = kv_t.reshape(B, 2, 16, HEAD_DIM, T).transpose(0, 4, 1, 2, 3)
    elf_p = lf_pad[:, :, :n_fox]

    n_pages = page_table.shape[1]
    past = n_pages * PAGE_SIZE
    xs = hs.reshape(DB, D)
    qs_arr = norm_proj(xs, g0, w_q, 384)
    kvs_arr = norm_proj(xs, g0, w_kv, 512)
    lfs_pad, _ = logf_cumsum(qs_arr.reshape(1, DB, -1), 8, bf_pad, DB)
    lf_new = lfs_pad[0, :, :n_fox]
    seg = np.zeros((16, 1024), np.float32)
    for m_ in range(16):
        seg[m_, m_ * HEAD_DIM:(m_ + 1) * HEAD_DIM] = 1.0
    qt = (qs_arr[:, None, :1024] * SCALE * seg[None]).astype(BF16)
    cn = jnp.pad(lf_new, ((0, 0), (0, 8)))[:, :, None]
    pos = jnp.arange(past).reshape(n_pages, PAGE_SIZE)
    rb = jnp.repeat(bias_table(table4, past - pos), 2, axis=0)
    tbt = jnp.concatenate([jnp.zeros((n_pages, 8, PAGE_SIZE), F32), rb.transpose(1, 0, 2)], axis=1)
    bo = jnp.concatenate([jnp.zeros((8,), F32), jnp.repeat(rel_table[0, :4], 2)]).reshape(16, 1)
    pool_t = pool_kv.transpose(0, 1, 3, 4, 5, 2)
    lf_t = pool_lf.transpose(0, 1, 3, 2)
    attn_s = even_decode(pool_t, lf_t, li, page_table.reshape(-1), n_pages, qt, cn, tbt,
                         kvs_arr[:, None, :1024], kvs_arr[:, None, 1024:], bo, lam_p, subln_g, lam_init,
                         pps=math.gcd(n_pages, 16))
    ekv_s = kvs_arr.reshape(DB, 1, 2, 16, HEAD_DIM)
    elf_s = lf_new.reshape(DB, 1, n_fox)
    return attn_p, attn_s.reshape(DB, -1), ekv_p, ekv_s, elf_p, elf_s


def odd_layer(hp, hs, pool, win_state, li, page_table, g0, w_in, b_gate, w_pos, b_cmp, w_cout, rel_table):
    B, T, D = hp.shape
    DB, S, _ = hs.shape
    assert S == 1
    d_q, d_kv = 1024, 512
    w_q, w_c, w_s, w_w, w_g = jnp.split(w_in, [d_q, d_q + d_kv, d_q + 2 * d_kv, d_q + 3 * d_kv], axis=1)
    n_gate = 12
    w_g4 = jnp.concatenate([_pad_cols(w_g[:, g * n_gate:(g + 1) * n_gate], LANES) for g in range(4)], axis=1)
    w_qg = jnp.concatenate([w_q, w_g4], axis=1).astype(BF16)
    w_cs = jnp.concatenate([w_c, w_s], axis=1).astype(BF16)
    w_w16 = w_w.astype(BF16)
    bg = b_gate.astype(F32)
    bg_pad = jnp.pad(bg.reshape(4, 1, n_gate), ((0, 0), (0, 0), (0, LANES - n_gate)))
    wp = w_pos.reshape(CMP_BLOCK, d_kv)
    w0, w1 = _pool_weights_t(wp)
    b_col = b_cmp.reshape(d_kv, 1).astype(F32)
    eye4 = jnp.eye(4, dtype=F32)
    w0t = jnp.kron(eye4, w_cout[0].T).astype(BF16)
    w1t = jnp.kron(eye4, w_cout[1].T).astype(BF16)
    table = rel_table

    d_g = d_kv // 4
    dup = lambda w: jnp.concatenate([w[:, g * HEAD_DIM:(g + 1) * HEAD_DIM] for g in range(4) for _ in range(2)], axis=1)
    w_qk = jnp.concatenate([w_q * (SCALE * LOG2E), dup(w_s[:, :2 * d_g]), dup(w_w[:, :2 * d_g])], axis=1).astype(BF16)
    w_g16 = jnp.concatenate([_pad_cols(w_g[:, g * n_gate:(g + 1) * n_gate], 16) for g in range(4)], axis=1)
    w_wg_t = jnp.concatenate([w_w, w_g16], axis=1).T.astype(BF16)
    bg_col = jnp.pad(bg.reshape(4, n_gate, 1), ((0, 0), (0, 16 - n_gate), (0, 0)))
    qk_arr = norm_proj(hp.reshape(B * T, D), g0, w_qk, 2048, BF16).reshape(B, T, -1)
    cs_t = norm_proj_t(hp, g0, w_cs.T, 1024)
    wg_t = norm_proj_t(hp, g0, w_wg_t, w_wg_t.shape[0])
    ck_t, cv_t = cmp_build(cs_t, w0, w1, b_col, w0t, w1t)
    nb = T // CMP_STRIDE
    n_blk = nb - 1
    n_selb = T // SEL_BLOCK
    tq = LANES
    tk = min(T, 512)
    t_all = jnp.arange(T)
    e_pos = jnp.arange(nb) * CMP_STRIDE + CMP_BLOCK - 1
    cbias_t = LOG2E * bias_table(table, t_all[None, :] - e_pos[:, None])
    toe_t = LOG2E * bias_table(table, toeplitz_dist(tq, T // tq).transpose(0, 2, 1))
    cover_t = jnp.asarray(cover_matrix_np(n_blk, n_selb, nb, LANES).T, BF16)
    expand_t = np.zeros((T, LANES), np.float32)
    expand_t[np.arange(T), np.arange(T) // SEL_BLOCK] = 1.0
    attn_p = nsa_prompt_attention_t(qk_arr, ck_t, cv_t, cs_t, wg_t, bg_col, cbias_t, toe_t, cover_t,
                                    jnp.asarray(expand_t, BF16), tq, tk)
    okv_p = cs_t.reshape(B, 4, 4, HEAD_DIM, T).transpose(0, 4, 1, 2, 3)
    wn = min(WINDOW, T)
    win_p = wg_t[:, :d_kv, T - wn:].reshape(B, 2, 4, HEAD_DIM, wn).transpose(0, 4, 1, 2, 3)

    n_pages = page_table.shape[1]
    past = n_pages * PAGE_SIZE
    xs = hs.reshape(DB, D)
    qgs = norm_proj(xs, g0, w_qg, 512)
    css = norm_proj(xs, g0, w_cs, 512)
    wns = norm_proj(xs, g0, w_w16, 512)
    pt_flat = page_table.reshape(-1)
    nbs = past // CMP_STRIDE
    n_past_blk = past // SEL_BLOCK
    n_selb_s = n_past_blk + 1
    nsp = -(-n_selb_s // LANES) * LANES
    qs = qgs[:, :d_q]
    segz = np.zeros((16, 256), np.float32)
    for h_ in range(16):
        segz[h_, (h_ // 4) * HEAD_DIM:(h_ // 4 + 1) * HEAD_DIM] = 1.0
    qz16 = (jnp.tile(qs.reshape(DB, 16, 1, HEAD_DIM), (1, 1, 4, 1)).reshape(DB, 16, 256) * SCALE * segz[None]).astype(BF16)
    e_pos_s = jnp.arange(nbs) * CMP_STRIDE + CMP_BLOCK - 1
    cbias_s = bias_table(table, past - e_pos_s)
    cover_s = jnp.asarray(cover_matrix_np(nbs, n_selb_s, nbs, nsp), BF16)
    pool_t = pool.transpose(0, 1, 3, 4, 5, 2)
    n_top = min(N_SEL, n_selb_s)
    oc16, idx8 = odd_decode_cmp(pool_t, li, pt_flat, n_pages, w0, w1, b_col.reshape(1, d_kv), w0t.T, w1t.T, qz16, cbias_s,
                                cover_s, n_selb_s, past, pps=math.gcd(n_pages, 16))
    idx_flat = idx8[:, :4, :n_top].reshape(-1)
    oc4 = oc16.reshape(DB, 4, 4, 4, HEAD_DIM)[:, np.arange(4), :, np.arange(4)].transpose(1, 0, 2, 3)
    qg = _pad_rows8(qs.reshape(DB, 4, 4, HEAD_DIM) * SCALE).astype(BF16)
    pos_p = jnp.arange(past).reshape(n_pages, PAGE_SIZE)
    tbp = bias_table(table, past - pos_p).reshape(4, 4, n_pages, PAGE_SIZE).transpose(2, 0, 1, 3)
    tbp = _pad_rows8(tbp).reshape(n_pages * 32, PAGE_SIZE)
    wb = win_state.shape[2]
    tw = _pad_rows8(bias_table(table, wb - jnp.arange(wb)).reshape(4, 4, wb)).reshape(32, wb)
    tw0 = _pad_rows8(jnp.broadcast_to(table[0].reshape(4, 4, 1), (4, 4, LANES))).reshape(32, LANES)
    g48 = jnp.concatenate([qgs[:, d_q + g * LANES:d_q + g * LANES + n_gate] for g in range(4)], axis=1) + bg[None]
    graw = jnp.pad(_pad_rows8(g48.reshape(DB, 4, 4, 3)), ((0, 0), (0, 0), (0, 0), (0, LANES - 3)))
    win_t = win_state.transpose(0, 1, 3, 4, 5, 2)
    s_new = css[:, d_kv:].reshape(DB, 2, 4, HEAD_DIM).transpose(0, 2, 1, 3)
    w_new = wns.reshape(DB, 2, 4, HEAD_DIM).transpose(0, 2, 1, 3)
    o4 = odd_decode_sel(pool_t, li, idx_flat, pt_flat, n_pages, win_t, qg, s_new, tbp, w_new, tw, tw0,
                        _pad_rows8(oc4), graw, n_top, past)
    attn_s = o4[:, :, :4, :].reshape(DB, d_q)
    okv_s = css.reshape(DB, 1, 4, 4, HEAD_DIM)
    keys_t = jnp.concatenate([win_t[li], wns.reshape(DB, 2, 4, HEAD_DIM, 1)], axis=-1)
    keep = min(WINDOW, wb + 1)
    win_s = keys_t[..., wb + 1 - keep:].transpose(0, 4, 1, 2, 3)
    return attn_p, attn_s, okv_p, okv_s, win_p, win_s


def kernel(x_prompt, x_sample, cache_even_kv, cache_even_logf, cache_odd_kv, state_odd_win, page_table, rel_table, norm_g, w_even_in, b_even_f, diff_lambda, diff_subln_g, w_even_out, w_odd_in, b_odd_gate, w_cmp_pos, b_cmp, w_cmp_out, w_odd_out, w_ffn_in, w_ffn_out):
    B, T, D = x_prompt.shape
    DB, S, _ = x_sample.shape
    depth = norm_g.shape[0]
    hp = x_prompt.reshape(B * T, D)
    hs = x_sample.reshape(DB * S, D)
    ekv_p, ekv_s, elf_p, elf_s = [], [], [], []
    okv_p, okv_s, win_p, win_s = [], [], [], []
    for layer in range(depth):
        g = norm_g[layer]
        li = layer // 2
        hp3, hs3 = hp.reshape(B, T, D), hs.reshape(DB, S, D)
        if layer % 2 == 0:
            a_p, a_s, kvp, kvs, lfp, lfs = even_layer(
                hp3, hs3, cache_even_kv, cache_even_logf, li, page_table, g[0], w_even_in[li], b_even_f[li],
                diff_lambda[li].astype(F32), diff_subln_g[li].astype(F32), rel_table, layer)
            w_o = w_even_out[li].astype(BF16)
            ekv_p.append(kvp); ekv_s.append(kvs); elf_p.append(lfp); elf_s.append(lfs)
        else:
            a_p, a_s, kvp, kvs, wp_, ws_ = odd_layer(
                hp3, hs3, cache_odd_kv, state_odd_win, li, page_table, g[0], w_odd_in[li], b_odd_gate[li],
                w_cmp_pos[li], b_cmp[li], w_cmp_out[li], rel_table)
            w_o = w_odd_out[li].astype(BF16)
            okv_p.append(kvp); okv_s.append(kvs); win_p.append(wp_); win_s.append(ws_)
        hp = (out_proj_residual if a_p.ndim == 2 else out_proj_residual_t)(a_p, w_o, g[1], hp)
        hs = out_proj_residual(a_s, w_o, g[1], hs)
        w1, w2 = ffn_weights(w_ffn_in[layer], w_ffn_out[layer], 256)
        hp = ffn_residual(hp, g[2], w1, w2, g[3])
        hs = ffn_residual(hs, g[2], w1, w2, g[3])
    return (hp.reshape(B, T, D), hs.reshape(DB, S, D), jnp.stack(ekv_p), jnp.stack(ekv_s), jnp.stack(elf_p), jnp.stack(elf_s),
            jnp.stack(okv_p), jnp.stack(okv_s), jnp.stack(win_p), jnp.stack(win_s))
```

```python
import functools
import math

import numpy as np
import jax
import jax.numpy as jnp
from jax import lax
from jax.experimental import pallas as pl
from jax.experimental.pallas import tpu as pltpu

F32 = jnp.float32
BF16 = jnp.bfloat16
I32 = jnp.int32

HEAD_DIM = 64
LANES = 128
PAGE_SIZE = 128
CMP_BLOCK = 32
CMP_STRIDE = 16
SEL_BLOCK = 64
N_SEL = 16
WINDOW = 512
N_BUCKETS = 32
MAX_DISTANCE = 1024
SCALE = HEAD_DIM ** -0.5
LOG2E = math.log2(math.e)
EPS = 1e-6
NEG = -1e30
FORCE = 1e9
PAD_SCORE = -3.0e38
VMEM_LIMIT = 56 * 1024 * 1024


def _cparams(n_axes):
    return pltpu.CompilerParams(dimension_semantics=("arbitrary",) * n_axes,
                                vmem_limit_bytes=VMEM_LIMIT)


def _dot(a, b):
    return jnp.dot(a, b, preferred_element_type=F32)


def _dot_nt(a, b):
    return lax.dot_general(a, b, (((1,), (1,)), ((), ())), preferred_element_type=F32)


def _split3(x):
    hi = x.astype(BF16)
    r1 = x - hi.astype(F32)
    mid = r1.astype(BF16)
    lo = (r1 - mid.astype(F32)).astype(BF16)
    return hi, mid, lo


def _dot_exact_rhs01(x, m01):
    hi, mid, lo = _split3(x)
    return _dot(hi, m01) + _dot(mid, m01) + _dot(lo, m01)


def _dot_exact_lhs01(m01, x):
    hi, mid, lo = _split3(x)
    return _dot(m01, hi) + _dot(m01, mid) + _dot(m01, lo)


def _lane_tile(x, n):
    return x if n == 1 else jnp.concatenate([x] * n, axis=1)


def _aligned(x, m):
    return x if isinstance(x, int) else pl.multiple_of(x, m)


def _shift_left_one_lane(x):
    n = x.shape[1] // LANES
    if n == 0:
        return pltpu.roll(x, x.shape[1] - 1, 1)
    keep = lax.broadcasted_iota(I32, (1, LANES), 1) < LANES - 1
    rolled = [pltpu.roll(x[:, c * LANES:(c + 1) * LANES], LANES - 1, 1) for c in range(n)]
    out = [jnp.where(keep, rolled[c], rolled[min(c + 1, n - 1)]) for c in range(n)]
    return out[0] if n == 1 else jnp.concatenate(out, axis=1)


def _silu(x):
    return x * (1.0 / (1.0 + jnp.exp(-x)))


def rel_bucket(dist):
    n = jnp.maximum(dist, 0)
    exact = N_BUCKETS // 2
    nf = jnp.maximum(n, 1).astype(F32)
    large = exact + (jnp.log(nf / exact) / math.log(MAX_DISTANCE / exact) * (N_BUCKETS - exact)).astype(I32)
    return jnp.where(n < exact, n, jnp.minimum(large, N_BUCKETS - 1))


def bias_table(table, dist):
    onehot = (rel_bucket(dist)[..., None] == jnp.arange(N_BUCKETS)).astype(F32)
    return jnp.einsum('...k,kh->h...', onehot, table.astype(F32), precision=lax.Precision.HIGHEST)


def _norm_rows(x, g):
    ms = jnp.mean(x * x, axis=-1, keepdims=True)
    return x * lax.rsqrt(ms + EPS) * g


def _proj_kernel(x_ref, g_ref, w_ref, o_ref, h_ref):
    @pl.when(pl.program_id(1) == 0)
    def _():
        h_ref[...] = _norm_rows(x_ref[...], g_ref[...]).astype(BF16)

    o_ref[...] = _dot(h_ref[...], w_ref[...]).astype(o_ref.dtype)


def norm_proj(x, g, w, tn, out_dtype=F32):
    M, D = x.shape
    N = w.shape[1]
    tm = min(M, 1024)
    return pl.pallas_call(
        _proj_kernel,
        grid=(M // tm, N // tn),
        in_specs=[pl.BlockSpec((tm, D), lambda i, j: (i, 0)),
                  pl.BlockSpec((1, D), lambda i, j: (0, 0)),
                  pl.BlockSpec((D, tn), lambda i, j: (0, j))],
        out_specs=pl.BlockSpec((tm, tn), lambda i, j: (i, j)),
        out_shape=jax.ShapeDtypeStruct((M, N), out_dtype),
        scratch_shapes=[pltpu.VMEM((tm, D), BF16)],
        compiler_params=_cparams(2),
        name="norm_proj",
    )(x, g.reshape(1, D), w)


def _proj_t_kernel(x_ref, g_ref, w_ref, o_ref, h_ref):
    @pl.when(pl.program_id(2) == 0)
    def _():
        h_ref[...] = _norm_rows(x_ref[...], g_ref[...]).astype(BF16)

    o_ref[...] = _dot_nt(w_ref[...], h_ref[...])


def norm_proj_t(x, g, w_t, tn):
    B, T, D = x.shape
    N = w_t.shape[0]
    tm = min(T, 1024)
    return pl.pallas_call(
        _proj_t_kernel,
        grid=(B, T // tm, N // tn),
        in_specs=[pl.BlockSpec((None, tm, D), lambda b, i, j: (b, i, 0)),
                  pl.BlockSpec((1, D), lambda b, i, j: (0, 0)),
                  pl.BlockSpec((tn, D), lambda b, i, j: (j, 0))],
        out_specs=pl.BlockSpec((None, tn, tm), lambda b, i, j: (b, j, i)),
        out_shape=jax.ShapeDtypeStruct((B, N, T), F32),
        scratch_shapes=[pltpu.VMEM((tm, D), BF16)],
        compiler_params=_cparams(3),
        name="norm_proj_t",
    )(x, g.reshape(1, D), w_t)


def _out_res_kernel(a_ref, w_ref, g_ref, r_ref, o_ref):
    y = _dot(a_ref[...].astype(BF16), w_ref[...])
    o_ref[...] = r_ref[...] + _norm_rows(y, g_ref[...])


def out_proj_residual(a, w, g, res):
    M, K = a.shape
    D = w.shape[1]
    tm = min(M, 512)
    return pl.pallas_call(
        _out_res_kernel,
        grid=(M // tm,),
        in_specs=[pl.BlockSpec((tm, K), lambda i: (i, 0)),
                  pl.BlockSpec((K, D), lambda i: (0, 0)),
                  pl.BlockSpec((1, D), lambda i: (0, 0)),
                  pl.BlockSpec((tm, D), lambda i: (i, 0))],
        out_specs=pl.BlockSpec((tm, D), lambda i: (i, 0)),
        out_shape=jax.ShapeDtypeStruct((M, D), F32),
        compiler_params=_cparams(1),
        name="out_proj_residual",
    )(a, w, g.reshape(1, D), res)


def _ffn_kernel(x_ref, g2_ref, wgu_ref, wo_ref, g3_ref, o_ref, h_ref, acc_ref):
    f = pl.program_id(1)
    tf = wo_ref.shape[0]

    @pl.when(f == 0)
    def _():
        h_ref[...] = _norm_rows(x_ref[...], g2_ref[...]).astype(BF16)
        acc_ref[...] = jnp.zeros_like(acc_ref)

    gu = _dot(h_ref[...], wgu_ref[...])
    acc_ref[...] += _dot((_silu(gu[:, :tf]) * gu[:, tf:]).astype(BF16), wo_ref[...])

    @pl.when(f == pl.num_programs(1) - 1)
    def _():
        o_ref[...] = x_ref[...] + _norm_rows(acc_ref[...], g3_ref[...])


def ffn_weights(w_in, w_out, tf):
    D = w_in.shape[0]
    Fdim = w_out.shape[0]
    nf = Fdim // tf
    gate = w_in[:, :Fdim].reshape(D, nf, tf)
    up = w_in[:, Fdim:].reshape(D, nf, tf)
    return jnp.concatenate([gate, up], axis=2).transpose(1, 0, 2).astype(BF16), w_out.astype(BF16)


def ffn_residual(x, g2, w_gu, w_out, g3):
    M, D = x.shape
    nf, _, tf2 = w_gu.shape
    tf = tf2 // 2
    tm = min(M, 1024)
    return pl.pallas_call(
        _ffn_kernel,
        grid=(M // tm, nf),
        in_specs=[pl.BlockSpec((tm, D), lambda i, f: (i, 0)),
                  pl.BlockSpec((1, D), lambda i, f: (0, 0)),
                  pl.BlockSpec((None, D, tf2), lambda i, f: (f, 0, 0)),
                  pl.BlockSpec((tf, D), lambda i, f: (f, 0)),
                  pl.BlockSpec((1, D), lambda i, f: (0, 0))],
        out_specs=pl.BlockSpec((tm, D), lambda i, f: (i, 0)),
        out_shape=jax.ShapeDtypeStruct((M, D), F32),
        scratch_shapes=[pltpu.VMEM((tm, D), BF16), pltpu.VMEM((tm, D), F32)],
        compiler_params=_cparams(2),
        name="ffn_residual",
    )(x, g2.reshape(1, D), w_gu, w_out, g3.reshape(1, D))


def _logf_kernel(f_ref, b_ref, tri_ref, lf_ref, cum_ref, carry_ref):
    @pl.when(pl.program_id(1) == 0)
    def _():
        carry_ref[...] = jnp.zeros_like(carry_ref)

    x = f_ref[...] + b_ref[...]
    lf = jnp.minimum(x, 0.0) - jnp.log(1.0 + jnp.exp(-jnp.abs(x)))
    lf_ref[...] = lf
    cum = _dot_exact_lhs01(tri_ref[...], lf) + carry_ref[...]
    cum_ref[...] = cum
    carry_ref[...] = cum[cum.shape[0] - 1:, :]


def logf_cumsum(fq, col_block, b_pad, tq):
    B, T, _ = fq.shape
    tri = jnp.asarray(np.tril(np.ones((tq, tq), np.float32)), BF16)
    return pl.pallas_call(
        _logf_kernel,
        grid=(B, T // tq),
        in_specs=[pl.BlockSpec((None, tq, LANES), lambda b, i: (b, i, col_block)),
                  pl.BlockSpec((1, LANES), lambda b, i: (0, 0)),
                  pl.BlockSpec((tq, tq), lambda b, i: (0, 0))],
        out_specs=[pl.BlockSpec((None, tq, LANES), lambda b, i: (b, i, 0)),
                   pl.BlockSpec((None, tq, LANES), lambda b, i: (b, i, 0))],
        out_shape=[jax.ShapeDtypeStruct((B, T, LANES), F32)] * 2,
        scratch_shapes=[pltpu.VMEM((1, LANES), F32)],
        compiler_params=_cparams(2),
        name="logf_cumsum",
    )(fq, b_pad, tri)


def _lambda_value(lp, lam_init):
    a = jnp.sum(lp[0:1, :] * lp[1:2, :], axis=-1, keepdims=True)
    b = jnp.sum(lp[2:3, :] * lp[3:4, :], axis=-1, keepdims=True)
    return jnp.exp(a) - jnp.exp(b) + lam_init


def _even_attn_kernel(lp_ref, q_ref, k_ref, v_ref, c_ref, tb_ref, sg_ref, o_ref, kb, vb, m_ref, l_ref, acc_ref,
                      *, tq, tk, n_fox, lam_init):
    c = pl.program_id(1)
    qi = pl.program_id(2)

    @pl.when(qi == 0)
    def _():
        kb[...] = k_ref[...].astype(BF16)
        vb[...] = v_ref[...].astype(BF16)

    lo = lax.broadcasted_iota(I32, (1, LANES), 1) < HEAD_DIM
    q = q_ref[...] * SCALE
    qh = (jnp.where(lo, q, 0.0).astype(BF16), jnp.where(lo, 0.0, q).astype(BF16))
    t0 = qi * tq
    n_full = t0 // tk
    sub = tk // tq
    nd = tb_ref.shape[0]

    def tile(kt, bias_fn, first, diag):
        start = _aligned(kt * tk, tk)
        k = kb[:, pl.ds(start, tk)]
        v = vb[:, pl.ds(start, tk)]
        if diag:
            row = t0 + lax.broadcasted_iota(I32, (tq, tk), 0)
            col = start + lax.broadcasted_iota(I32, (tq, tk), 1)
            causal = col <= row
        for h in range(2):
            s = _dot(qh[h], k) + bias_fn(h, kt, start)
            if diag:
                s = jnp.where(causal, s, NEG)
            rmax = jnp.broadcast_to(jnp.max(s, axis=-1, keepdims=True), (tq, LANES))
            if first:
                m_new = rmax
            else:
                m_prev = m_ref[h]
                m_new = jnp.maximum(m_prev, rmax)
                alpha = jnp.exp(m_prev - m_new)
            p = jnp.exp(s - _lane_tile(m_new, tk // LANES))
            rsum = jnp.broadcast_to(jnp.sum(p, axis=-1, keepdims=True), (tq, LANES))
            pv = _dot_nt(p.astype(BF16), v)
            if first:
                l_ref[h] = rsum
                acc_ref[h] = pv
            else:
                l_ref[h] = alpha * l_ref[h] + rsum
                acc_ref[h] = alpha * acc_ref[h] + pv
            m_ref[h] = m_new

    def run(bias_fn):
        @pl.when(n_full == 0)
        def _():
            tile(0, bias_fn, True, True)

        @pl.when(n_full > 0)
        def _():
            tile(0, bias_fn, True, False)

            def body(kt, carry):
                tile(kt, bias_fn, False, False)
                return carry

            lax.fori_loop(1, n_full, body, 0)
            tile(n_full, bias_fn, False, True)

    @pl.when(c < n_fox)
    def _():
        def bias_fn(h, kt, start):
            return -c_ref[h:h + 1, pl.ds(start, tk)]

        run(bias_fn)
        o_ref[...] = jnp.where(lo, acc_ref[0] / l_ref[0], acc_ref[1] / l_ref[1])

    @pl.when(c >= n_fox)
    def _():
        def bias_fn(h, kt, start):
            d0 = qi - kt * sub
            return jnp.concatenate([tb_ref[jnp.clip(d0 - cc, 0, nd - 1)] for cc in range(sub)], axis=1)

        run(bias_fn)
        lam = _lambda_value(lp_ref[...], lam_init)
        o = acc_ref[0] / l_ref[0] - lam * (acc_ref[1] / l_ref[1])
        ms = jnp.mean(o * o, axis=-1, keepdims=True)
        o_ref[...] = o * lax.rsqrt(ms + EPS) * sg_ref[...] * (1.0 - lam_init)


def even_prompt_attention(q_arr, kv_t, cum_pairs, toe, lam_p, subln_g, lam_init, tq, tk):
    B, T, _ = q_arr.shape
    n_fox = 4
    n_tiles = 8
    nd = toe.shape[1]
    kern = functools.partial(_even_attn_kernel, tq=tq, tk=tk, n_fox=n_fox, lam_init=lam_init)
    return pl.pallas_call(
        kern,
        grid=(B, n_tiles, T // tq),
        in_specs=[pl.BlockSpec((4, HEAD_DIM), lambda b, c, i: (0, 0)),
                  pl.BlockSpec((None, tq, LANES), lambda b, c, i: (b, i, c)),
                  pl.BlockSpec((None, LANES, T), lambda b, c, i: (b, c, 0)),
                  pl.BlockSpec((None, LANES, T), lambda b, c, i: (b, n_tiles + c, 0)),
                  pl.BlockSpec((None, None, 2, T), lambda b, c, i: (b, jnp.minimum(c, n_fox - 1), 0, 0)),
                  pl.BlockSpec((None, nd, tq, tq), lambda b, c, i: (jnp.maximum(c - n_fox, 0), 0, 0, 0)),
                  pl.BlockSpec((1, LANES), lambda b, c, i: (0, 0))],
        out_specs=pl.BlockSpec((None, tq, LANES), lambda b, c, i: (b, i, c)),
        out_shape=jax.ShapeDtypeStruct((B, T, n_tiles * LANES), F32),
        scratch_shapes=[pltpu.VMEM((LANES, T), BF16), pltpu.VMEM((LANES, T), BF16),
                        pltpu.VMEM((2, tq, LANES), F32), pltpu.VMEM((2, tq, LANES), F32), pltpu.VMEM((2, tq, LANES), F32)],
        compiler_params=_cparams(3),
        name="even_prompt_attention",
    )(lam_p, q_arr, kv_t, kv_t, cum_pairs, toe, subln_g.reshape(1, LANES))


def _even_attn_t_kernel(lp_ref, q_ref, k_ref, v_ref, c_ref, tbt_ref, sg_ref, o_ref, kb, vb, cb, m_ref, acc_ref,
                        *, tq, tk, n_fox, lam_init):
    c = pl.program_id(1)
    qi = pl.program_id(2)
    T = k_ref.shape[1]
    n_val = v_ref.shape[0]

    @pl.when(qi == 0)
    def _():
        kb[...] = k_ref[...].T.astype(BF16)
        vb[0:n_val, :] = v_ref[...].astype(BF16)
        vb[n_val:, :] = jnp.ones((16, T), BF16)
        for h in range(2):
            cb[h] = jnp.broadcast_to(c_ref[h:h + 1, :] * (-LOG2E), (LANES, T)).T

    lo = lax.broadcasted_iota(I32, (1, LANES), 1) < HEAD_DIM
    q = q_ref[...] * (SCALE * LOG2E)
    qs = jnp.concatenate([jnp.where(lo, q, 0.0), jnp.where(lo, 0.0, q)], axis=0).astype(BF16)
    t0 = qi * tq
    tpos = t0 + lax.broadcasted_iota(I32, (1, tq), 1)
    n_full = t0 // tk
    sub = tk // tq
    nd = tbt_ref.shape[0]

    def tile(kt, bias_fn, first, diag):
        start = _aligned(kt * tk, tk)
        s = _dot_nt(kb[pl.ds(start, tk), :], qs) + bias_fn(kt, start)
        if diag:
            visible = start + lax.broadcasted_iota(I32, (tk, 1), 0) <= tpos
            s = s + _lane_tile(jnp.where(visible, 0.0, NEG), 2)
        s_max = jnp.max(s, axis=0, keepdims=True)
        if first:
            m_new = s_max
        else:
            m_prev = m_ref[...]
            m_new = jnp.maximum(m_prev, s_max)
        p = jnp.exp2((s - m_new).astype(BF16))
        pv = _dot(vb[:, pl.ds(start, tk)], p)
        acc_ref[...] = pv if first else jnp.exp2(m_prev - m_new) * acc_ref[...] + pv
        m_ref[...] = m_new

    def run(bias_fn):
        @pl.when(n_full == 0)
        def _():
            tile(0, bias_fn, True, True)

        @pl.when(n_full > 0)
        def _():
            tile(0, bias_fn, True, False)

            def body(kt, carry):
                tile(kt, bias_fn, False, False)
                return carry

            lax.fori_loop(1, n_full, body, 0)
            tile(n_full, bias_fn, False, True)

    @pl.when(c < n_fox)
    def _():
        def bias_fn(kt, start):
            return jnp.concatenate([_lane_tile(cb[h, pl.ds(start, tk), :], tq // LANES) for h in range(2)], axis=1)

        run(bias_fn)
        acc = acc_ref[...]
        o_ref[0:HEAD_DIM, :] = acc[0:HEAD_DIM, 0:tq] / acc[n_val:n_val + 1, 0:tq]
        o_ref[HEAD_DIM:, :] = acc[HEAD_DIM:n_val, tq:] / acc[n_val:n_val + 1, tq:]

    @pl.when(c >= n_fox)
    def _():
        def bias_fn(kt, start):
            d0 = qi - kt * sub
            rows = jnp.concatenate([tbt_ref[jnp.clip(d0 - cc, 0, nd - 1)] for cc in range(sub)], axis=0)
            return _lane_tile(rows, 2)

        run(bias_fn)
        acc = acc_ref[...]
        lam = _lambda_value(lp_ref[...], lam_init)
        o = acc[0:n_val, 0:tq] / acc[n_val:n_val + 1, 0:tq] - lam * (acc[0:n_val, tq:] / acc[n_val:n_val + 1, tq:])
        ms = jnp.mean(o * o, axis=0, keepdims=True)
        o_ref[...] = o * lax.rsqrt(ms + EPS) * sg_ref[...] * (1.0 - lam_init)


def even_prompt_attention_t(q_arr, kv_t, cum_pairs, toe_t, lam_p, subln_g, lam_init, tq, tk):
    B, T, _ = q_arr.shape
    n_fox = 4
    n_tiles = 8
    nd = toe_t.shape[1]
    assert tq % LANES == 0 and tk % tq == 0 and T % tk == 0
    kern = functools.partial(_even_attn_t_kernel, tq=tq, tk=tk, n_fox=n_fox, lam_init=lam_init)
    return pl.pallas_call(
        kern,
        grid=(B, n_tiles, T // tq),
        in_specs=[pl.BlockSpec((4, HEAD_DIM), lambda b, c, i: (0, 0)),
                  pl.BlockSpec((None, tq, LANES), lambda b, c, i: (b, i, c)),
                  pl.BlockSpec((None, LANES, T), lambda b, c, i: (b, c, 0)),
                  pl.BlockSpec((None, LANES, T), lambda b, c, i: (b, n_tiles + c, 0)),
                  pl.BlockSpec((None, None, 2, T), lambda b, c, i: (b, jnp.minimum(c, n_fox - 1), 0, 0)),
                  pl.BlockSpec((None, nd, tq, tq), lambda b, c, i: (jnp.maximum(c - n_fox, 0), 0, 0, 0)),
                  pl.BlockSpec((LANES, 1), lambda b, c, i: (0, 0))],
        out_specs=pl.BlockSpec((None, LANES, tq), lambda b, c, i: (b, c, i)),
        out_shape=jax.ShapeDtypeStruct((B, n_tiles * LANES, T), F32),
        scratch_shapes=[pltpu.VMEM((T, LANES), BF16), pltpu.VMEM((LANES + 16, T), BF16), pltpu.VMEM((2, T, LANES), F32),
                        pltpu.VMEM((1, 2 * tq), F32), pltpu.VMEM((LANES + 16, 2 * tq), F32)],
        compiler_params=_cparams(3),
        name="even_prompt_attention_t",
    )(lam_p, q_arr, kv_t, kv_t, cum_pairs, toe_t, subln_g.reshape(LANES, 1))


def toeplitz_dist(tq, nd):
    i = np.arange(tq)[:, None]
    j = np.arange(tq)[None, :]
    return jnp.asarray(np.maximum(np.arange(nd)[:, None, None] * tq + (i - j)[None], 0), I32)


def _pool_weights_t(w_pos_rows):
    reps = LANES // CMP_STRIDE
    w0 = jnp.tile(w_pos_rows[:CMP_STRIDE].T, (1, reps))
    w1 = jnp.tile(w_pos_rows[CMP_STRIDE:].T, (1, reps))
    return w0.astype(F32), w1.astype(F32)


def _group_sum_stack(n_tiles):
    spt = LANES // CMP_STRIDE
    assert n_tiles * spt <= LANES
    m = np.zeros((n_tiles, 2 * LANES, 2 * LANES), np.float32)
    pos = np.arange(LANES)
    for i in range(n_tiles):
        m[i, pos, i * spt + pos // CMP_STRIDE] = 1.0
        m[i, LANES + pos, LANES + i * spt + pos // CMP_STRIDE] = 1.0
    return m


def _pool2(x, w0, w1, gmat2, two_pass):
    y = jnp.concatenate([x * w0, x * w1], axis=1)
    hi = y.astype(BF16)
    out = _dot(hi, gmat2)
    if two_pass:
        out = out + _dot((y - hi.astype(F32)).astype(BF16), gmat2)
    return out


def _cmp_finish_t(p0, p1, b_col, w0t, w1t):
    half = w0t.shape[0]
    z = _silu(p0 + _shift_left_one_lane(p1) + b_col)
    ck = _dot(w0t, z[:half].astype(BF16))
    cv = _dot(w1t, z[half:].astype(BF16))
    return ck, cv


def _cmp_build_kernel(c_ref, w0_ref, w1_ref, g_ref, b_ref, w0t_ref, w1t_ref, ck_ref, cv_ref):
    T = c_ref.shape[1]
    nb = ck_ref.shape[1]
    w0 = w0_ref[...]
    w1 = w1_ref[...]
    acc = None
    for lt in range(T // LANES):
        a = _pool2(c_ref[:, lt * LANES:(lt + 1) * LANES], w0, w1, g_ref[lt], True)
        acc = a if acc is None else acc + a
    ck, cv = _cmp_finish_t(acc[:, 0:nb], acc[:, LANES:LANES + nb], b_ref[...], w0t_ref[...], w1t_ref[...])
    ck_ref[...] = ck
    cv_ref[...] = cv


def cmp_build(cs_t, w0, w1, b_col, w0t, w1t):
    B, _, T = cs_t.shape
    nb = T // CMP_STRIDE
    width = w0.shape[0]
    half = width // 2
    n_tiles = T // LANES
    gmat = jnp.asarray(_group_sum_stack(n_tiles), BF16)
    const = lambda shape: pl.BlockSpec(shape, lambda b: (0,) * len(shape))
    return pl.pallas_call(
        _cmp_build_kernel,
        grid=(B,),
        in_specs=[pl.BlockSpec((None, width, T), lambda b: (b, 0, 0)),
                  const((width, LANES)), const((width, LANES)), const((n_tiles, 2 * LANES, 2 * LANES)), const((width, 1)),
                  const((half, half)), const((half, half))],
        out_specs=[pl.BlockSpec((None, half, nb), lambda b: (b, 0, 0))] * 2,
        out_shape=[jax.ShapeDtypeStruct((B, half, nb), F32)] * 2,
        compiler_params=_cparams(1),
        name="cmp_build",
    )(cs_t, w0, w1, gmat, b_col, w0t, w1t)


def _nsa_kernel(q_ref, ck_ref, cv_ref, sk_ref, sv_ref, wk_ref, wv_ref, gt_ref, bg_ref, cb_ref, tb_ref, cov_ref, ex_ref,
                o_ref, sk2, sv2, wk2, wv2, ck2, cv2, madd, p_ref, m_ref, l_ref, a_ref, acc_ref,
                *, tq, tk, n_blk, n_selb, n_top, win_chunks):
    qi = pl.program_id(2)

    @pl.when(qi == 0)
    def _():
        for src, dst in ((sk_ref, sk2), (sv_ref, sv2), (wk_ref, wk2), (wv_ref, wv2), (ck_ref, ck2), (cv_ref, cv2)):
            x = src[...].astype(BF16)
            dst[0:HEAD_DIM, :] = x
            dst[HEAD_DIM:, :] = x

    T = sk_ref.shape[1]
    nb = ck_ref.shape[1]
    nd = tb_ref.shape[1]
    sub = tk // tq
    t0 = qi * tq
    lane = lax.broadcasted_iota(I32, (1, LANES), 1)
    lo = lane < HEAD_DIM
    tcol = t0 + lax.broadcasted_iota(I32, (tq, 1), 0)
    q = q_ref[...] * SCALE
    parts = []
    for j in range(4):
        tile = q[:, (j // 2) * LANES:(j // 2 + 1) * LANES]
        parts.append(jnp.where(lo if j % 2 == 0 else jnp.logical_not(lo), tile, 0.0))
    qs = jnp.concatenate(parts, axis=0).astype(BF16)
    gates = 1.0 / (1.0 + jnp.exp(-(gt_ref[...] + bg_ref[...])))

    n_idx = lax.broadcasted_iota(I32, (1, nb), 1)
    validc = jnp.logical_and(n_idx * CMP_STRIDE + (CMP_BLOCK - 1) <= tcol, n_idx < n_blk)
    sc = _dot(qs, ck2[...])
    o_c = []
    psum = jnp.zeros((tq, nb), F32)
    for j in range(4):
        s = jnp.where(validc, sc[j * tq:(j + 1) * tq] + cb_ref[j], NEG)
        m = jnp.max(s, axis=-1, keepdims=True)
        e = jnp.where(validc, jnp.exp(s - m), 0.0)
        l = jnp.sum(e, axis=-1, keepdims=True)
        p = e / jnp.where(l > 0.0, l, 1.0)
        psum = psum + p
        o_c.append(_dot_nt(p.astype(BF16), cv2[...]))

    score = _dot_exact_rhs01(psum, cov_ref[...])
    cur = tcol // SEL_BLOCK
    forced = jnp.logical_or(lane == 0, jnp.logical_or(lane == cur, lane == cur - 1))
    score = jnp.where(forced, FORCE, score)
    score = jnp.where(lane * SEL_BLOCK <= tcol, score, NEG)
    score = jnp.where(lane < n_selb, score, PAD_SCORE)
    n_selp = -(-n_selb // 8) * 8
    sc_t = score.T[:n_selp]
    jrow = lax.broadcasted_iota(I32, (n_selp, 1), 0)
    rank = jnp.zeros((n_selp, tq), F32)
    for i in range(n_selb):
        ri = sc_t[i:i + 1, :]
        tie = jnp.where(i < jrow, 1.0, 0.0)
        rank = rank + jnp.where(ri > sc_t, 1.0, jnp.where(ri == sc_t, tie, 0.0))
    sel_t = jnp.where(rank < n_top, 1.0, 0.0)
    if n_selp < LANES:
        sel_t = jnp.concatenate([sel_t, jnp.zeros((LANES - n_selp, tq), F32)], axis=0)
    chosen = _dot(sel_t.T.astype(BF16), ex_ref[...])
    kpos = lax.broadcasted_iota(I32, (1, T), 1)
    madd[...] = jnp.where(jnp.logical_and(chosen > 0.5, kpos <= tcol), 0.0, NEG)

    def softmax_rows(j, s, first):
        rows = slice(j * tq, (j + 1) * tq)
        w = s.shape[1]
        rmax = jnp.broadcast_to(jnp.max(s, axis=-1, keepdims=True), (tq, LANES))
        if first:
            m_new = rmax
        else:
            m_prev = m_ref[rows]
            m_new = jnp.maximum(m_prev, rmax)
            alpha = jnp.exp(m_prev - m_new)
            a_ref[rows] = alpha
        p = jnp.exp(s - _lane_tile(m_new, w // LANES))
        rsum = jnp.broadcast_to(jnp.sum(p, axis=-1, keepdims=True), (tq, LANES))
        l_ref[rows] = rsum if first else alpha * l_ref[rows] + rsum
        m_ref[rows] = m_new
        p_ref[rows, 0:w] = p.astype(BF16)

    def toe_bias(j, d0, n_chunks):
        return jnp.concatenate([tb_ref[j, jnp.clip(d0 - cc, 0, nd - 1)] for cc in range(n_chunks)], axis=1)

    kt_d = qi // sub

    def sel_tile(kt, first):
        start = _aligned(kt * tk, tk)
        mask = madd[:, pl.ds(start, tk)]
        s4 = _dot(qs, sk2[:, pl.ds(start, tk)])
        for j in range(4):
            softmax_rows(j, s4[j * tq:(j + 1) * tq] + toe_bias(j, qi - kt * sub, sub) + mask, first)
        pv = _dot_nt(p_ref[:, 0:tk], sv2[:, pl.ds(start, tk)])
        acc_ref[...] = pv if first else a_ref[...] * acc_ref[...] + pv

    sel_tile(kt_d, True)

    def sel_body(i, carry):
        sel_tile(kt_d - 1 - i, False)
        return carry

    lax.fori_loop(0, kt_d, sel_body, 0)
    o_s = acc_ref[...] / l_ref[...]

    ww = win_chunks * LANES
    w0 = jnp.maximum(qi - (win_chunks - 1), 0)
    wstart = pl.multiple_of(w0 * tq, tq)
    dist = tcol - (wstart + lax.broadcasted_iota(I32, (1, ww), 1))
    wmask = jnp.where(jnp.logical_and(dist >= 0, dist <= WINDOW), 0.0, NEG)
    s4 = _dot(qs, wk2[:, pl.ds(wstart, ww)])
    for j in range(4):
        softmax_rows(j, s4[j * tq:(j + 1) * tq] + toe_bias(j, qi - w0, win_chunks) + wmask, True)
    o_w = _dot_nt(p_ref[:, 0:ww], wv2[:, pl.ds(wstart, ww)]) / l_ref[...]

    outs = []
    for j in range(4):
        rows = slice(j * tq, (j + 1) * tq)
        outs.append(gates[:, 3 * j:3 * j + 1] * o_c[j] + gates[:, 3 * j + 1:3 * j + 2] * o_s[rows]
                    + gates[:, 3 * j + 2:3 * j + 3] * o_w[rows])
    o_ref[:, 0:LANES] = jnp.where(lo, outs[0], outs[1])
    o_ref[:, LANES:2 * LANES] = jnp.where(lo, outs[2], outs[3])


def nsa_prompt_attention(qg_arr, ck_t, cv_t, cs_t, w_t, bg_pad, cbias, toe, cover, expand, tq, tk):
    B, T, _ = qg_arr.shape
    nb = ck_t.shape[2]
    nd = toe.shape[1]
    n_selb = T // SEL_BLOCK
    win_chunks = WINDOW // tq + 1
    ww = win_chunks * LANES
    assert tq == LANES and T >= ww and tk % tq == 0 and T % tk == 0
    kern = functools.partial(_nsa_kernel, tq=tq, tk=tk, n_blk=nb - 1, n_selb=n_selb, n_top=min(N_SEL, n_selb),
                             win_chunks=win_chunks)
    rows64 = lambda blk0: pl.BlockSpec((None, HEAD_DIM, T), lambda g, b, i: (b, blk0 + g, 0))
    wide = max(tk, ww)
    return pl.pallas_call(
        kern,
        grid=(4, B, T // tq),
        in_specs=[pl.BlockSpec((None, tq, 2 * LANES), lambda g, b, i: (b, i, g)),
                  pl.BlockSpec((None, HEAD_DIM, nb), lambda g, b, i: (b, g, 0)),
                  pl.BlockSpec((None, HEAD_DIM, nb), lambda g, b, i: (b, g, 0)),
                  rows64(8), rows64(12), rows64(0), rows64(4),
                  pl.BlockSpec((None, tq, LANES), lambda g, b, i: (b, i, 8 + g)),
                  pl.BlockSpec((None, 1, LANES), lambda g, b, i: (g, 0, 0)),
                  pl.BlockSpec((4, tq, nb), lambda g, b, i: (g, i, 0)),
                  pl.BlockSpec((4, nd, tq, tq), lambda g, b, i: (g, 0, 0, 0)),
                  pl.BlockSpec((nb, LANES), lambda g, b, i: (0, 0)),
                  pl.BlockSpec((LANES, T), lambda g, b, i: (0, 0))],
        out_specs=pl.BlockSpec((None, tq, 2 * LANES), lambda g, b, i: (b, i, g)),
        out_shape=jax.ShapeDtypeStruct((B, T, 8 * LANES), F32),
        scratch_shapes=[pltpu.VMEM((LANES, T), BF16)] * 4 + [pltpu.VMEM((LANES, nb), BF16)] * 2
                       + [pltpu.VMEM((tq, T), F32), pltpu.VMEM((4 * tq, wide), BF16)]
                       + [pltpu.VMEM((4 * tq, LANES), F32)] * 4,
        compiler_params=_cparams(3),
        name="nsa_prompt_attention",
    )(qg_arr, ck_t, cv_t, cs_t, cs_t, w_t, w_t, qg_arr, bg_pad, cbias, toe, cover, expand)


def _nsa_t_kernel(q_ref, ckt_ref, cvt_ref, sk_ref, svt_ref, wk_ref, wvt_ref, gt_ref, bg_ref, cbt_ref, tbt_ref, covt_ref, ext_ref,
                  o_ref, svb, wvb, ck2, cvb, *, tq, tk, n_blk, n_selb, n_top, win_chunks, n_chains):
    qi = pl.program_id(2)

    @pl.when(qi == 0)
    def _():
        ones = jnp.ones((16, svt_ref.shape[1]), BF16)
        svb[0:HEAD_DIM, :] = svt_ref[...].astype(BF16)
        svb[HEAD_DIM:, :] = ones
        wvb[0:HEAD_DIM, :] = wvt_ref[...].astype(BF16)
        wvb[HEAD_DIM:, :] = ones
        cvb[...] = cvt_ref[...].astype(BF16)
        ckt = ckt_ref[...]
        ck2[...] = jnp.concatenate([ckt, ckt], axis=0).T.astype(BF16)

    T = svt_ref.shape[1]
    nb = ckt_ref.shape[1]
    nd = tbt_ref.shape[1]
    sub = tk // tq
    t0 = qi * tq
    lane = lax.broadcasted_iota(I32, (1, LANES), 1)
    lo = lane < HEAD_DIM
    tpos = t0 + lax.broadcasted_iota(I32, (1, tq), 1)
    q = q_ref[...]
    zero = jnp.zeros((), BF16)
    parts = []
    for j in range(4):
        tile = q[:, (j // 2) * LANES:(j // 2 + 1) * LANES]
        parts.append(jnp.where(lo if j % 2 == 0 else jnp.logical_not(lo), tile, zero))
    qs = jnp.concatenate(parts, axis=0)
    gt = 1.0 / (1.0 + jnp.exp(-(gt_ref[...] + bg_ref[...])))

    def gate(r):
        return jnp.concatenate([gt[3 * j + r:3 * j + r + 1, :] for j in range(4)], axis=1)

    n_col = lax.broadcasted_iota(I32, (nb, 1), 0)
    validc = jnp.logical_and(n_col * CMP_STRIDE + (CMP_BLOCK - 1) <= tpos, n_col < n_blk)
    validc4 = _lane_tile(validc, 4)
    sc = _dot_nt(ck2[...], qs) + jnp.concatenate([cbt_ref[j] for j in range(4)], axis=1)
    sc = jnp.where(validc4, sc, NEG)
    m = jnp.max(sc, axis=0, keepdims=True)
    e = jnp.where(validc4, jnp.exp2(sc - m), 0.0)
    l = jnp.sum(e, axis=0, keepdims=True)
    p = e / jnp.where(l > 0.0, l, 1.0)
    o_c = _dot(cvb[...], p.astype(BF16))
    psum = p[:, 0:tq] + p[:, tq:2 * tq] + p[:, 2 * tq:3 * tq] + p[:, 3 * tq:4 * tq]

    score = _dot_exact_lhs01(covt_ref[...], psum)
    jrow = lax.broadcasted_iota(I32, (LANES, 1), 0)
    cur = tpos // SEL_BLOCK
    forced = jnp.logical_or(jrow == 0, jnp.logical_or(jrow == cur, jrow == cur - 1))
    score = jnp.where(forced, FORCE, score)
    score = jnp.where(jrow * SEL_BLOCK <= tpos, score, NEG)
    n_selp = -(-n_selb // 8) * 8
    sc_t = score[:n_selp]
    jr = jrow[:n_selp]
    rank = jnp.zeros((n_selp, tq), F32)
    for i in range(n_selb):
        ri = sc_t[i:i + 1, :]
        tie = jnp.where(i < jr, 1.0, 0.0)
        rank = rank + jnp.where(ri > sc_t, 1.0, jnp.where(ri == sc_t, tie, 0.0))
    sel_t = jnp.where(jnp.logical_and(rank < n_top, jr < n_selb), 1.0, 0.0)
    if n_selp < LANES:
        sel_t = jnp.concatenate([sel_t, jnp.zeros((LANES - n_selp, tq), F32)], axis=0)
    sel_b = sel_t.astype(BF16)

    hpc = 4 // n_chains
    qs_c = [qs[c * hpc * tq:(c + 1) * hpc * tq] for c in range(n_chains)]

    def toe_bias(c, d0, n_chunks):
        cols = [jnp.concatenate([tbt_ref[j, jnp.clip(d0 - cc, 0, nd - 1)] for cc in range(n_chunks)], axis=0)
                for j in range(c * hpc, (c + 1) * hpc)]
        return cols[0] if hpc == 1 else jnp.concatenate(cols, axis=1)

    def update(s, vt, state):
        s_max = jnp.max(s, axis=0, keepdims=True)
        if state is None:
            m_new = s_max
        else:
            m_prev, acc_prev = state
            m_new = jnp.maximum(m_prev, s_max)
        p = jnp.exp2((s - m_new).astype(BF16))
        pv = _dot(vt, p)
        if state is None:
            return m_new, pv
        return m_new, jnp.exp2(m_prev - m_new) * acc_prev + pv

    def attend(k_ref, v_ref, start, width, d0, mask, states):
        k = k_ref[pl.ds(start, width), :]
        v = v_ref[:, pl.ds(start, width)]
        mask_c = _lane_tile(mask, hpc)
        cs = range(n_chains)
        raws = [_dot_nt(k, qs_c[c]) for c in cs]
        ss = [raws[c] + toe_bias(c, d0, width // tq) + mask_c for c in cs]
        ms = [jnp.max(ss[c], axis=0, keepdims=True) for c in cs]
        if states is not None:
            ms = [jnp.maximum(states[c][0], ms[c]) for c in cs]
        ps = [jnp.exp2((ss[c] - ms[c]).astype(BF16)) for c in cs]
        pvs = [_dot(v, ps[c]) for c in cs]
        if states is None:
            return tuple((ms[c], pvs[c]) for c in cs)
        return tuple((ms[c], jnp.exp2(states[c][0] - ms[c]) * states[c][1] + pvs[c]) for c in cs)

    kt_d = qi // sub

    def sel_tile(kt, states):
        start = _aligned(kt * tk, tk)
        chosen = _dot(ext_ref[pl.ds(start, tk), :], sel_b) > 0.5
        if states is None:
            chosen = jnp.logical_and(chosen, start + lax.broadcasted_iota(I32, (tk, 1), 0) <= tpos)
        return attend(sk_ref, svb, start, tk, qi - kt * sub, jnp.where(chosen, 0.0, NEG), states)

    states = lax.fori_loop(0, kt_d, lambda i, st: sel_tile(kt_d - 1 - i, st), sel_tile(kt_d, None))
    o_s = jnp.concatenate([acc[0:HEAD_DIM] / acc[HEAD_DIM:HEAD_DIM + 1] for _, acc in states], axis=1)

    ww = win_chunks * tq
    w0 = jnp.maximum(qi - (win_chunks - 1), 0)
    wstart = pl.multiple_of(w0 * tq, tq)
    dist = tpos - (wstart + lax.broadcasted_iota(I32, (ww, 1), 0))
    wmask = jnp.where(jnp.logical_and(dist >= 0, dist <= WINDOW), 0.0, NEG)
    states = attend(wk_ref, wvb, wstart, ww, qi - w0, wmask, None)
    o_w = jnp.concatenate([acc[0:HEAD_DIM] / acc[HEAD_DIM:HEAD_DIM + 1] for _, acc in states], axis=1)

    o = gate(0) * o_c + gate(1) * o_s + gate(2) * o_w
    for j in range(4):
        o_ref[j * HEAD_DIM:(j + 1) * HEAD_DIM, :] = o[:, j * tq:(j + 1) * tq]


def nsa_prompt_attention_t(qk_arr, ck_t, cv_t, cs_t, wg_t, bg_col, cbias_t, toe_t, cover_t, expand_t, tq, tk):
    B, T, _ = qk_arr.shape
    nb = ck_t.shape[2]
    nd = toe_t.shape[1]
    n_selb = T // SEL_BLOCK
    win_chunks = WINDOW // tq + 1
    assert tq == LANES and T >= win_chunks * tq and tk % tq == 0 and T % tk == 0 and nb <= LANES
    kern = functools.partial(_nsa_t_kernel, tq=tq, tk=tk, n_blk=nb - 1, n_selb=n_selb, n_top=min(N_SEL, n_selb),
                             win_chunks=win_chunks, n_chains=1)
    return pl.pallas_call(
        kern,
        grid=(4, B, T // tq),
        in_specs=[pl.BlockSpec((None, tq, 2 * LANES), lambda g, b, i: (b, i, g)),
                  pl.BlockSpec((None, HEAD_DIM, nb), lambda g, b, i: (b, g, 0)),
                  pl.BlockSpec((None, HEAD_DIM, nb), lambda g, b, i: (b, g, 0)),
                  pl.BlockSpec((None, T, LANES), lambda g, b, i: (b, 0, 8 + g)),
                  pl.BlockSpec((None, HEAD_DIM, T), lambda g, b, i: (b, 12 + g, 0)),
                  pl.BlockSpec((None, T, LANES), lambda g, b, i: (b, 0, 12 + g)),
                  pl.BlockSpec((None, HEAD_DIM, T), lambda g, b, i: (b, 4 + g, 0)),
                  pl.BlockSpec((None, 16, tq), lambda g, b, i: (b, 32 + g, i)),
                  pl.BlockSpec((None, 16, 1), lambda g, b, i: (g, 0, 0)),
                  pl.BlockSpec((4, nb, tq), lambda g, b, i: (g, 0, i)),
                  pl.BlockSpec((4, nd, tq, tq), lambda g, b, i: (g, 0, 0, 0)),
                  pl.BlockSpec((LANES, nb), lambda g, b, i: (0, 0)),
                  pl.BlockSpec((T, LANES), lambda g, b, i: (0, 0))],
        out_specs=pl.BlockSpec((None, 4 * HEAD_DIM, tq), lambda g, b, i: (b, g, i)),
        out_shape=jax.ShapeDtypeStruct((B, 16 * HEAD_DIM, T), F32),
        scratch_shapes=[pltpu.VMEM((HEAD_DIM + 16, T), BF16), pltpu.VMEM((HEAD_DIM + 16, T), BF16),
                        pltpu.VMEM((nb, LANES), BF16), pltpu.VMEM((HEAD_DIM, nb), BF16)],
        compiler_params=_cparams(3),
        name="nsa_prompt_attention_t",
    )(qk_arr, ck_t, cv_t, qk_arr, cs_t, qk_arr, wg_t, wg_t, bg_col, cbias_t, toe_t, cover_t, expand_t)


def _out_res_t_kernel(a_ref, w_ref, g_ref, r_ref, o_ref):
    y = _dot(a_ref[...].T.astype(BF16), w_ref[...])
    o_ref[...] = r_ref[...] + _norm_rows(y, g_ref[...])


def out_proj_residual_t(a_t, w, g, res):
    B, K, T = a_t.shape
    D = w.shape[1]
    tm = min(T, 512)
    nt = T // tm
    return pl.pallas_call(
        _out_res_t_kernel,
        grid=(B, nt),
        in_specs=[pl.BlockSpec((None, K, tm), lambda b, i: (b, 0, i)),
                  pl.BlockSpec((K, D), lambda b, i: (0, 0)),
                  pl.BlockSpec((1, D), lambda b, i: (0, 0)),
                  pl.BlockSpec((tm, D), lambda b, i: (b * nt + i, 0))],
        out_specs=pl.BlockSpec((tm, D), lambda b, i: (b * nt + i, 0)),
        out_shape=jax.ShapeDtypeStruct((B * T, D), F32),
        compiler_params=_cparams(2),
        name="out_proj_residual_t",
    )(a_t, w, g.reshape(1, D), res)


def cover_matrix_np(n_blk, n_selb, rows, cols):
    i = np.arange(rows)[:, None]
    j = np.arange(cols)[None, :]
    m = (i * CMP_STRIDE < (j + 1) * SEL_BLOCK) & (i * CMP_STRIDE + CMP_BLOCK > j * SEL_BLOCK) & (i < n_blk) & (j < n_selb)
    return m.astype(np.float32)


def _even_dec_kernel(pt_ref, *refs, pps, n_steps, lam_init):
    kv_refs = refs[0:pps]
    lf_refs = refs[pps:2 * pps]
    (qt_ref, cn_ref, tbt_ref, ut_ref, kn_ref, vn_ref, bo_ref, lp_ref, sg_ref,
     o_ref, m_ref, l_ref, acc_ref, carry_ref) = refs[2 * pps:]
    s = pl.program_id(1)
    n_maps, width = qt_ref.shape
    n_fox = lf_refs[0].shape[0]

    @pl.when(s == 0)
    def _():
        m_ref[...] = jnp.full_like(m_ref, NEG)
        l_ref[...] = jnp.zeros_like(l_ref)
        acc_ref[...] = jnp.zeros_like(acc_ref)
        carry_ref[...] = jnp.zeros_like(carry_ref)

    qt = qt_ref[...]
    is_fox = lax.broadcasted_iota(I32, (n_maps, 1), 0) < n_fox
    pad = jnp.zeros((n_maps - n_fox, PAGE_SIZE), F32)
    lfts = jnp.concatenate([jnp.concatenate([lf_refs[i][...], pad], axis=0) for i in range(pps)], axis=0)
    within = _dot_exact_rhs01(lfts, ut_ref[...])
    totals = jnp.sum(lfts, axis=-1, keepdims=True)
    run = carry_ref[...]
    later = [None] * pps
    for i in reversed(range(pps)):
        later[i] = run
        run = run + totals[i * n_maps:(i + 1) * n_maps]
    carry_ref[...] = run
    cn = cn_ref[...]
    sts = []
    for i in range(pps):
        kpg = kv_refs[i][0].reshape(width, PAGE_SIZE).astype(BF16)
        suffix = within[i * n_maps:(i + 1) * n_maps] + later[i] + cn
        sts.append(_dot(qt, kpg) + jnp.where(is_fox, suffix, tbt_ref[i]))
    st = jnp.concatenate(sts, axis=1)
    m = m_ref[...]
    m_new = jnp.maximum(m, jnp.max(st, axis=-1, keepdims=True))
    alpha = jnp.exp(m - m_new)
    p = jnp.exp(st - m_new)
    l_ref[...] = alpha * l_ref[...] + jnp.sum(p, axis=-1, keepdims=True)
    pb = p.astype(BF16)
    pv = None
    for i in range(pps):
        vpg = kv_refs[i][1].reshape(width, PAGE_SIZE).astype(BF16)
        d = _dot_nt(pb[:, i * PAGE_SIZE:(i + 1) * PAGE_SIZE], vpg)
        pv = d if pv is None else pv + d
    acc_ref[...] = alpha * acc_ref[...] + pv
    m_ref[...] = m_new

    @pl.when(s == n_steps - 1)
    def _():
        kn = kn_ref[...].astype(BF16).astype(F32)
        vn = vn_ref[...].astype(BF16).astype(F32)
        s_own = jnp.sum(qt.astype(F32) * kn, axis=-1, keepdims=True) + bo_ref[...]
        m = m_ref[...]
        m_all = jnp.maximum(m, s_own)
        a = jnp.exp(m - m_all)
        e_own = jnp.exp(s_own - m_all)
        l_all = a * l_ref[...] + e_own
        o = (a * acc_ref[...] + e_own * vn) / l_all
        r = lax.broadcasted_iota(I32, (n_maps, width), 0)
        cidx = lax.broadcasted_iota(I32, (n_maps, width), 1)
        d_fox = n_fox * HEAD_DIM
        fox_sel = jnp.logical_and(cidx < d_fox, r == cidx // HEAD_DIM)
        dh = (cidx - d_fox) // LANES
        d1_sel = jnp.logical_and(cidx >= d_fox, r == n_fox + 2 * dh)
        d2_sel = jnp.logical_and(cidx >= d_fox, r == n_fox + 2 * dh + 1)
        o_f = jnp.sum(jnp.where(fox_sel, o, 0.0), axis=0, keepdims=True)
        a1 = jnp.sum(jnp.where(d1_sel, o, 0.0), axis=0, keepdims=True)
        a2 = jnp.sum(jnp.where(d2_sel, o, 0.0), axis=0, keepdims=True)
        lam = _lambda_value(lp_ref[...], lam_init)
        o_d = a1 - lam * a2
        for tl in range(width // LANES):
            sl = slice(tl * LANES, (tl + 1) * LANES)
            if tl * LANES < d_fox:
                o_ref[:, sl] = o_f[:, sl]
            else:
                x = o_d[:, sl]
                ms = jnp.mean(x * x, axis=-1, keepdims=True)
                o_ref[:, sl] = x * lax.rsqrt(ms + EPS) * sg_ref[...] * (1.0 - lam_init)


def even_decode(pool_t, lf_t, li, pt_flat, n_pages, qt, cn, tbt, kn, vn, bo, lam_p, subln_g, lam_init, pps):
    DB, n_maps, width = qt.shape
    n_fox = lf_t.shape[2]
    n_steps = n_pages // pps
    ut = jnp.asarray(np.tril(np.ones((PAGE_SIZE, PAGE_SIZE), np.float32), -1), BF16)

    def page(i):
        return lambda b, s, pt: pt[b * n_pages + (n_steps - 1 - s) * pps + i]

    kv_block = (None, None, 2, n_maps, HEAD_DIM, PAGE_SIZE)
    kv_specs = [pl.BlockSpec(kv_block, lambda b, s, pt, f=page(i): (li, f(b, s, pt), 0, 0, 0, 0)) for i in range(pps)]
    lf_specs = [pl.BlockSpec((None, None, n_fox, PAGE_SIZE), lambda b, s, pt, f=page(i): (li, f(b, s, pt), 0, 0)) for i in range(pps)]
    per_b = lambda shape: pl.BlockSpec((None,) + shape, lambda b, s, pt: (b,) + (0,) * len(shape))
    const = lambda shape: pl.BlockSpec(shape, lambda b, s, pt: (0,) * len(shape))
    grid_spec = pltpu.PrefetchScalarGridSpec(
        num_scalar_prefetch=1,
        grid=(DB, n_steps),
        in_specs=kv_specs + lf_specs + [
            per_b((n_maps, width)), per_b((n_maps, 1)),
            pl.BlockSpec((pps, n_maps, PAGE_SIZE), lambda b, s, pt: (n_steps - 1 - s, 0, 0)),
            const((PAGE_SIZE, PAGE_SIZE)),
            per_b((1, width)), per_b((1, width)),
            const((n_maps, 1)), const((4, HEAD_DIM)), const((1, LANES))],
        out_specs=per_b((1, width)),
        scratch_shapes=[pltpu.VMEM((n_maps, 1), F32), pltpu.VMEM((n_maps, 1), F32), pltpu.VMEM((n_maps, width), F32),
                        pltpu.VMEM((n_maps, 1), F32)])
    kern = functools.partial(_even_dec_kernel, pps=pps, n_steps=n_steps, lam_init=lam_init)
    return pl.pallas_call(
        kern, grid_spec=grid_spec,
        out_shape=jax.ShapeDtypeStruct((DB, 1, width), F32),
        compiler_params=_cparams(2),
        name="even_decode",
    )(pt_flat, *([pool_t] * pps), *([lf_t] * pps), qt, cn, tbt, ut, kn, vn, bo, lam_p, subln_g.reshape(1, LANES))


def _odd_cmp_kernel(pt_ref, *refs, pps, n_steps, n_selb, n_top, t_pos):
    r_refs = refs[0:pps]
    (w0_ref, w1_ref, g_ref, b_ref, w0t_ref, w1t_ref, qz_ref, cb_ref, cov_ref, oc_ref, idx_ref, p0, p1) = refs[pps:]
    s = pl.program_id(1)
    width = w0_ref.shape[0]
    half = width // 2
    spp = PAGE_SIZE // CMP_STRIDE
    w0 = w0_ref[...]
    w1 = w1_ref[...]
    g = g_ref[...]
    for i in range(pps):
        base = pl.multiple_of((s * pps + i) * spp, spp)
        for fc in range(width // LANES):
            cols = slice(fc * LANES, (fc + 1) * LANES)
            x = r_refs[i][fc // 2, 2 * (fc % 2):2 * (fc % 2) + 2].reshape(LANES, PAGE_SIZE)
            y = jnp.concatenate([x * w0[cols], x * w1[cols]], axis=0).astype(BF16)
            sums = _dot_nt(g, y)
            p0[pl.ds(base, spp), cols] = sums[0:spp, 0:LANES]
            p1[pl.ds(base, spp), cols] = sums[0:spp, LANES:]

    @pl.when(s == n_steps - 1)
    def _():
        nb = p0.shape[0]
        z = _silu(p0[...] + pltpu.roll(p1[...], nb - 1, 0) + b_ref[...])
        ck = _dot(z[:, :half].astype(BF16), w0t_ref[...]).astype(BF16)
        cv = _dot(z[:, half:].astype(BF16), w1t_ref[...]).astype(BF16)
        st = _dot_nt(qz_ref[...], ck) + cb_ref[...]
        n_idx = lax.broadcasted_iota(I32, (1, nb), 1)
        valid = n_idx * CMP_STRIDE + (CMP_BLOCK - 1) <= t_pos
        sm = jnp.where(valid, st, NEG)
        m = jnp.max(sm, axis=-1, keepdims=True)
        e = jnp.where(valid, jnp.exp(sm - m), 0.0)
        l = jnp.sum(e, axis=-1, keepdims=True)
        p = e / jnp.where(l > 0.0, l, 1.0)
        oc_ref[...] = _dot(p.astype(BF16), cv)
        gqa = p.shape[0] // 4
        rows = [jnp.sum(p[g * gqa:(g + 1) * gqa], axis=0, keepdims=True) for g in range(4)]
        psum = jnp.concatenate(rows + rows, axis=0)
        score = _dot_exact_rhs01(psum, cov_ref[...])
        nsp = score.shape[1]
        jb = lax.broadcasted_iota(I32, (1, nsp), 1)
        cur = t_pos // SEL_BLOCK
        forced = jnp.logical_or(jb == 0, jnp.logical_or(jb == cur, jb == cur - 1))
        score = jnp.where(forced, FORCE, score)
        score = jnp.where(jb * SEL_BLOCK <= t_pos, score, NEG)
        score = jnp.where(jb < n_selb, score, PAD_SCORE)
        jf = jb.astype(F32)
        slot = lax.broadcasted_iota(I32, (1, LANES), 1)
        picks = jnp.zeros((8, LANES), F32)
        for it in range(n_top):
            mx = jnp.max(score, axis=-1, keepdims=True)
            ix = jnp.min(jnp.where(score == mx, jf, 1e9), axis=-1, keepdims=True)
            picks = jnp.where(slot == it, ix, picks)
            score = jnp.where(jf == ix, PAD_SCORE, score)
        idx_ref[...] = picks.astype(I32)


def odd_decode_cmp(pool_t, li, pt_flat, n_pages, w0, w1, b_row, w0bd, w1bd, qz16, cbias, cover, n_selb, t_pos, pps):
    DB = qz16.shape[0]
    width = w0.shape[0]
    half = width // 2
    n_steps = n_pages // pps
    nb = n_pages * PAGE_SIZE // CMP_STRIDE
    nsp = cover.shape[1]
    spp = PAGE_SIZE // CMP_STRIDE
    gsum = np.zeros((16, PAGE_SIZE), np.float32)
    gsum[np.arange(PAGE_SIZE) // CMP_STRIDE, np.arange(PAGE_SIZE)] = 1.0
    gsum = jnp.asarray(gsum, BF16)
    r_specs = [pl.BlockSpec((None, None, 2, 4, HEAD_DIM, PAGE_SIZE),
                            lambda b, s, pt, i=i: (li, pt[b * n_pages + s * pps + i], 0, 0, 0, 0)) for i in range(pps)]
    per_b = lambda shape: pl.BlockSpec((None,) + shape, lambda b, s, pt: (b,) + (0,) * len(shape))
    const = lambda shape: pl.BlockSpec(shape, lambda b, s, pt: (0,) * len(shape))
    grid_spec = pltpu.PrefetchScalarGridSpec(
        num_scalar_prefetch=1,
        grid=(DB, n_steps),
        in_specs=r_specs + [const((width, LANES)), const((width, LANES)), const((16, PAGE_SIZE)), const((1, width)),
                            const((half, half)), const((half, half)),
                            per_b((16, half)), const((16, nb)), const((nb, nsp))],
        out_specs=[per_b((16, half)), per_b((8, LANES))],
        scratch_shapes=[pltpu.VMEM((nb, width), F32), pltpu.VMEM((nb, width), F32)])
    kern = functools.partial(_odd_cmp_kernel, pps=pps, n_steps=n_steps, n_selb=n_selb, n_top=min(N_SEL, n_selb), t_pos=t_pos)
    return pl.pallas_call(
        kern, grid_spec=grid_spec,
        out_shape=[jax.ShapeDtypeStruct((DB, 16, half), F32), jax.ShapeDtypeStruct((DB, 8, LANES), I32)],
        compiler_params=_cparams(2),
        name="odd_decode_cmp",
    )(pt_flat, *([pool_t] * pps), w0, w1, gsum, b_row, w0bd, w1bd, qz16, cbias, cover)


def _odd_sel_kernel(idx_ref, pt_ref, *refs, n_top, n_past_blk, bpp, t_pos):
    kv_refs = refs[0:n_top]
    (q_ref, sn_ref, tb_ref, wkv_ref, wn_ref, tw_ref, tw0_ref, oc_ref, gt_ref, o_ref) = refs[n_top:]
    b = pl.program_id(0)
    g = pl.program_id(1)
    q = q_ref[...]
    qf = q.astype(F32)
    lane = lax.broadcasted_iota(I32, (1, PAGE_SIZE), 1)
    bias0 = tw0_ref[:, 0:1]

    logits, vals = [], []
    n_new = jnp.zeros((), I32)
    for k in range(n_top):
        blk = idx_ref[(b * 4 + g) * n_top + k]
        is_new = blk >= n_past_blk
        n_new = n_new + is_new.astype(I32)
        page = jnp.minimum(blk // bpp, n_past_blk // bpp - 1)
        bias = tb_ref[pl.ds(pl.multiple_of((page * 4 + g) * 8, 8), 8), :]
        s = _dot(q, kv_refs[k][0].astype(BF16)) + bias
        pos = page * PAGE_SIZE + lane
        valid = jnp.logical_and(jnp.logical_and(pos // SEL_BLOCK == blk, pos <= t_pos), jnp.logical_not(is_new))
        logits.append(jnp.where(valid, s, NEG))
        vals.append(kv_refs[k][1].astype(BF16))
    has_new = n_new > 0
    kn = sn_ref[0:1, :].astype(BF16).astype(F32)
    vn = sn_ref[1:2, :].astype(BF16).astype(F32)
    s_new = jnp.where(has_new, jnp.sum(qf * kn, axis=-1, keepdims=True) + bias0, NEG)
    m = s_new
    for s in logits:
        m = jnp.maximum(m, s.max(axis=-1, keepdims=True))
    p_new = jnp.where(has_new, jnp.exp(s_new - m), 0.0)
    l = p_new
    acc = p_new * vn
    for s, vt in zip(logits, vals):
        p = jnp.where(s > 0.5 * NEG, jnp.exp(s - m), 0.0)
        l = l + jnp.sum(p, axis=-1, keepdims=True)
        acc = acc + _dot_nt(p.astype(BF16), vt)
    o_s = acc / jnp.where(l > 0.0, l, 1.0)

    sw = _dot(q, wkv_ref[0].astype(BF16)) + tw_ref[...]
    kwn = wn_ref[0:1, :].astype(BF16).astype(F32)
    vwn = wn_ref[1:2, :].astype(BF16).astype(F32)
    sw_new = jnp.sum(qf * kwn, axis=-1, keepdims=True) + bias0
    mw = jnp.maximum(jnp.max(sw, axis=-1, keepdims=True), sw_new)
    pw = jnp.exp(sw - mw)
    pw_new = jnp.exp(sw_new - mw)
    lw = jnp.sum(pw, axis=-1, keepdims=True) + pw_new
    o_w = (_dot_nt(pw.astype(BF16), wkv_ref[1].astype(BF16)) + pw_new * vwn) / lw

    gates = 1.0 / (1.0 + jnp.exp(-gt_ref[...]))
    o_ref[...] = gates[:, 0:1] * oc_ref[...] + gates[:, 1:2] * o_s + gates[:, 2:3] * o_w


def odd_decode_sel(pool_t, li, idx_flat, pt_flat, n_pages, win_t, qg, s_new, tbp, w_new, tw, tw0, oc_g, graw, n_top, t_pos):
    DB = qg.shape[0]
    bpp = PAGE_SIZE // SEL_BLOCK
    wb = win_t.shape[-1]
    n_past_blk = n_pages * bpp

    def blk_spec(k):
        def imap(b, g, idx, pt):
            blk = idx[(b * 4 + g) * n_top + k]
            page = pt[b * n_pages + jnp.minimum(blk // bpp, n_pages - 1)]
            return (li, page, 1, g, 0, 0)
        return pl.BlockSpec((None, None, 2, None, HEAD_DIM, PAGE_SIZE), imap)

    per_bg = lambda shape: pl.BlockSpec((None, None) + shape, lambda b, g, idx, pt: (b, g) + (0,) * len(shape))
    grid_spec = pltpu.PrefetchScalarGridSpec(
        num_scalar_prefetch=2,
        grid=(DB, 4),
        in_specs=[blk_spec(k) for k in range(n_top)] + [
            per_bg((8, HEAD_DIM)), per_bg((2, HEAD_DIM)),
            pl.BlockSpec(tbp.shape, lambda b, g, idx, pt: (0, 0)),
            pl.BlockSpec((None, None, 2, None, HEAD_DIM, wb), lambda b, g, idx, pt: (li, b, 0, g, 0, 0)),
            per_bg((2, HEAD_DIM)),
            pl.BlockSpec((8, wb), lambda b, g, idx, pt: (g, 0)),
            pl.BlockSpec((8, LANES), lambda b, g, idx, pt: (g, 0)),
            per_bg((8, HEAD_DIM)), per_bg((8, LANES))],
        out_specs=per_bg((8, HEAD_DIM)))
    kern = functools.partial(_odd_sel_kernel, n_top=n_top, n_past_blk=n_past_blk, bpp=bpp, t_pos=t_pos)
    return pl.pallas_call(
        kern, grid_spec=grid_spec,
        out_shape=jax.ShapeDtypeStruct((DB, 4, 8, HEAD_DIM), F32),
        compiler_params=_cparams(2),
        name="odd_decode_sel",
    )(idx_flat, pt_flat, *([pool_t] * n_top), qg, s_new, tbp, win_t, w_new, tw, tw0, oc_g, graw)


def _pad_cols(w, n):
    return jnp.pad(w, ((0, 0), (0, n - w.shape[1])))


def _pad_rows8(x4):
    pad = [(0, 0)] * x4.ndim
    pad[-2] = (0, 4)
    return jnp.pad(x4, pad)


def even_layer(hp, hs, pool_kv, pool_lf, li, page_table, g0, w_in, b_f, lam_p, subln_g, rel_table, layer):
    B, T, D = hp.shape
    DB, S, _ = hs.shape
    assert S == 1
    n_fox, d_fox = 8, 512
    lam_init = 0.8 - 0.6 * math.exp(-0.3 * layer)
    cuts = np.cumsum((d_fox,) * 6)
    q_f, k_f, v_f, q_d, k_d, v_d, f_w = jnp.split(w_in, [int(c) for c in cuts], axis=1)
    w_q = jnp.concatenate([q_f, q_d, _pad_cols(f_w, LANES)], axis=1).astype(BF16)
    w_kv = jnp.concatenate([k_f, k_d, v_f, v_d], axis=1).astype(BF16)
    bf_pad = jnp.pad(b_f.astype(F32), (0, LANES - n_fox)).reshape(1, LANES)
    table4 = rel_table[:, :4]

    q_arr = norm_proj(hp.reshape(B * T, D), g0, w_q, w_q.shape[1]).reshape(B, T, -1)
    kv_t = norm_proj_t(hp, g0, w_kv.T, 2048)
    tq = min(T, 256)
    tk = min(T, 1024)
    lf_pad, cum_pad = logf_cumsum(q_arr, 8, bf_pad, tq)
    cum_pairs = cum_pad[:, :, :n_fox].transpose(0, 2, 1).reshape(B, 4, 2, T)
    toe_t = LOG2E * bias_table(table4, toeplitz_dist(tq, T // tq).transpose(0, 2, 1))
    attn_p = even_prompt_attention_t(q_arr, kv_t, cum_pairs, toe_t, lam_p, subln_g, lam_init, tq, tk)
    ekv_p = kv_t.reshape(B, 2, 16, HEAD_DIM, T).transpose(0, 4, 1, 2, 3)
    elf_p = lf_pad[:, :, :n_fox]

    n_pages = page_table.shape[1]
    past = n_pages * PAGE_SIZE
    xs = hs.reshape(DB, D)
    qs_arr = norm_proj(xs, g0, w_q, 384)
    kvs_arr = norm_proj(xs, g0, w_kv, 512)
    lfs_pad, _ = logf_cumsum(qs_arr.reshape(1, DB, -1), 8, bf_pad, DB)
    lf_new = lfs_pad[0, :, :n_fox]
    seg = np.zeros((16, 1024), np.float32)
    for m_ in range(16):
        seg[m_, m_ * HEAD_DIM:(m_ + 1) * HEAD_DIM] = 1.0
    qt = (qs_arr[:, None, :1024] * SCALE * seg[None]).astype(BF16)
    cn = jnp.pad(lf_new, ((0, 0), (0, 8)))[:, :, None]
    pos = jnp.arange(past).reshape(n_pages, PAGE_SIZE)
    rb = jnp.repeat(bias_table(table4, past - pos), 2, axis=0)
    tbt = jnp.concatenate([jnp.zeros((n_pages, 8, PAGE_SIZE), F32), rb.transpose(1, 0, 2)], axis=1)
    bo = jnp.concatenate([jnp.zeros((8,), F32), jnp.repeat(rel_table[0, :4], 2)]).reshape(16, 1)
    pool_t = pool_kv.transpose(0, 1, 3, 4, 5, 2)
    lf_t = pool_lf.transpose(0, 1, 3, 2)
    attn_s = even_decode(pool_t, lf_t, li, page_table.reshape(-1), n_pages, qt, cn, tbt,
                         kvs_arr[:, None, :1024], kvs_arr[:, None, 1024:], bo, lam_p, subln_g, lam_init,
                         pps=math.gcd(n_pages, 16))
    ekv_s = kvs_arr.reshape(DB, 1, 2, 16, HEAD_DIM)
    elf_s = lf_new.reshape(DB, 1, n_fox)
    return attn_p, attn_s.reshape(DB, -1), ekv_p, ekv_s, elf_p, elf_s


def odd_layer(hp, hs, pool, win_state, li, page_table, g0, w_in, b_gate, w_pos, b_cmp, w_cout, rel_table):
    B, T, D = hp.shape
    DB, S, _ = hs.shape
    assert S == 1
    d_q, d_kv = 1024, 512
    w_q, w_c, w_s, w_w, w_g = jnp.split(w_in, [d_q, d_q + d_kv, d_q + 2 * d_kv, d_q + 3 * d_kv], axis=1)
    n_gate = 12
    w_g4 = jnp.concatenate([_pad_cols(w_g[:, g * n_gate:(g + 1) * n_gate], LANES) for g in range(4)], axis=1)
    w_qg = jnp.concatenate([w_q, w_g4], axis=1).astype(BF16)
    w_cs = jnp.concatenate([w_c, w_s], axis=1).astype(BF16)
    w_w16 = w_w.astype(BF16)
    bg = b_gate.astype(F32)
    bg_pad = jnp.pad(bg.reshape(4, 1, n_gate), ((0, 0), (0, 0), (0, LANES - n_gate)))
    wp = w_pos.reshape(CMP_BLOCK, d_kv)
    w0, w1 = _pool_weights_t(wp)
    b_col = b_cmp.reshape(d_kv, 1).astype(F32)
    eye4 = jnp.eye(4, dtype=F32)
    w0t = jnp.kron(eye4, w_cout[0].T).astype(BF16)
    w1t = jnp.kron(eye4, w_cout[1].T).astype(BF16)
    table = rel_table

    d_g = d_kv // 4
    dup = lambda w: jnp.concatenate([w[:, g * HEAD_DIM:(g + 1) * HEAD_DIM] for g in range(4) for _ in range(2)], axis=1)
    w_qk = jnp.concatenate([w_q * (SCALE * LOG2E), dup(w_s[:, :2 * d_g]), dup(w_w[:, :2 * d_g])], axis=1).astype(BF16)
    w_g16 = jnp.concatenate([_pad_cols(w_g[:, g * n_gate:(g + 1) * n_gate], 16) for g in range(4)], axis=1)
    w_wg_t = jnp.concatenate([w_w, w_g16], axis=1).T.astype(BF16)
    bg_col = jnp.pad(bg.reshape(4, n_gate, 1), ((0, 0), (0, 16 - n_gate), (0, 0)))
    qk_arr = norm_proj(hp.reshape(B * T, D), g0, w_qk, 2048, BF16).reshape(B, T, -1)
    cs_t = norm_proj_t(hp, g0, w_cs.T, 1024)
    wg_t = norm_proj_t(hp, g0, w_wg_t, w_wg_t.shape[0])
    ck_t, cv_t = cmp_build(cs_t, w0, w1, b_col, w0t, w1t)
    nb = T // CMP_STRIDE
    n_blk = nb - 1
    n_selb = T // SEL_BLOCK
    tq = LANES
    tk = min(T, 512)
    t_all = jnp.arange(T)
    e_pos = jnp.arange(nb) * CMP_STRIDE + CMP_BLOCK - 1
    cbias_t = LOG2E * bias_table(table, t_all[None, :] - e_pos[:, None])
    toe_t = LOG2E * bias_table(table, toeplitz_dist(tq, T // tq).transpose(0, 2, 1))
    cover_t = jnp.asarray(cover_matrix_np(n_blk, n_selb, nb, LANES).T, BF16)
    expand_t = np.zeros((T, LANES), np.float32)
    expand_t[np.arange(T), np.arange(T) // SEL_BLOCK] = 1.0
    attn_p = nsa_prompt_attention_t(qk_arr, ck_t, cv_t, cs_t, wg_t, bg_col, cbias_t, toe_t, cover_t,
                                    jnp.asarray(expand_t, BF16), tq, tk)
    okv_p = cs_t.reshape(B, 4, 4, HEAD_DIM, T).transpose(0, 4, 1, 2, 3)
    wn = min(WINDOW, T)
    win_p = wg_t[:, :d_kv, T - wn:].reshape(B, 2, 4, HEAD_DIM, wn).transpose(0, 4, 1, 2, 3)

    n_pages = page_table.shape[1]
    past = n_pages * PAGE_SIZE
    xs = hs.reshape(DB, D)
    qgs = norm_proj(xs, g0, w_qg, 512)
    css = norm_proj(xs, g0, w_cs, 512)
    wns = norm_proj(xs, g0, w_w16, 512)
    pt_flat = page_table.reshape(-1)
    nbs = past // CMP_STRIDE
    n_past_blk = past // SEL_BLOCK
    n_selb_s = n_past_blk + 1
    nsp = -(-n_selb_s // LANES) * LANES
    qs = qgs[:, :d_q]
    segz = np.zeros((16, 256), np.float32)
    for h_ in range(16):
        segz[h_, (h_ // 4) * HEAD_DIM:(h_ // 4 + 1) * HEAD_DIM] = 1.0
    qz16 = (jnp.tile(qs.reshape(DB, 16, 1, HEAD_DIM), (1, 1, 4, 1)).reshape(DB, 16, 256) * SCALE * segz[None]).astype(BF16)
    e_pos_s = jnp.arange(nbs) * CMP_STRIDE + CMP_BLOCK - 1
    cbias_s = bias_table(table, past - e_pos_s)
    cover_s = jnp.asarray(cover_matrix_np(nbs, n_selb_s, nbs, nsp), BF16)
    pool_t = pool.transpose(0, 1, 3, 4, 5, 2)
    n_top = min(N_SEL, n_selb_s)
    oc16, idx8 = odd_decode_cmp(pool_t, li, pt_flat, n_pages, w0, w1, b_col.reshape(1, d_kv), w0t.T, w1t.T, qz16, cbias_s,
                                cover_s, n_selb_s, past, pps=math.gcd(n_pages, 16))
    idx_flat = idx8[:, :4, :n_top].reshape(-1)
    oc4 = oc16.reshape(DB, 4, 4, 4, HEAD_DIM)[:, np.arange(4), :, np.arange(4)].transpose(1, 0, 2, 3)
    qg = _pad_rows8(qs.reshape(DB, 4, 4, HEAD_DIM) * SCALE).astype(BF16)
    pos_p = jnp.arange(past).reshape(n_pages, PAGE_SIZE)
    tbp = bias_table(table, past - pos_p).reshape(4, 4, n_pages, PAGE_SIZE).transpose(2, 0, 1, 3)
    tbp = _pad_rows8(tbp).reshape(n_pages * 32, PAGE_SIZE)
    wb = win_state.shape[2]
    tw = _pad_rows8(bias_table(table, wb - jnp.arange(wb)).reshape(4, 4, wb)).reshape(32, wb)
    tw0 = _pad_rows8(jnp.broadcast_to(table[0].reshape(4, 4, 1), (4, 4, LANES))).reshape(32, LANES)
    g48 = jnp.concatenate([qgs[:, d_q + g * LANES:d_q + g * LANES + n_gate] for g in range(4)], axis=1) + bg[None]
    graw = jnp.pad(_pad_rows8(g48.reshape(DB, 4, 4, 3)), ((0, 0), (0, 0), (0, 0), (0, LANES - 3)))
    win_t = win_state.transpose(0, 1, 3, 4, 5, 2)
    s_new = css[:, d_kv:].reshape(DB, 2, 4, HEAD_DIM).transpose(0, 2, 1, 3)
    w_new = wns.reshape(DB, 2, 4, HEAD_DIM).transpose(0, 2, 1, 3)
    o4 = odd_decode_sel(pool_t, li, idx_flat, pt_flat, n_pages, win_t, qg, s_new, tbp, w_new, tw, tw0,
                        _pad_rows8(oc4), graw, n_top, past)
    attn_s = o4[:, :, :4, :].reshape(DB, d_q)
    okv_s = css.reshape(DB, 1, 4, 4, HEAD_DIM)
    keys_t = jnp.concatenate([win_t[li], wns.reshape(DB, 2, 4, HEAD_DIM, 1)], axis=-1)
    keep = min(WINDOW, wb + 1)
    win_s = keys_t[..., wb + 1 - keep:].transpose(0, 4, 1, 2, 3)
    return attn_p, attn_s, okv_p, okv_s, win_p, win_s


def kernel(x_prompt, x_sample, cache_even_kv, cache_even_logf, cache_odd_kv, state_odd_win, page_table, rel_table, norm_g, w_even_in, b_even_f, diff_lambda, diff_subln_g, w_even_out, w_odd_in, b_odd_gate, w_cmp_pos, b_cmp, w_cmp_out, w_odd_out, w_ffn_in, w_ffn_out):
    B, T, D = x_prompt.shape
    DB, S, _ = x_sample.shape
    depth = norm_g.shape[0]
    hp = x_prompt.reshape(B * T, D)
    hs = x_sample.reshape(DB * S, D)
    ekv_p, ekv_s, elf_p, elf_s = [], [], [], []
    okv_p, okv_s, win_p, win_s = [], [], [], []
    for layer in range(depth):
        g = norm_g[layer]
        li = layer // 2
        hp3, hs3 = hp.reshape(B, T, D), hs.reshape(DB, S, D)
        if layer % 2 == 0:
            a_p, a_s, kvp, kvs, lfp, lfs = even_layer(
                hp3, hs3, cache_even_kv, cache_even_logf, li, page_table, g[0], w_even_in[li], b_even_f[li],
                diff_lambda[li].astype(F32), diff_subln_g[li].astype(F32), rel_table, layer)
            w_o = w_even_out[li].astype(BF16)
            ekv_p.append(kvp); ekv_s.append(kvs); elf_p.append(lfp); elf_s.append(lfs)
        else:
            a_p, a_s, kvp, kvs, wp_, ws_ = odd_layer(
                hp3, hs3, cache_odd_kv, state_odd_win, li, page_table, g[0], w_odd_in[li], b_odd_gate[li],
                w_cmp_pos[li], b_cmp[li], w_cmp_out[li], rel_table)
            w_o = w_odd_out[li].astype(BF16)
            okv_p.append(kvp); okv_s.append(kvs); win_p.append(wp_); win_s.append(ws_)
        hp = (out_proj_residual if a_p.ndim == 2 else out_proj_residual_t)(a_p, w_o, g[1], hp)
        hs = out_proj_residual(a_s, w_o, g[1], hs)
        w1, w2 = ffn_weights(w_ffn_in[layer], w_ffn_out[layer], 256)
        hp = ffn_residual(hp, g[2], w1, w2, g[3])
        hs = ffn_residual(hs, g[2], w1, w2, g[3])
    return (hp.reshape(B, T, D), hs.reshape(DB, S, D), jnp.stack(ekv_p), jnp.stack(ekv_s), jnp.stack(elf_p), jnp.stack(elf_s),
            jnp.stack(okv_p), jnp.stack(okv_s), jnp.stack(win_p), jnp.stack(win_s))
```
